```python
import math
import jax, jax.numpy as jnp
from jax import lax
import numpy as np

D_MODEL = 2048
BATCH = 8
SEQ = 2048
DEPTH = 1

RWKV_HEAD = 64
RWKV_WIDTH = D_MODEL // 2
RWKV_HEADS = RWKV_WIDTH // RWKV_HEAD
DECAY_LORA = max(32, int(round(RWKV_WIDTH ** 0.5 * 1.8 / 32)) * 32)
AAA_LORA = max(32, int(round(RWKV_WIDTH ** 0.5 * 1.8 / 32)) * 32)
GATE_LORA = max(32, int(round(RWKV_WIDTH ** 0.8 * 0.6 / 32)) * 32)
RWKV_COLS = 3 * RWKV_WIDTH + DECAY_LORA + AAA_LORA + GATE_LORA

DIFF_HEAD = 64
DIFF_VDIM = 2 * DIFF_HEAD
DIFF_WIDTH = D_MODEL - RWKV_WIDTH
DIFF_HEADS = DIFF_WIDTH // DIFF_VDIM
DIFF_QK_COLS = DIFF_HEADS * 2 * DIFF_HEAD
DIFF_COLS = 2 * DIFF_QK_COLS + DIFF_WIDTH
IN_COLS = RWKV_COLS + DIFF_COLS
Q_BLOCK = 128

N_EXPERTS = 32
TOP_K = 4
D_FF = D_MODEL
SWIGLU_LIMIT = 7.0
SWIGLU_ALPHA = 1.702
MOE_BLOCK = 128

LN_EPS = 1e-5
GN_EPS = RWKV_HEAD * 1e-5
RMS_EPS = 1e-5
NEG_BIG = -1e30
DEEPNORM_ALPHA = (2.0 * DEPTH) ** 0.25
DEEPNORM_BETA = (8.0 * DEPTH) ** -0.25

kernel_name = "hybrid_rwkv7_diffattn_moe_deepnorm"


def _split_cols(p, sizes):
    offs = [int(o) for o in np.cumsum(sizes)[:-1]]
    return jnp.split(p, offs, axis=-1)


def layer_norm(x, g, b):
    xf = x.astype(jnp.float32)
    mu = jnp.mean(xf, -1, keepdims=True)
    var = jnp.mean(jnp.square(xf - mu), -1, keepdims=True)
    y = (xf - mu) * lax.rsqrt(var + LN_EPS) * g.astype(jnp.float32) + b.astype(jnp.float32)
    return y.astype(x.dtype)


def token_shift(p, mu):
    prev = jnp.pad(p, ((0, 0), (1, 0), (0, 0)))[:, :-1]
    return p + (prev - p) * mu


def rwkv7_mix(p, mu, w0, w_up, a0, a_up, g_up, k_k, k_a, r_k, gn_g, gn_b):
    B, S, _ = p.shape
    H, N = RWKV_HEADS, RWKV_HEAD
    f32 = jnp.float32
    p = token_shift(p, mu)
    r, k, v, dw, da, dg = _split_cols(p, [RWKV_WIDTH] * 3 + [DECAY_LORA, AAA_LORA, GATE_LORA])
    w = (w0 + jnp.tanh(dw) @ w_up).astype(f32)
    w = -jax.nn.softplus(-w) - 0.5
    decay = jnp.exp(-jnp.exp(w))
    a = jax.nn.sigmoid((a0 + da @ a_up).astype(f32))
    g = (jax.nn.sigmoid(dg) @ g_up).astype(f32)
    hs = lambda t: t.astype(f32).reshape(B, S, H, N)
    r, k, v, a, decay = hs(r), hs(k), hs(v), hs(a), hs(decay)
    kk = k * k_k.astype(f32).reshape(H, N)
    kk = kk / jnp.maximum(jnp.sqrt(jnp.sum(kk * kk, -1, keepdims=True)), 1e-12)
    k = k * (1.0 + (a - 1.0) * k_a.astype(f32).reshape(H, N))

    def step(state, inp):
        r_t, w_t, k_t, v_t, a_t, b_t = inp
        sa = jnp.einsum('bhvk,bhk->bhv', state, a_t)
        state = (state * w_t[:, :, None, :] + sa[..., None] * b_t[:, :, None, :]
                 + v_t[..., None] * k_t[:, :, None, :])
        y_t = jnp.einsum('bhvk,bhk->bhv', state, r_t)
        return state, y_t

    xs = tuple(jnp.swapaxes(t, 0, 1) for t in (r, decay, k, v, -kk, kk * a))
    s0 = jnp.zeros((B, H, N, N), f32)
    _, y = lax.scan(step, s0, xs)
    y = jnp.swapaxes(y, 0, 1)
    mu_y = jnp.mean(y, -1, keepdims=True)
    var_y = jnp.mean(jnp.square(y - mu_y), -1, keepdims=True)
    y = ((y - mu_y) * lax.rsqrt(var_y + GN_EPS)).reshape(B, S, RWKV_WIDTH)
    y = y * gn_g.astype(f32) + gn_b.astype(f32)
    bonus = jnp.sum(r * k * r_k.astype(f32), -1, keepdims=True) * v
    out = (y + bonus.reshape(B, S, RWKV_WIDTH)) * g
    return out.astype(p.dtype)


def diff_attention(p, lq1, lk1, lq2, lk2, subln_g, lambda_init):
    B, S, _ = p.shape
    H, Dh, Vd = DIFF_HEADS, DIFF_HEAD, DIFF_VDIM
    f32 = jnp.float32
    q, k, v = _split_cols(p, [DIFF_QK_COLS, DIFF_QK_COLS, DIFF_WIDTH])
    q = q.reshape(B, S, H, 2, Dh)
    k = k.reshape(B, S, H, 2, Dh)
    v = v.reshape(B, S, H, Vd).astype(f32)
    lam = (jnp.exp(jnp.sum(lq1.astype(f32) * lk1.astype(f32)))
           - jnp.exp(jnp.sum(lq2.astype(f32) * lk2.astype(f32))) + lambda_init)
    nb = S // Q_BLOCK
    qb = jnp.moveaxis(q.reshape(B, nb, Q_BLOCK, H, 2, Dh), 1, 0)
    kpos = jnp.arange(S)
    scale = Dh ** -0.5

    def block(args):
        q_blk, start = args
        qpos = start + jnp.arange(Q_BLOCK)
        mask = kpos[None, :] <= qpos[:, None]
        s = jnp.einsum('bqhmd,bkhmd->bmhqk', q_blk, k).astype(f32) * scale
        s = jnp.where(mask, s, NEG_BIG)
        pr = jax.nn.softmax(s, axis=-1)
        attn = pr[:, 0] - lam * pr[:, 1]
        return jnp.einsum('bhqk,bkhd->bqhd', attn, v)

    o = lax.map(block, (qb, jnp.arange(nb) * Q_BLOCK))
    o = jnp.moveaxis(o, 0, 1).reshape(B, S, H, Vd)
    o = o * lax.rsqrt(jnp.mean(o * o, -1, keepdims=True) + RMS_EPS) * subln_g.astype(f32)
    o = o * (1.0 - lambda_init)
    return o.reshape(B, S, DIFF_WIDTH).astype(p.dtype)


def moe_ffn(h, w_router, b_router, w_gu, b_gu, w_dn, b_dn):
    B, S, D = h.shape
    n_tok = B * S
    nk = n_tok * TOP_K
    hf = h.reshape(n_tok, D)
    logits = (hf @ w_router + b_router).astype(jnp.float32)
    top_vals, top_idx = lax.top_k(logits, TOP_K)
    gates = jax.nn.softmax(top_vals, axis=-1)
    flat_e = top_idx.reshape(-1)
    flat_g = gates.reshape(-1)
    flat_t = jnp.arange(nk, dtype=jnp.int32) // TOP_K
    order = jnp.argsort(flat_e)
    se, st, sg = flat_e[order], flat_t[order], flat_g[order]
    counts = jnp.bincount(flat_e, length=N_EXPERTS)
    starts = jnp.cumsum(counts) - counts
    padded = ((counts + MOE_BLOCK - 1) // MOE_BLOCK) * MOE_BLOCK
    pends = jnp.cumsum(padded)
    pstarts = pends - padded
    dest = pstarts[se] + (jnp.arange(nk) - starts[se])
    n_blocks = (nk + MOE_BLOCK - 1) // MOE_BLOCK + N_EXPERTS
    n_rows = n_blocks * MOE_BLOCK
    row_tok = jnp.zeros((n_rows,), jnp.int32).at[dest].set(st)
    row_gate = jnp.zeros((n_rows,), jnp.float32).at[dest].set(sg)
    block_e = jnp.minimum(jnp.searchsorted(pends, jnp.arange(n_blocks) * MOE_BLOCK, side='right'),
                          N_EXPERTS - 1)
    xs = hf[row_tok].reshape(n_blocks, MOE_BLOCK, D)

    def expert_block(args):
        xb, e = args
        gu = xb @ w_gu[e] + b_gu[e]
        gate, up = gu[:, :D_FF], gu[:, D_FF:]
        gate = jnp.minimum(gate, SWIGLU_LIMIT)
        up = jnp.clip(up, -SWIGLU_LIMIT, SWIGLU_LIMIT)
        act = (up + 1.0) * (gate * jax.nn.sigmoid(SWIGLU_ALPHA * gate))
        return act @ w_dn[e] + b_dn[e]

    ys = lax.map(expert_block, (xs, block_e)).reshape(n_rows, D)
    y = jnp.zeros((n_tok, D), h.dtype).at[row_tok].add(ys * row_gate[:, None].astype(h.dtype))
    return y.reshape(B, S, D)


def setup_inputs(seed: int = 0) -> dict:
    key = jax.random.key(seed)
    ks = iter(jax.random.split(key, 40))
    f32 = jnp.float32

    def nrm(shape, scale):
        return jax.random.normal(next(ks), shape, f32) * scale

    L, D, C = DEPTH, D_MODEL, RWKV_WIDTH
    x = nrm((BATCH, SEQ, D), 1.0)
    col_scale = jnp.ones((IN_COLS,), f32)
    col_scale = col_scale.at[2 * C:3 * C].set(DEEPNORM_BETA)
    col_scale = col_scale.at[RWKV_COLS + 2 * DIFF_QK_COLS:].set(DEEPNORM_BETA)
    w_in = nrm((L, D, IN_COLS), D ** -0.5) * col_scale
    shift_mu = jax.random.uniform(next(ks), (L, RWKV_COLS), f32)
    w0 = jnp.linspace(-6.5, -1.5, C, dtype=f32)[None, :] + nrm((L, C), 0.1)
    w_up = nrm((L, DECAY_LORA, C), 0.1 * DECAY_LORA ** -0.5)
    a0 = nrm((L, C), 0.1)
    a_up = nrm((L, AAA_LORA, C), AAA_LORA ** -0.5)
    g_up = nrm((L, GATE_LORA, C), GATE_LORA ** -0.5)
    k_k = 0.85 + nrm((L, C), 0.02)
    k_a = 1.0 + nrm((L, C), 0.02)
    r_k = nrm((L, RWKV_HEADS, RWKV_HEAD), 0.1)
    gn_g = 1.0 + nrm((L, C), 0.02)
    gn_b = nrm((L, C), 0.01)
    lq1 = nrm((L, DIFF_HEAD), 0.1)
    lk1 = nrm((L, DIFF_HEAD), 0.1)
    lq2 = nrm((L, DIFF_HEAD), 0.1)
    lk2 = nrm((L, DIFF_HEAD), 0.1)
    subln_g = 1.0 + nrm((L, DIFF_VDIM), 0.02)
    w_out = nrm((L, D, D), D ** -0.5 * DEEPNORM_BETA)
    ln1_g = 1.0 + nrm((L, D), 0.02)
    ln1_b = nrm((L, D), 0.01)
    w_router = nrm((L, D, N_EXPERTS), D ** -0.5)
    b_router = nrm((L, N_EXPERTS), 0.01)
    w_gu = nrm((L, N_EXPERTS, D, 2 * D_FF), D ** -0.5 * DEEPNORM_BETA)
    b_gu = nrm((L, N_EXPERTS, 2 * D_FF), 0.01)
    w_dn = nrm((L, N_EXPERTS, D_FF, D), D_FF ** -0.5 * DEEPNORM_BETA)
    b_dn = nrm((L, N_EXPERTS, D), 0.01)
    ln2_g = 1.0 + nrm((L, D), 0.02)
    ln2_b = nrm((L, D), 0.01)
    return {"x": x, "w_in": w_in, "shift_mu": shift_mu, "w0": w0, "w_up": w_up,
            "a0": a0, "a_up": a_up, "g_up": g_up, "k_k": k_k, "k_a": k_a, "r_k": r_k,
            "gn_g": gn_g, "gn_b": gn_b, "lq1": lq1, "lk1": lk1, "lq2": lq2, "lk2": lk2,
            "subln_g": subln_g, "w_out": w_out, "ln1_g": ln1_g, "ln1_b": ln1_b,
            "w_router": w_router, "b_router": b_router, "w_gu": w_gu, "b_gu": b_gu,
            "w_dn": w_dn, "b_dn": b_dn, "ln2_g": ln2_g, "ln2_b": ln2_b}


def reference(x, w_in, shift_mu, w0, w_up, a0, a_up, g_up, k_k, k_a, r_k, gn_g, gn_b,
              lq1, lk1, lq2, lk2, subln_g, w_out, ln1_g, ln1_b,
              w_router, b_router, w_gu, b_gu, w_dn, b_dn, ln2_g, ln2_b):
    for l in range(DEPTH):
        lambda_init = 0.8 - 0.6 * math.exp(-0.3 * l)
        p = x @ w_in[l]
        p_rwkv, p_diff = p[..., :RWKV_COLS], p[..., RWKV_COLS:]
        h_rwkv = rwkv7_mix(p_rwkv, shift_mu[l], w0[l], w_up[l], a0[l], a_up[l], g_up[l],
                           k_k[l], k_a[l], r_k[l], gn_g[l], gn_b[l])
        h_diff = diff_attention(p_diff, lq1[l], lk1[l], lq2[l], lk2[l], subln_g[l], lambda_init)
        mix = jnp.concatenate([h_rwkv, h_diff], axis=-1) @ w_out[l]
        x = layer_norm(DEEPNORM_ALPHA * x + mix, ln1_g[l], ln1_b[l])
        ffn = moe_ffn(x, w_router[l], b_router[l], w_gu[l], b_gu[l], w_dn[l], b_dn[l])
        x = layer_norm(DEEPNORM_ALPHA * x + ffn, ln2_g[l], ln2_b[l])
    return x
```

```python
import functools
import math

import jax
import jax.numpy as jnp
from jax import lax
from jax.experimental import pallas as pl
from jax.experimental.pallas import tpu as pltpu

F32 = jnp.float32
BF16 = jnp.bfloat16

D_MODEL = 2048
RWKV_HEAD = 64
RWKV_WIDTH = 1024
RWKV_HEADS = 16
DECAY_LORA = 64
AAA_LORA = 64
GATE_LORA = 160
DIFF_HEAD = 64
DIFF_VDIM = 128
DIFF_HEADS = 8
DIFF_WIDTH = 1024
N_EXPERTS = 32
TOP_K = 4
D_FF = 2048
SWIGLU_LIMIT = 7.0
SWIGLU_ALPHA = 1.702
LN_EPS = 1e-5
GN_EPS = RWKV_HEAD * 1e-5
RMS_EPS = 1e-5
NEG_BIG = -1e30
DEPTH = 1
DEEPNORM_ALPHA = (2.0 * DEPTH) ** 0.25

LANES = 128
SUBLANES = 8
ROW_WORDS = D_MODEL // 2
ROW_TILE = ROW_WORDS // LANES
RWKV_GROUP = 256
RWKV_CHUNK = 64
LORA_COLS = 512
VMEM_LIMIT = 56 * 1024 * 1024
HIGH_HALF = -65536


def _cparams(sem, vmem=None):
    return pltpu.CompilerParams(dimension_semantics=sem, vmem_limit_bytes=vmem)


def _dot(a, b):
    return jnp.dot(a, b, preferred_element_type=F32)


def _dot_nt(a, b):
    return lax.dot_general(a, b, (((1,), (1,)), ((), ())), preferred_element_type=F32)


def _dot_tn(a, b):
    return lax.dot_general(a, b, (((0,), (0,)), ((), ())), preferred_element_type=F32)


def _split3(x):
    h = x.astype(BF16)
    r = x - h.astype(F32)
    m = r.astype(BF16)
    l = (r - m.astype(F32)).astype(BF16)
    return h, m, l


def _dot_exact_rhs(x, ones):
    h, m, l = _split3(x)
    return _dot(h, ones) + _dot(m, ones) + _dot(l, ones)


def _dot_exact_lhs(ones, x):
    h, m, l = _split3(x)
    return _dot(ones, h) + _dot(ones, m) + _dot(ones, l)


def _sigmoid(x):
    return 1.0 / (1.0 + jnp.exp(-x))


def _matmul_kernel(x_ref, w_ref, o_ref):
    o_ref[...] = _dot(x_ref[...], w_ref[...]).astype(o_ref.dtype)


def _matmul(x, w, out_dtype, tm, tn):
    m, k = x.shape
    n = w.shape[1]
    return pl.pallas_call(
        _matmul_kernel,
        grid=(m // tm, n // tn),
        in_specs=[pl.BlockSpec((tm, k), lambda i, j: (i, 0)),
                  pl.BlockSpec((k, tn), lambda i, j: (0, j))],
        out_specs=pl.BlockSpec((tm, tn), lambda i, j: (i, j)),
        out_shape=jax.ShapeDtypeStruct((m, n), out_dtype),
        compiler_params=_cparams(("parallel", "parallel"), VMEM_LIMIT),
        name="in_proj",
    )(x, w)


def _rwkv_kernel(r_ref, k_ref, v_ref, l_ref, mur_ref, muk_ref, muv_ref, mul_ref,
                 w0_ref, a0_ref, kk_ref, ka_ref, rk_ref, gng_ref, gnb_ref,
                 wup_ref, aup_ref, gup_ref, o_ref,
                 pr_s, pk_s, pv_s, pl_s, state_s, r_s, w_s, k_s, v_s, a_s, b_s, g_s):
    s = pl.program_id(2)
    T = r_ref.shape[0]
    G = RWKV_GROUP
    C = RWKV_CHUNK

    @pl.when(s == 0)
    def _():
        state_s[...] = jnp.zeros_like(state_s)
        pr_s[...] = jnp.zeros_like(pr_s)
        pk_s[...] = jnp.zeros_like(pk_s)
        pv_s[...] = jnp.zeros_like(pv_s)
        pl_s[...] = jnp.zeros_like(pl_s)

    row = lax.broadcasted_iota(jnp.int32, (T, 1), 0)

    def shift(ref, prev_s, mu_ref):
        p = ref[...]
        prev = jnp.where(row == 0, prev_s[...], pltpu.roll(p, 1, 0))
        prev_s[...] = p[T - 1:T, :]
        return p + (prev - p) * mu_ref[...]

    r = shift(r_ref, pr_s, mur_ref)
    k = shift(k_ref, pk_s, muk_ref)
    v = shift(v_ref, pv_s, muv_ref)
    lo = shift(l_ref, pl_s, mul_ref)
    dw = lo[:, 0:LANES]
    da = lo[:, LANES:2 * LANES]
    dg = lo[:, 2 * LANES:4 * LANES]

    wpre = w0_ref[...] + _dot(jnp.tanh(dw).astype(BF16), wup_ref[...])
    sp = jnp.maximum(-wpre, 0.0) + jnp.log(1.0 + jnp.exp(-jnp.abs(wpre)))
    wlog = -jnp.exp(-sp - 0.5)
    a_sig = _sigmoid(a0_ref[...] + _dot(da.astype(BF16), aup_ref[...]))
    gate = _dot(_sigmoid(dg).astype(BF16), gup_ref[...])

    gi = lax.broadcasted_iota(jnp.int32, (G, G), 0)
    gj = lax.broadcasted_iota(jnp.int32, (G, G), 1)
    same_head = (gi // RWKV_HEAD) == (gj // RWKV_HEAD)
    head_ones = jnp.where(same_head, 1.0, 0.0).astype(BF16)

    kk = k * kk_ref[...]
    nrm = jnp.sqrt(_dot_exact_rhs(kk * kk, head_ones))
    kk = kk / jnp.maximum(nrm, 1e-12)
    k2 = k * (1.0 + (a_sig - 1.0) * ka_ref[...])

    r_s[...] = r
    w_s[...] = wlog
    k_s[...] = k2
    v_s[...] = v
    a_s[...] = -kk
    b_s[...] = kk * a_sig
    g_s[...] = gate

    ti = gi % C
    tj = gj % C
    strict = tj < ti
    incl = tj <= ti
    eye = (gi == gj)
    ci = lax.broadcasted_iota(jnp.int32, (C, C), 0)
    cj = lax.broadcasted_iota(jnp.int32, (C, C), 1)
    tri = jnp.where(cj <= ci, 1.0, 0.0).astype(BF16)
    ones_cg = jnp.ones((C, G), BF16)

    def stack(x):
        return jnp.where(same_head, jnp.concatenate([x, x, x, x], axis=0), 0.0)

    def chunk(c, carry):
        sl = pl.ds(pl.multiple_of(c * C, C), C)
        rc, wc, kc, vc, ac, bc = r_s[sl, :], w_s[sl, :], k_s[sl, :], v_s[sl, :], a_s[sl, :], b_s[sl, :]
        cum = _dot_exact_lhs(tri, wc)
        tot = cum[C - 1:C, :]
        gam = jnp.exp(cum)
        ginv = jnp.exp(-cum)
        gend = jnp.exp(tot - cum)
        r2f = stack(rc * gam)
        r2 = r2f.astype(BF16)
        a2 = stack(ac * jnp.exp(cum - wc)).astype(BF16)
        k2m = stack(kc * ginv).astype(BF16)
        b2m = stack(bc * ginv).astype(BF16)
        ke2 = stack(kc * gend).astype(BF16)
        be2 = stack(bc * gend).astype(BF16)
        v2 = stack(vc).astype(BF16)

        l_ab = jnp.where(strict, _dot_nt(a2, b2m), 0.0)
        l_ak = jnp.where(strict, _dot_nt(a2, k2m), 0.0)
        l_rb = jnp.where(incl, _dot_nt(r2, b2m), 0.0)
        l_rk = jnp.where(incl, _dot_nt(r2, k2m), 0.0)

        p = l_ab
        tinv = jnp.where(eye, 1.0, 0.0) + l_ab
        for _ in range(int(math.log2(C)) - 1):
            pb = p.astype(BF16)
            p = _dot(pb, pb)
            tinv = tinv + _dot(tinv.astype(BF16), p.astype(BF16))
        tb = tinv.astype(BF16)

        akv = _dot(l_ak.astype(BF16), v2)
        w2 = _dot(tb, a2)
        uv2 = _dot(tb, akv.astype(BF16))
        lrb = l_rb.astype(BF16)
        w2b = w2.astype(BF16)
        rq2 = r2f + _dot(lrb, w2b)
        yv2 = _dot(lrb, uv2.astype(BF16)) + _dot(l_rk.astype(BF16), v2)

        st = state_s[...]
        stb = st.astype(BF16)
        y2 = _dot(rq2.astype(BF16), stb) + yv2
        u2 = _dot(w2b, stb) + uv2
        wh, wm, wl = _split3(wc)
        gcol = jnp.exp(_dot_tn(wh, ones_cg) + _dot_tn(wm, ones_cg) + _dot_tn(wl, ones_cg))
        state_s[...] = gcol * st + _dot_tn(be2, u2.astype(BF16)) + _dot_tn(ke2, v2)

        y = y2[0:C] + y2[C:2 * C] + y2[2 * C:3 * C] + y2[3 * C:4 * C]
        mean = _dot_exact_rhs(y, head_ones) * (1.0 / RWKV_HEAD)
        d = y - mean
        var = _dot_exact_rhs(d * d, head_ones) * (1.0 / RWKV_HEAD)
        yn = d * lax.rsqrt(var + GN_EPS) * gng_ref[...] + gnb_ref[...]
        bonus = _dot_exact_rhs(rc * kc * rk_ref[...], head_ones) * vc
        o_ref[sl, :] = ((yn + bonus) * g_s[sl, :]).astype(o_ref.dtype)
        return carry

    lax.fori_loop(0, T // C, chunk, 0)


def _rwkv(p_r, mu, w0, a0, k_k, k_a, r_k, gn_g, gn_b, w_up, a_up, g_up, batch, seq, tseq):
    n = batch * seq
    G = RWKV_GROUP
    nq = RWKV_WIDTH // G
    ns = seq // tseq
    lora_blk = 3 * RWKV_WIDTH // LORA_COLS

    def tok(off):
        return pl.BlockSpec((tseq, G), lambda b, q, s: (b * ns + s, off + q))

    def par(off):
        return pl.BlockSpec((1, G), lambda b, q, s: (0, off + q))

    in_specs = [
        tok(0), tok(nq), tok(2 * nq),
        pl.BlockSpec((tseq, LORA_COLS), lambda b, q, s: (b * ns + s, lora_blk)),
        par(0), par(nq), par(2 * nq),
        pl.BlockSpec((1, LORA_COLS), lambda b, q, s: (0, lora_blk)),
        par(0), par(0), par(0), par(0), par(0), par(0), par(0),
        pl.BlockSpec((LANES, G), lambda b, q, s: (0, q)),
        pl.BlockSpec((LANES, G), lambda b, q, s: (0, q)),
        pl.BlockSpec((2 * LANES, G), lambda b, q, s: (0, q)),
    ]
    scratch = [pltpu.VMEM((1, G), F32), pltpu.VMEM((1, G), F32), pltpu.VMEM((1, G), F32),
               pltpu.VMEM((1, LORA_COLS), F32), pltpu.VMEM((G, G), F32)]
    scratch += [pltpu.VMEM((tseq, G), F32) for _ in range(7)]
    return pl.pallas_call(
        _rwkv_kernel,
        grid=(batch, nq, ns),
        in_specs=in_specs,
        out_specs=pl.BlockSpec((tseq, G), lambda b, q, s: (b * ns + s, q)),
        out_shape=jax.ShapeDtypeStruct((n, RWKV_WIDTH), BF16),
        scratch_shapes=scratch,
        compiler_params=_cparams(("parallel", "parallel", "arbitrary"), VMEM_LIMIT),
        name="rwkv7",
    )(p_r, p_r, p_r, p_r, mu, mu, mu, mu, w0, a0, k_k, k_a, r_k, gn_g, gn_b, w_up, a_up, g_up)


def _attn_kernel(q_ref, k_ref, v_ref, lq1_ref, lk1_ref, lq2_ref, lk2_ref, g_ref, o_ref, *, lambda_init):
    i = pl.program_id(2)
    tq = q_ref.shape[0]
    lane = lax.broadcasted_iota(jnp.int32, (1, DIFF_VDIM), 1)
    q = q_ref[...] * (DIFF_HEAD ** -0.5)
    q1 = jnp.where(lane < DIFF_HEAD, q, 0.0).astype(BF16)
    q2 = jnp.where(lane >= DIFF_HEAD, q, 0.0).astype(BF16)
    lam = (jnp.exp(jnp.sum(lq1_ref[...] * lk1_ref[...], axis=-1, keepdims=True))
           - jnp.exp(jnp.sum(lq2_ref[...] * lk2_ref[...], axis=-1, keepdims=True)) + lambda_init)

    def update(qm, kj, vj, m, l, acc, mask):
        sc = _dot_nt(qm, kj)
        if mask is not None:
            sc = jnp.where(mask, sc, NEG_BIG)
        m_new = jnp.maximum(m, jnp.max(sc, axis=-1, keepdims=True))
        alpha = jnp.exp(m - m_new)
        p = jnp.exp(sc - m_new)
        l = alpha * l + jnp.sum(p, axis=-1, keepdims=True)
        acc = alpha * acc + _dot(p.astype(BF16), vj)
        return m_new, l, acc

    def step(j, carry, mask):
        m1, l1, acc1, m2, l2, acc2 = carry
        sl = pl.ds(pl.multiple_of(j * tq, tq), tq)
        kj = k_ref[sl, :]
        vj = v_ref[sl, :]
        m1, l1, acc1 = update(q1, kj, vj, m1, l1, acc1, mask)
        m2, l2, acc2 = update(q2, kj, vj, m2, l2, acc2, mask)
        return m1, l1, acc1, m2, l2, acc2

    zero1 = jnp.zeros((tq, 1), F32)
    neg1 = jnp.full((tq, 1), NEG_BIG, F32)
    zacc = jnp.zeros((tq, DIFF_VDIM), F32)
    carry = (neg1, zero1, zacc, neg1, zero1, zacc)
    carry = lax.fori_loop(0, i, lambda j, c: step(j, c, None), carry)
    qi = lax.broadcasted_iota(jnp.int32, (tq, tq), 0)
    kj_ = lax.broadcasted_iota(jnp.int32, (tq, tq), 1)
    m1, l1, acc1, m2, l2, acc2 = step(i, carry, kj_ <= qi)
    o = acc1 / l1 - lam * (acc2 / l2)
    o = o * lax.rsqrt(jnp.mean(o * o, axis=-1, keepdims=True) + RMS_EPS) * g_ref[...]
    o_ref[...] = (o * (1.0 - lambda_init)).astype(o_ref.dtype)


def _diff_attention(p_d, lq1, lk1, lq2, lk2, subln_g, lambda_init, batch, seq, tq):
    n = batch * seq
    nq = seq // tq
    H = DIFF_HEADS
    small = pl.BlockSpec((1, DIFF_HEAD), lambda b, h, i: (0, 0))
    return pl.pallas_call(
        functools.partial(_attn_kernel, lambda_init=lambda_init),
        grid=(batch, H, nq),
        in_specs=[pl.BlockSpec((tq, DIFF_VDIM), lambda b, h, i: (b * nq + i, h)),
                  pl.BlockSpec((seq, DIFF_VDIM), lambda b, h, i: (b, H + h)),
                  pl.BlockSpec((seq, DIFF_VDIM), lambda b, h, i: (b, 2 * H + h)),
                  small, small, small, small,
                  pl.BlockSpec((1, DIFF_VDIM), lambda b, h, i: (0, 0))],
        out_specs=pl.BlockSpec((tq, DIFF_VDIM), lambda b, h, i: (b * nq + i, h)),
        out_shape=jax.ShapeDtypeStruct((n, DIFF_WIDTH), BF16),
        compiler_params=_cparams(("parallel", "parallel", "arbitrary"), VMEM_LIMIT),
        name="diff_attn",
    )(p_d, p_d, p_d, lq1, lk1, lq2, lk2, subln_g)


def _layer_norm(y, g, b):
    mu = jnp.mean(y, axis=-1, keepdims=True)
    d = y - mu
    var = jnp.mean(d * d, axis=-1, keepdims=True)
    return d * lax.rsqrt(var + LN_EPS) * g + b


def _store_row_tiles(ref, words):
    rows = words.shape[0]
    for s in range(ROW_TILE):
        ref[pl.ds(s, rows, stride=ROW_TILE), :] = words[:, s * LANES:(s + 1) * LANES]


def _load_row_tiles(ref, start, rows):
    parts = [ref[pl.ds(start * ROW_TILE + s, rows, stride=ROW_TILE), :] for s in range(ROW_TILE)]
    return jnp.concatenate(parts, axis=1)


def _pack_row(y):
    bits = lax.bitcast_convert_type(y.astype(BF16).astype(F32), jnp.int32)
    lo = lax.shift_right_logical(bits[:, :ROW_WORDS], 16)
    return lo | (bits[:, ROW_WORDS:] & HIGH_HALF)


def _unpack_row(words):
    lo = lax.bitcast_convert_type(lax.shift_left(words, 16), F32)
    hi = lax.bitcast_convert_type(words & HIGH_HALF, F32)
    return lo, hi


def _outproj_kernel(hr_ref, hd_ref, x_ref, wt_ref, wb_ref, g_ref, b_ref, wr_ref, br_ref,
                    x1_ref, x1p_ref, idx_ref, rank_ref, gate_ref, cnt_ref, carry_s):
    i = pl.program_id(0)
    tm = x_ref.shape[0]

    @pl.when(i == 0)
    def _():
        carry_s[...] = jnp.zeros_like(carry_s)

    mix = _dot(hr_ref[...], wt_ref[...]) + _dot(hd_ref[...], wb_ref[...])
    x1 = _layer_norm(DEEPNORM_ALPHA * x_ref[...] + mix, g_ref[...], b_ref[...])
    x1_ref[...] = x1
    _store_row_tiles(x1p_ref, _pack_row(x1))

    logits = jnp.dot(x1, wr_ref[...], preferred_element_type=F32,
                     precision=lax.Precision.HIGHEST) + br_ref[...]
    lane = lax.broadcasted_iota(jnp.int32, (tm, LANES), 1).astype(F32)
    work = logits
    sels, vals, idxs = [], [], []
    for _ in range(TOP_K):
        mx = jnp.max(work, axis=-1, keepdims=True)
        idx = jnp.min(jnp.where(work == mx, lane, float(LANES)), axis=-1, keepdims=True)
        sel = lane == idx
        work = jnp.where(sel, -jnp.inf, work)
        sels.append(sel)
        vals.append(mx)
        idxs.append(idx)
    exps = [jnp.exp(vv - vals[0]) for vv in vals]
    den = exps[0] + exps[1] + exps[2] + exps[3]
    onehot = jnp.zeros((tm, LANES), F32)
    for sel in sels:
        onehot = onehot + jnp.where(sel, 1.0, 0.0)
    ti = lax.broadcasted_iota(jnp.int32, (tm, tm), 0)
    tj = lax.broadcasted_iota(jnp.int32, (tm, tm), 1)
    before = jnp.where(tj < ti, 1.0, 0.0).astype(BF16)
    cum = _dot(before, onehot.astype(BF16)) + carry_s[0:1, :]
    idx_out = jnp.zeros((tm, LANES), F32)
    rank_out = jnp.zeros((tm, LANES), F32)
    gate_out = jnp.zeros((tm, LANES), F32)
    for kk in range(TOP_K):
        rk = jnp.sum(jnp.where(sels[kk], cum, 0.0), axis=-1, keepdims=True)
        slot = lane == float(kk)
        idx_out = jnp.where(slot, idxs[kk], idx_out)
        rank_out = jnp.where(slot, rk, rank_out)
        gate_out = jnp.where(slot, exps[kk] / den, gate_out)
    idx_ref[...] = idx_out.astype(jnp.int32)
    rank_ref[...] = rank_out.astype(jnp.int32)
    gate_ref[...] = gate_out
    total = carry_s[0:1, :] + jnp.sum(onehot, axis=0, keepdims=True)
    carry_s[...] = jnp.broadcast_to(total, carry_s.shape)
    cnt_ref[...] = jnp.broadcast_to(total, cnt_ref.shape)


def _outproj_ln_router(hr, hd, x, w_top, w_bot, g, b, w_r, b_r, tm):
    n = x.shape[0]
    const = lambda shape: pl.BlockSpec(shape, lambda i: (0, 0))
    rowb = lambda cols: pl.BlockSpec((tm, cols), lambda i: (i, 0))
    out_shape = (jax.ShapeDtypeStruct((n, D_MODEL), F32),
                 jax.ShapeDtypeStruct((n * ROW_TILE, LANES), jnp.int32),
                 jax.ShapeDtypeStruct((n, LANES), jnp.int32),
                 jax.ShapeDtypeStruct((n, LANES), jnp.int32),
                 jax.ShapeDtypeStruct((n, LANES), F32),
                 jax.ShapeDtypeStruct((SUBLANES, LANES), F32))
    return pl.pallas_call(
        _outproj_kernel,
        grid=(n // tm,),
        in_specs=[rowb(RWKV_WIDTH), rowb(DIFF_WIDTH), rowb(D_MODEL),
                  const((RWKV_WIDTH, D_MODEL)), const((DIFF_WIDTH, D_MODEL)),
                  const((1, D_MODEL)), const((1, D_MODEL)),
                  const((D_MODEL, LANES)), const((1, LANES))],
        out_specs=(rowb(D_MODEL), pl.BlockSpec((tm * ROW_TILE, LANES), lambda i: (i, 0)),
                   rowb(LANES), rowb(LANES), rowb(LANES), const((SUBLANES, LANES))),
        out_shape=out_shape,
        scratch_shapes=[pltpu.VMEM((SUBLANES, LANES), F32)],
        compiler_params=_cparams(("arbitrary",), VMEM_LIMIT),
        name="outproj_ln_router",
    )(hr, hd, x, w_top, w_bot, g, b, w_r, b_r)


def _dispatch_kernel(idx_ref, rank_ref, ps_ref, x_hbm, xs_in, xs_hbm, sem):
    del xs_in
    tt = idx_ref.shape[0] // TOP_K
    base = pl.program_id(0) * tt

    def copy(src_row, dst_row):
        return pltpu.make_async_copy(x_hbm.at[src_row], xs_hbm.at[dst_row], sem)

    def issue(i, c):
        dst = ps_ref[idx_ref[i]] + rank_ref[i]
        copy(base + i // TOP_K, dst).start()
        return c

    lax.fori_loop(0, tt * TOP_K, issue, 0)

    def drain(i, c):
        copy(0, 0).wait()
        return c

    lax.fori_loop(0, tt * TOP_K, drain, 0)


def _dispatch(idx_flat, rank_flat, pstarts, x1p3, n_rows, tt):
    n = x1p3.shape[0]
    zeros = jnp.zeros((n_rows, ROW_TILE, LANES), jnp.int32)
    smem = lambda size: pl.BlockSpec((size,), lambda i: (i,), memory_space=pltpu.SMEM)
    return pl.pallas_call(
        _dispatch_kernel,
        grid=(n // tt,),
        in_specs=[smem(tt * TOP_K), smem(tt * TOP_K),
                  pl.BlockSpec((LANES,), lambda i: (0,), memory_space=pltpu.SMEM),
                  pl.BlockSpec(memory_space=pl.ANY), pl.BlockSpec(memory_space=pl.ANY)],
        out_specs=pl.BlockSpec(memory_space=pl.ANY),
        out_shape=jax.ShapeDtypeStruct((n_rows, ROW_TILE, LANES), jnp.int32),
        scratch_shapes=[pltpu.SemaphoreType.DMA],
        input_output_aliases={4: 0},
        compiler_params=_cparams(("arbitrary",)),
        name="moe_dispatch",
    )(idx_flat, rank_flat, pstarts, x1p3, zeros)


def _new_expert(be_ref, j):
    return jnp.logical_or(j == 0, be_ref[j] != be_ref[jnp.maximum(j - 1, 0)])


def _gu_kernel(be_ref, nv_ref, xs_ref, wg_ref, wu_ref, bg_ref, bu_ref, o_ref, wg_s, wu_s):
    j = pl.program_id(1)
    bm = o_ref.shape[0]

    @pl.when(jnp.logical_and(j < nv_ref[0], _new_expert(be_ref, j)))
    def _():
        wg_s[...] = wg_ref[0].astype(BF16)
        wu_s[...] = wu_ref[0].astype(BF16)

    @pl.when(j < nv_ref[0])
    def _():
        lo, hi = _unpack_row(_load_row_tiles(xs_ref, 0, bm))
        x = jnp.concatenate([lo, hi], axis=1).astype(BF16)
        g = _dot(x, wg_s[...]) + bg_ref[0]
        u = _dot(x, wu_s[...]) + bu_ref[0]
        g = jnp.minimum(g, SWIGLU_LIMIT)
        u = jnp.clip(u, -SWIGLU_LIMIT, SWIGLU_LIMIT)
        o_ref[...] = ((u + 1.0) * (g * _sigmoid(SWIGLU_ALPHA * g))).astype(o_ref.dtype)

    @pl.when(j >= nv_ref[0])
    def _():
        o_ref[...] = jnp.zeros_like(o_ref)


def _moe_gate_up(block_e, n_valid, xs2, w_gu, b_gu, bm, tn):
    n_rows = xs2.shape[0] // ROW_TILE
    n_blocks = n_rows // bm
    nt = D_FF // tn
    grid_spec = pltpu.PrefetchScalarGridSpec(
        num_scalar_prefetch=2,
        grid=(nt, n_blocks),
        in_specs=[pl.BlockSpec((bm * ROW_TILE, LANES), lambda n, j, be, nv: (j, 0)),
                  pl.BlockSpec((1, D_MODEL, tn), lambda n, j, be, nv: (be[j], 0, n)),
                  pl.BlockSpec((1, D_MODEL, tn), lambda n, j, be, nv: (be[j], 0, nt + n)),
                  pl.BlockSpec((1, 1, tn), lambda n, j, be, nv: (be[j], 0, n)),
                  pl.BlockSpec((1, 1, tn), lambda n, j, be, nv: (be[j], 0, nt + n))],
        out_specs=pl.BlockSpec((bm, tn), lambda n, j, be, nv: (j, n)),
        scratch_shapes=[pltpu.VMEM((D_MODEL, tn), BF16), pltpu.VMEM((D_MODEL, tn), BF16)])
    return pl.pallas_call(
        _gu_kernel,
        grid_spec=grid_spec,
        out_shape=jax.ShapeDtypeStruct((n_rows, D_FF), BF16),
        compiler_params=_cparams(("arbitrary", "arbitrary"), VMEM_LIMIT),
        name="moe_gate_up",
    )(block_e, n_valid, xs2, w_gu, w_gu, b_gu, b_gu)


def _dn_kernel(be_ref, nv_ref, a_ref, w_ref, b_ref, o_ref, w_s):
    j = pl.program_id(0)

    @pl.when(jnp.logical_and(j < nv_ref[0], _new_expert(be_ref, j)))
    def _():
        w_s[...] = w_ref[0].astype(BF16)

    @pl.when(j < nv_ref[0])
    def _():
        y = _dot(a_ref[...], w_s[...]) + b_ref[0]
        _store_row_tiles(o_ref, _pack_row(y))

    @pl.when(j >= nv_ref[0])
    def _():
        o_ref[...] = jnp.zeros_like(o_ref)


def _moe_down(block_e, n_valid, act, w_dn, b_dn, bm):
    n_rows = act.shape[0]
    n_blocks = n_rows // bm
    grid_spec = pltpu.PrefetchScalarGridSpec(
        num_scalar_prefetch=2,
        grid=(n_blocks,),
        in_specs=[pl.BlockSpec((bm, D_FF), lambda j, be, nv: (j, 0)),
                  pl.BlockSpec((1, D_FF, D_MODEL), lambda j, be, nv: (be[j], 0, 0)),
                  pl.BlockSpec((1, 1, D_MODEL), lambda j, be, nv: (be[j], 0, 0))],
        out_specs=pl.BlockSpec((bm * ROW_TILE, LANES), lambda j, be, nv: (j, 0)),
        scratch_shapes=[pltpu.VMEM((D_FF, D_MODEL), BF16)])
    return pl.pallas_call(
        _dn_kernel,
        grid_spec=grid_spec,
        out_shape=jax.ShapeDtypeStruct((n_rows * ROW_TILE, LANES), jnp.int32),
        compiler_params=_cparams(("arbitrary",), VMEM_LIMIT),
        name="moe_down",
    )(block_e, n_valid, act, w_dn, b_dn)


def _combine_kernel(idx_ref, rank_ref, ps_ref, gate_ref, x1_ref, g_ref, b_ref, ys_hbm, o_ref, buf, sem):
    tt = x1_ref.shape[0]

    def copy(src_row, slot):
        dst = buf.at[pl.ds(pl.multiple_of(slot * ROW_TILE, ROW_TILE), ROW_TILE), :]
        return pltpu.make_async_copy(ys_hbm.at[src_row], dst, sem)

    def issue(i, c):
        src = ps_ref[idx_ref[i]] + rank_ref[i]
        copy(src, (i % TOP_K) * tt + i // TOP_K).start()
        return c

    lax.fori_loop(0, tt * TOP_K, issue, 0)

    def drain(i, c):
        copy(0, 0).wait()
        return c

    lax.fori_loop(0, tt * TOP_K, drain, 0)

    gates = gate_ref[...]
    acc_lo = jnp.zeros((tt, ROW_WORDS), F32)
    acc_hi = jnp.zeros((tt, ROW_WORDS), F32)
    for kk in range(TOP_K):
        lo, hi = _unpack_row(_load_row_tiles(buf, kk * tt, tt))
        gk = gates[:, kk:kk + 1]
        acc_lo = acc_lo + gk * lo
        acc_hi = acc_hi + gk * hi
    ffn = jnp.concatenate([acc_lo, acc_hi], axis=1)
    o_ref[...] = _layer_norm(DEEPNORM_ALPHA * x1_ref[...] + ffn, g_ref[...], b_ref[...])


def _combine_ln(idx_flat, rank_flat, pstarts, gates, x1, g, b, ys3, tt):
    n = x1.shape[0]
    smem = lambda size: pl.BlockSpec((size,), lambda i: (i,), memory_space=pltpu.SMEM)
    const = lambda shape: pl.BlockSpec(shape, lambda i: (0, 0))
    return pl.pallas_call(
        _combine_kernel,
        grid=(n // tt,),
        in_specs=[smem(tt * TOP_K), smem(tt * TOP_K),
                  pl.BlockSpec((LANES,), lambda i: (0,), memory_space=pltpu.SMEM),
                  pl.BlockSpec((tt, LANES), lambda i: (i, 0)),
                  pl.BlockSpec((tt, D_MODEL), lambda i: (i, 0)),
                  const((1, D_MODEL)), const((1, D_MODEL)),
                  pl.BlockSpec(memory_space=pl.ANY)],
        out_specs=pl.BlockSpec((tt, D_MODEL), lambda i: (i, 0)),
        out_shape=jax.ShapeDtypeStruct((n, D_MODEL), F32),
        scratch_shapes=[pltpu.VMEM((TOP_K * tt * ROW_TILE, LANES), jnp.int32), pltpu.SemaphoreType.DMA],
        compiler_params=_cparams(("arbitrary",), VMEM_LIMIT),
        name="moe_combine_ln",
    )(idx_flat, rank_flat, pstarts, gates, x1, g, b, ys3)


def _pad_cols(a, width):
    return jnp.pad(a, ((0, 0), (0, width - a.shape[1])))


def _lora_layout(a):
    dw = a[:, :DECAY_LORA]
    da = a[:, DECAY_LORA:DECAY_LORA + AAA_LORA]
    dg = a[:, DECAY_LORA + AAA_LORA:]
    return jnp.concatenate([_pad_cols(dw, LANES), _pad_cols(da, LANES), _pad_cols(dg, 2 * LANES)], axis=1)


def _moe_ffn(x1, x1p2, idx, rank, gates, counts, w_gu, b_gu, w_dn, b_dn, ln_g, ln_b, bm):
    n = x1.shape[0]
    nk = n * TOP_K
    n_blocks = nk // bm + N_EXPERTS
    n_rows = n_blocks * bm
    cnt = counts[0, :N_EXPERTS].astype(jnp.int32)
    padded = ((cnt + bm - 1) // bm) * bm
    pends = jnp.cumsum(padded)
    pstarts = jnp.pad(pends - padded, (0, LANES - N_EXPERTS)).astype(jnp.int32)
    block_e = jnp.minimum(jnp.searchsorted(pends, jnp.arange(n_blocks, dtype=jnp.int32) * bm, side="right"),
                          N_EXPERTS - 1).astype(jnp.int32)
    n_valid = (pends[-1:] // bm).astype(jnp.int32)
    idx_flat = idx[:, :TOP_K].reshape(nk)
    rank_flat = rank[:, :TOP_K].reshape(nk)

    xs3 = _dispatch(idx_flat, rank_flat, pstarts, x1p2.reshape(n, ROW_TILE, LANES), n_rows, 256)
    act = _moe_gate_up(block_e, n_valid, xs3.reshape(n_rows * ROW_TILE, LANES), w_gu,
                       b_gu.reshape(N_EXPERTS, 1, 2 * D_FF), bm, 512)
    ys2 = _moe_down(block_e, n_valid, act, w_dn, b_dn.reshape(N_EXPERTS, 1, D_MODEL), bm)
    return _combine_ln(idx_flat, rank_flat, pstarts, gates, x1, ln_g, ln_b,
                       ys2.reshape(n_rows, ROW_TILE, LANES), 256)


def _layer(x, w_in, shift_mu, w0, w_up, a0, a_up, g_up, k_k, k_a, r_k, gn_g, gn_b,
           lq1, lk1, lq2, lk2, subln_g, w_out, ln1_g, ln1_b,
           w_router, b_router, w_gu, b_gu, w_dn, b_dn, ln2_g, ln2_b, lambda_init,
           tm_in=512, tseq=512, tq=256, tm_out=256, bm=256):
    batch, seq, d = x.shape
    n = batch * seq
    rw = 3 * RWKV_WIDTH
    rcols = rw + DECAY_LORA + AAA_LORA + GATE_LORA
    row = lambda a: a.reshape(1, -1)

    xf = x.reshape(n, d)
    xb = xf.astype(BF16)
    w_r = jnp.concatenate([w_in[:, :rw], _lora_layout(w_in[:, rw:rcols])], axis=1).astype(BF16)
    w_d = w_in[:, rcols:].astype(BF16)
    mu = jnp.concatenate([row(shift_mu)[:, :rw], _lora_layout(row(shift_mu)[:, rw:])], axis=1)
    p_r = _matmul(xb, w_r, F32, tm_in, 512)
    p_d = _matmul(xb, w_d, BF16, tm_in, 512)

    pad_rows = lambda a, rows: jnp.pad(a, ((0, rows - a.shape[0]), (0, 0))).astype(BF16)
    h_r = _rwkv(p_r, mu, row(w0), row(a0), row(k_k), row(k_a), row(r_k), row(gn_g), row(gn_b),
                pad_rows(w_up, LANES), pad_rows(a_up, LANES), pad_rows(g_up, 2 * LANES), batch, seq, tseq)
    h_d = _diff_attention(p_d, row(lq1), row(lk1), row(lq2), row(lk2), row(subln_g), lambda_init,
                          batch, seq, tq)

    w_ob = w_out.astype(BF16)
    w_rp = _pad_cols(w_router, LANES)
    b_rp = jnp.concatenate([row(b_router), jnp.full((1, LANES - N_EXPERTS), NEG_BIG, F32)], axis=1)
    x1, x1p2, idx, rank, gates, counts = _outproj_ln_router(
        h_r, h_d, xf, w_ob[:RWKV_WIDTH], w_ob[RWKV_WIDTH:], row(ln1_g), row(ln1_b), w_rp, b_rp, tm_out)
    out = _moe_ffn(x1, x1p2, idx, rank, gates, counts, w_gu, b_gu, w_dn, b_dn, row(ln2_g), row(ln2_b), bm)
    return out.reshape(batch, seq, d)


def kernel(x, w_in, shift_mu, w0, w_up, a0, a_up, g_up, k_k, k_a, r_k, gn_g, gn_b, lq1, lk1, lq2, lk2,
           subln_g, w_out, ln1_g, ln1_b, w_router, b_router, w_gu, b_gu, w_dn, b_dn, ln2_g, ln2_b):
    for l in range(DEPTH):
        lambda_init = 0.8 - 0.6 * math.exp(-0.3 * l)
        x = _layer(x, w_in[l], shift_mu[l], w0[l], w_up[l], a0[l], a_up[l], g_up[l], k_k[l], k_a[l],
                   r_k[l], gn_g[l], gn_b[l], lq1[l], lk1[l], lq2[l], lk2[l], subln_g[l], w_out[l],
                   ln1_g[l], ln1_b[l], w_router[l], b_router[l], w_gu[l], b_gu[l], w_dn[l], b_dn[l],
                   ln2_g[l], ln2_b[l], lambda_init)
    return x
```

```python
import functools
import math

import jax
import jax.numpy as jnp
from jax import lax
from jax.experimental import pallas as pl
from jax.experimental.pallas import tpu as pltpu

F32 = jnp.float32
BF16 = jnp.bfloat16

D_MODEL = 2048
RWKV_HEAD = 64
RWKV_WIDTH = 1024
RWKV_HEADS = 16
DECAY_LORA = 64
AAA_LORA = 64
GATE_LORA = 160
DIFF_HEAD = 64
DIFF_VDIM = 128
DIFF_HEADS = 8
DIFF_WIDTH = 1024
N_EXPERTS = 32
TOP_K = 4
D_FF = 2048
SWIGLU_LIMIT = 7.0
SWIGLU_ALPHA = 1.702
LN_EPS = 1e-5
GN_EPS = RWKV_HEAD * 1e-5
RMS_EPS = 1e-5
NEG_BIG = -1e30
DEPTH = 1
DEEPNORM_ALPHA = (2.0 * DEPTH) ** 0.25

LANES = 128
SUBLANES = 8
ROW_WORDS = D_MODEL // 2
ROW_TILE = ROW_WORDS // LANES
RWKV_GROUP = 256
RWKV_CHUNK = 64
LORA_COLS = 512
VMEM_LIMIT = 56 * 1024 * 1024
HIGH_HALF = -65536
MXU_DEPTH = 256


def _cparams(sem, vmem=None):
    return pltpu.CompilerParams(dimension_semantics=sem, vmem_limit_bytes=vmem)


def _dot(a, b):
    return jnp.dot(a, b, preferred_element_type=F32)


def _dot_nt(a, b):
    return lax.dot_general(a, b, (((1,), (1,)), ((), ())), preferred_element_type=F32)


def _dot_tn(a, b):
    return lax.dot_general(a, b, (((0,), (0,)), ((), ())), preferred_element_type=F32)


def _split3(x):
    h = x.astype(BF16)
    r = x - h.astype(F32)
    m = r.astype(BF16)
    l = (r - m.astype(F32)).astype(BF16)
    return h, m, l


def _dot_exact_rhs(x, ones):
    h, m, l = _split3(x)
    return _dot(h, ones) + _dot(m, ones) + _dot(l, ones)


def _dot_exact_lhs(ones, x):
    h, m, l = _split3(x)
    return _dot(ones, h) + _dot(ones, m) + _dot(ones, l)


def _sigmoid(x):
    return 1.0 / (1.0 + jnp.exp(-x))


def _matmul_kernel(x_ref, w_ref, o_ref):
    o_ref[...] = _dot(x_ref[...], w_ref[...]).astype(o_ref.dtype)


def _matmul(x, w, out_dtype, tm, tn):
    m, k = x.shape
    n = w.shape[1]
    return pl.pallas_call(
        _matmul_kernel,
        grid=(n // tn, m // tm),
        in_specs=[pl.BlockSpec((tm, k), lambda j, i: (i, 0)),
                  pl.BlockSpec((k, tn), lambda j, i: (0, j))],
        out_specs=pl.BlockSpec((tm, tn), lambda j, i: (i, j)),
        out_shape=jax.ShapeDtypeStruct((m, n), out_dtype),
        compiler_params=_cparams(("parallel", "parallel"), VMEM_LIMIT),
        name="in_proj",
    )(x, w)


def _rwkv_kernel(r_ref, k_ref, v_ref, l_ref, mur_ref, muk_ref, muv_ref, mul_ref,
                 w0_ref, a0_ref, kk_ref, ka_ref, rk_ref, gng_ref, gnb_ref,
                 wup_ref, aup_ref, gup_ref, o_ref,
                 pr_s, pk_s, pv_s, pl_s, state_s, r_s, w_s, k_s, v_s, a_s, b_s, g_s):
    s = pl.program_id(2)
    T = r_ref.shape[0]
    G = RWKV_GROUP
    C = RWKV_CHUNK

    @pl.when(s == 0)
    def _():
        state_s[...] = jnp.zeros_like(state_s)
        pr_s[...] = jnp.zeros_like(pr_s)
        pk_s[...] = jnp.zeros_like(pk_s)
        pv_s[...] = jnp.zeros_like(pv_s)
        pl_s[...] = jnp.zeros_like(pl_s)

    row = lax.broadcasted_iota(jnp.int32, (T, 1), 0)

    def shift(ref, prev_s, mu_ref):
        p = ref[...]
        prev = jnp.where(row == 0, prev_s[...], pltpu.roll(p, 1, 0))
        prev_s[...] = p[T - 1:T, :]
        return p + (prev - p) * mu_ref[...]

    r = shift(r_ref, pr_s, mur_ref)
    k = shift(k_ref, pk_s, muk_ref)
    v = shift(v_ref, pv_s, muv_ref)
    lo = shift(l_ref, pl_s, mul_ref)
    dw = lo[:, 0:LANES]
    da = lo[:, LANES:2 * LANES]
    dg = lo[:, 2 * LANES:4 * LANES]

    wpre = w0_ref[...] + _dot(jnp.tanh(dw).astype(BF16), wup_ref[...])
    sp = jnp.maximum(-wpre, 0.0) + jnp.log(1.0 + jnp.exp(-jnp.abs(wpre)))
    wlog = -jnp.exp(-sp - 0.5)
    a_sig = _sigmoid(a0_ref[...] + _dot(da.astype(BF16), aup_ref[...]))
    gate = _dot(_sigmoid(dg).astype(BF16), gup_ref[...])

    gi = lax.broadcasted_iota(jnp.int32, (G, G), 0)
    gj = lax.broadcasted_iota(jnp.int32, (G, G), 1)
    same_head = (gi // RWKV_HEAD) == (gj // RWKV_HEAD)
    head_ones = jnp.where(same_head, 1.0, 0.0).astype(BF16)

    kk = k * kk_ref[...]
    nrm = jnp.sqrt(_dot_exact_rhs(kk * kk, head_ones))
    kk = kk / jnp.maximum(nrm, 1e-12)
    k2 = k * (1.0 + (a_sig - 1.0) * ka_ref[...])

    r_s[...] = r
    w_s[...] = wlog
    k_s[...] = k2
    v_s[...] = v
    a_s[...] = -kk
    b_s[...] = kk * a_sig
    g_s[...] = gate

    ti = gi % C
    tj = gj % C
    strict = tj < ti
    incl = tj <= ti
    eye = (gi == gj)
    ci = lax.broadcasted_iota(jnp.int32, (C, C), 0)
    cj = lax.broadcasted_iota(jnp.int32, (C, C), 1)
    tri = jnp.where(cj <= ci, 1.0, 0.0).astype(BF16)
    ones_cg = jnp.ones((C, G), BF16)

    def stack(x):
        return jnp.where(same_head, jnp.concatenate([x, x, x, x], axis=0), 0.0)

    def chunk(c, carry):
        sl = pl.ds(pl.multiple_of(c * C, C), C)
        rc, wc, kc, vc, ac, bc = r_s[sl, :], w_s[sl, :], k_s[sl, :], v_s[sl, :], a_s[sl, :], b_s[sl, :]
        cum = _dot_exact_lhs(tri, wc)
        tot = cum[C - 1:C, :]
        gam = jnp.exp(cum)
        ginv = jnp.exp(-cum)
        gend = jnp.exp(tot - cum)
        r2f = stack(rc * gam)
        r2 = r2f.astype(BF16)
        a2 = stack(ac * jnp.exp(cum - wc)).astype(BF16)
        k2m = stack(kc * ginv).astype(BF16)
        b2m = stack(bc * ginv).astype(BF16)
        ke2 = stack(kc * gend).astype(BF16)
        be2 = stack(bc * gend).astype(BF16)
        v2 = stack(vc).astype(BF16)

        l_ab = jnp.where(strict, _dot_nt(a2, b2m), 0.0)
        l_ak = jnp.where(strict, _dot_nt(a2, k2m), 0.0)
        l_rb = jnp.where(incl, _dot_nt(r2, b2m), 0.0)
        l_rk = jnp.where(incl, _dot_nt(r2, k2m), 0.0)

        p = l_ab
        tinv = jnp.where(eye, 1.0, 0.0) + l_ab
        for _ in range(int(math.log2(C)) - 1):
            pb = p.astype(BF16)
            p = _dot(pb, pb)
            tinv = tinv + _dot(tinv.astype(BF16), p.astype(BF16))
        tb = tinv.astype(BF16)

        akv = _dot(l_ak.astype(BF16), v2)
        w2 = _dot(tb, a2)
        uv2 = _dot(tb, akv.astype(BF16))
        lrb = l_rb.astype(BF16)
        w2b = w2.astype(BF16)
        rq2 = r2f + _dot(lrb, w2b)
        yv2 = _dot(lrb, uv2.astype(BF16)) + _dot(l_rk.astype(BF16), v2)

        st = state_s[...]
        stb = st.astype(BF16)
        y2 = _dot(rq2.astype(BF16), stb) + yv2
        u2 = _dot(w2b, stb) + uv2
        wh, wm, wl = _split3(wc)
        gcol = jnp.exp(_dot_tn(wh, ones_cg) + _dot_tn(wm, ones_cg) + _dot_tn(wl, ones_cg))
        state_s[...] = gcol * st + _dot_tn(be2, u2.astype(BF16)) + _dot_tn(ke2, v2)

        y = y2[0:C] + y2[C:2 * C] + y2[2 * C:3 * C] + y2[3 * C:4 * C]
        mean = _dot_exact_rhs(y, head_ones) * (1.0 / RWKV_HEAD)
        d = y - mean
        var = _dot_exact_rhs(d * d, head_ones) * (1.0 / RWKV_HEAD)
        yn = d * lax.rsqrt(var + GN_EPS) * gng_ref[...] + gnb_ref[...]
        bonus = _dot_exact_rhs(rc * kc * rk_ref[...], head_ones) * vc
        o_ref[sl, :] = ((yn + bonus) * g_s[sl, :]).astype(o_ref.dtype)
        return carry

    lax.fori_loop(0, T // C, chunk, 0)


def _rwkv(p_r, mu, w0, a0, k_k, k_a, r_k, gn_g, gn_b, w_up, a_up, g_up, batch, seq, tseq):
    n = batch * seq
    G = RWKV_GROUP
    nq = RWKV_WIDTH // G
    ns = seq // tseq
    lora_blk = 3 * RWKV_WIDTH // LORA_COLS

    def tok(off):
        return pl.BlockSpec((tseq, G), lambda b, q, s: (b * ns + s, off + q))

    def par(off):
        return pl.BlockSpec((1, G), lambda b, q, s: (0, off + q))

    in_specs = [
        tok(0), tok(nq), tok(2 * nq),
        pl.BlockSpec((tseq, LORA_COLS), lambda b, q, s: (b * ns + s, lora_blk)),
        par(0), par(nq), par(2 * nq),
        pl.BlockSpec((1, LORA_COLS), lambda b, q, s: (0, lora_blk)),
        par(0), par(0), par(0), par(0), par(0), par(0), par(0),
        pl.BlockSpec((LANES, G), lambda b, q, s: (0, q)),
        pl.BlockSpec((LANES, G), lambda b, q, s: (0, q)),
        pl.BlockSpec((2 * LANES, G), lambda b, q, s: (0, q)),
    ]
    scratch = [pltpu.VMEM((1, G), F32), pltpu.VMEM((1, G), F32), pltpu.VMEM((1, G), F32),
               pltpu.VMEM((1, LORA_COLS), F32), pltpu.VMEM((G, G), F32)]
    scratch += [pltpu.VMEM((tseq, G), F32) for _ in range(7)]
    return pl.pallas_call(
        _rwkv_kernel,
        grid=(batch, nq, ns),
        in_specs=in_specs,
        out_specs=pl.BlockSpec((tseq, G), lambda b, q, s: (b * ns + s, q)),
        out_shape=jax.ShapeDtypeStruct((n, RWKV_WIDTH), BF16),
        scratch_shapes=scratch,
        compiler_params=_cparams(("parallel", "parallel", "arbitrary"), VMEM_LIMIT),
        name="rwkv7",
    )(p_r, p_r, p_r, p_r, mu, mu, mu, mu, w0, a0, k_k, k_a, r_k, gn_g, gn_b, w_up, a_up, g_up)


def _attn_kernel(q_ref, k_ref, v_ref, lq1_ref, lk1_ref, lq2_ref, lk2_ref, g_ref, o_ref, *, lambda_init):
    i = pl.program_id(2)
    tq = q_ref.shape[0]
    lane = lax.broadcasted_iota(jnp.int32, (1, DIFF_VDIM), 1)
    q = q_ref[...] * (DIFF_HEAD ** -0.5)
    q1 = jnp.where(lane < DIFF_HEAD, q, 0.0).astype(BF16)
    q2 = jnp.where(lane >= DIFF_HEAD, q, 0.0).astype(BF16)
    lam = (jnp.exp(jnp.sum(lq1_ref[...] * lk1_ref[...], axis=-1, keepdims=True))
           - jnp.exp(jnp.sum(lq2_ref[...] * lk2_ref[...], axis=-1, keepdims=True)) + lambda_init)

    def update(qm, kj, vj, m, l, acc, mask):
        sc = _dot_nt(qm, kj)
        if mask is not None:
            sc = jnp.where(mask, sc, NEG_BIG)
        m_new = jnp.maximum(m, jnp.max(sc, axis=-1, keepdims=True))
        alpha = jnp.exp(m - m_new)
        p = jnp.exp(sc - m_new)
        l = alpha * l + jnp.sum(p, axis=-1, keepdims=True)
        acc = alpha * acc + _dot(p.astype(BF16), vj)
        return m_new, l, acc

    def step(j, carry, mask):
        m1, l1, acc1, m2, l2, acc2 = carry
        sl = pl.ds(pl.multiple_of(j * tq, tq), tq)
        kj = k_ref[sl, :]
        vj = v_ref[sl, :]
        m1, l1, acc1 = update(q1, kj, vj, m1, l1, acc1, mask)
        m2, l2, acc2 = update(q2, kj, vj, m2, l2, acc2, mask)
        return m1, l1, acc1, m2, l2, acc2

    zero1 = jnp.zeros((tq, 1), F32)
    neg1 = jnp.full((tq, 1), NEG_BIG, F32)
    zacc = jnp.zeros((tq, DIFF_VDIM), F32)
    carry = (neg1, zero1, zacc, neg1, zero1, zacc)
    carry = lax.fori_loop(0, i, lambda j, c: step(j, c, None), carry)
    qi = lax.broadcasted_iota(jnp.int32, (tq, tq), 0)
    kj_ = lax.broadcasted_iota(jnp.int32, (tq, tq), 1)
    m1, l1, acc1, m2, l2, acc2 = step(i, carry, kj_ <= qi)
    o = acc1 / l1 - lam * (acc2 / l2)
    o = o * lax.rsqrt(jnp.mean(o * o, axis=-1, keepdims=True) + RMS_EPS) * g_ref[...]
    o_ref[...] = (o * (1.0 - lambda_init)).astype(o_ref.dtype)


def _diff_attention(p_d, lq1, lk1, lq2, lk2, subln_g, lambda_init, batch, seq, tq):
    n = batch * seq
    nq = seq // tq
    H = DIFF_HEADS
    small = pl.BlockSpec((1, DIFF_HEAD), lambda b, h, i: (0, 0))
    return pl.pallas_call(
        functools.partial(_attn_kernel, lambda_init=lambda_init),
        grid=(batch, H, nq),
        in_specs=[pl.BlockSpec((tq, DIFF_VDIM), lambda b, h, i: (b * nq + i, h)),
                  pl.BlockSpec((seq, DIFF_VDIM), lambda b, h, i: (b, H + h)),
                  pl.BlockSpec((seq, DIFF_VDIM), lambda b, h, i: (b, 2 * H + h)),
                  small, small, small, small,
                  pl.BlockSpec((1, DIFF_VDIM), lambda b, h, i: (0, 0))],
        out_specs=pl.BlockSpec((tq, DIFF_VDIM), lambda b, h, i: (b * nq + i, h)),
        out_shape=jax.ShapeDtypeStruct((n, DIFF_WIDTH), BF16),
        compiler_params=_cparams(("parallel", "parallel", "arbitrary"), VMEM_LIMIT),
        name="diff_attn",
    )(p_d, p_d, p_d, lq1, lk1, lq2, lk2, subln_g)


def _layer_norm(y, g, b):
    mu = jnp.mean(y, axis=-1, keepdims=True)
    d = y - mu
    var = jnp.mean(d * d, axis=-1, keepdims=True)
    return d * lax.rsqrt(var + LN_EPS) * g + b


def _store_row_tiles(ref, words):
    rows = words.shape[0]
    for s in range(ROW_TILE):
        ref[pl.ds(s, rows, stride=ROW_TILE), :] = words[:, s * LANES:(s + 1) * LANES]


def _load_row_tiles(ref, start, rows):
    parts = [ref[pl.ds(start * ROW_TILE + s, rows, stride=ROW_TILE), :] for s in range(ROW_TILE)]
    return jnp.concatenate(parts, axis=1)


def _pack_row(y):
    bits = lax.bitcast_convert_type(y.astype(BF16).astype(F32), jnp.int32)
    lo = lax.shift_right_logical(bits[:, :ROW_WORDS], 16)
    return lo | (bits[:, ROW_WORDS:] & HIGH_HALF)


def _unpack_row(words):
    lo = lax.bitcast_convert_type(lax.shift_left(words, 16), F32)
    hi = lax.bitcast_convert_type(words & HIGH_HALF, F32)
    return lo, hi


def _expert_onehots(idx):
    lane = lax.broadcasted_iota(jnp.int32, idx.shape, 1)
    sels = [lane == idx[:, kk:kk + 1] for kk in range(TOP_K)]
    onehot = jnp.zeros(idx.shape, F32)
    for sel in sels:
        onehot = onehot + jnp.where(sel, 1.0, 0.0)
    return sels, onehot


def _outproj_kernel(hr_ref, hd_ref, x_ref, wt_ref, wb_ref, g_ref, b_ref, wr_ref, br_ref,
                    x1_ref, x1p_ref, idx_ref, gate_ref, cnt_ref, carry_s):
    i = pl.program_id(0)
    tm = x_ref.shape[0]

    @pl.when(i == 0)
    def _():
        carry_s[...] = jnp.zeros_like(carry_s)

    mix = _dot(hr_ref[...], wt_ref[...]) + _dot(hd_ref[...], wb_ref[...])
    x1 = _layer_norm(DEEPNORM_ALPHA * x_ref[...] + mix, g_ref[...], b_ref[...])
    x1_ref[...] = x1
    _store_row_tiles(x1p_ref, _pack_row(x1))

    logits = jnp.dot(x1, wr_ref[...], preferred_element_type=F32,
                     precision=lax.Precision.HIGHEST) + br_ref[...]
    lane = lax.broadcasted_iota(jnp.int32, (tm, LANES), 1).astype(F32)
    work = logits
    onehot = jnp.zeros((tm, LANES), F32)
    vals, idxs = [], []
    for _ in range(TOP_K):
        mx = jnp.max(work, axis=-1, keepdims=True)
        idx = jnp.min(jnp.where(work == mx, lane, float(LANES)), axis=-1, keepdims=True)
        sel = lane == idx
        work = jnp.where(sel, -jnp.inf, work)
        onehot = onehot + jnp.where(sel, 1.0, 0.0)
        vals.append(mx)
        idxs.append(idx)
    exps = [jnp.exp(vv - vals[0]) for vv in vals]
    den = exps[0] + exps[1] + exps[2] + exps[3]
    idx_out = jnp.zeros((tm, LANES), F32)
    gate_out = jnp.zeros((tm, LANES), F32)
    for kk in range(TOP_K):
        slot = lane == float(kk)
        idx_out = jnp.where(slot, idxs[kk], idx_out)
        gate_out = jnp.where(slot, exps[kk] / den, gate_out)
    idx_ref[...] = idx_out.astype(jnp.int32)
    gate_ref[...] = gate_out
    total = carry_s[0:1, :] + jnp.sum(onehot, axis=0, keepdims=True)
    carry_s[...] = jnp.broadcast_to(total, carry_s.shape)
    cnt_ref[...] = jnp.broadcast_to(total, cnt_ref.shape)


def _outproj_ln_router(hr, hd, x, w_top, w_bot, g, b, w_r, b_r, tm):
    n = x.shape[0]
    const = lambda shape: pl.BlockSpec(shape, lambda i: (0, 0))
    rowb = lambda cols: pl.BlockSpec((tm, cols), lambda i: (i, 0))
    out_shape = (jax.ShapeDtypeStruct((n, D_MODEL), F32),
                 jax.ShapeDtypeStruct((n * ROW_TILE, LANES), jnp.int32),
                 jax.ShapeDtypeStruct((n, LANES), jnp.int32),
                 jax.ShapeDtypeStruct((n, LANES), F32),
                 jax.ShapeDtypeStruct((SUBLANES, LANES), F32))
    return pl.pallas_call(
        _outproj_kernel,
        grid=(n // tm,),
        in_specs=[rowb(RWKV_WIDTH), rowb(DIFF_WIDTH), rowb(D_MODEL),
                  const((RWKV_WIDTH, D_MODEL)), const((DIFF_WIDTH, D_MODEL)),
                  const((1, D_MODEL)), const((1, D_MODEL)),
                  const((D_MODEL, LANES)), const((1, LANES))],
        out_specs=(rowb(D_MODEL), pl.BlockSpec((tm * ROW_TILE, LANES), lambda i: (i, 0)),
                   rowb(LANES), rowb(LANES), const((SUBLANES, LANES))),
        out_shape=out_shape,
        scratch_shapes=[pltpu.VMEM((SUBLANES, LANES), F32)],
        compiler_params=_cparams(("arbitrary",), VMEM_LIMIT),
        name="outproj_ln_router",
    )(hr, hd, x, w_top, w_bot, g, b, w_r, b_r)


def _dest_kernel(idx_ref, ps_ref, dest_ref, carry_s):
    i = pl.program_id(0)
    tm = idx_ref.shape[0]

    @pl.when(i == 0)
    def _():
        carry_s[...] = jnp.broadcast_to(ps_ref[...], carry_s.shape)

    sels, onehot = _expert_onehots(idx_ref[...])
    ti = lax.broadcasted_iota(jnp.int32, (tm, tm), 0)
    tj = lax.broadcasted_iota(jnp.int32, (tm, tm), 1)
    before = jnp.where(tj < ti, 1.0, 0.0).astype(BF16)
    cum = _dot(before, onehot.astype(BF16)) + carry_s[0:1, :]
    lane = lax.broadcasted_iota(jnp.int32, (tm, LANES), 1)
    dest = jnp.zeros((tm, LANES), F32)
    for kk in range(TOP_K):
        dk = jnp.sum(jnp.where(sels[kk], cum, 0.0), axis=-1, keepdims=True)
        dest = jnp.where(lane == kk, dk, dest)
    dest_ref[...] = dest.astype(jnp.int32)
    total = carry_s[0:1, :] + jnp.sum(onehot, axis=0, keepdims=True)
    carry_s[...] = jnp.broadcast_to(total, carry_s.shape)


def _row_dest(idx, pstarts, tm):
    n = idx.shape[0]
    return pl.pallas_call(
        _dest_kernel,
        grid=(n // tm,),
        in_specs=[pl.BlockSpec((tm, LANES), lambda i: (i, 0)), pl.BlockSpec((1, LANES), lambda i: (0, 0))],
        out_specs=pl.BlockSpec((tm, LANES), lambda i: (i, 0)),
        out_shape=jax.ShapeDtypeStruct((n, LANES), jnp.int32),
        scratch_shapes=[pltpu.VMEM((SUBLANES, LANES), F32)],
        compiler_params=_cparams(("arbitrary",)),
        name="moe_row_dest",
    )(idx, pstarts)


def _wait_rows(ref, rows, sem):
    view = ref.at[pl.ds(0, rows)]
    pltpu.make_async_copy(view, view, sem).wait()


def _dispatch_kernel(dest_ref, x_ref, xs_in, xs_hbm, sem):
    del xs_in
    tt = x_ref.shape[0] // ROW_TILE

    def issue(t, c):
        src = x_ref.at[pl.ds(pl.multiple_of(t * ROW_TILE, ROW_TILE), ROW_TILE), :]
        for kk in range(TOP_K):
            pltpu.make_async_copy(src, xs_hbm.at[dest_ref[t * TOP_K + kk]], sem).start()
        return c

    lax.fori_loop(0, tt, issue, 0)
    for _ in range(TOP_K):
        _wait_rows(xs_hbm, tt, sem)


def _dispatch(dest_flat, x1p2, n_rows, tt):
    n = x1p2.shape[0] // ROW_TILE
    zeros = jnp.zeros((n_rows, ROW_TILE, LANES), jnp.int32)
    return pl.pallas_call(
        _dispatch_kernel,
        grid=(n // tt,),
        in_specs=[pl.BlockSpec((tt * TOP_K,), lambda i: (i,), memory_space=pltpu.SMEM),
                  pl.BlockSpec((tt * ROW_TILE, LANES), lambda i: (i, 0)),
                  pl.BlockSpec(memory_space=pl.ANY)],
        out_specs=pl.BlockSpec(memory_space=pl.ANY),
        out_shape=jax.ShapeDtypeStruct((n_rows, ROW_TILE, LANES), jnp.int32),
        scratch_shapes=[pltpu.SemaphoreType.DMA],
        input_output_aliases={2: 0},
        compiler_params=_cparams(("arbitrary",)),
        name="moe_dispatch",
    )(dest_flat, x1p2, zeros)


def _dot_f32_weights(x, w_ref):
    acc = None
    for kc in range(x.shape[1] // MXU_DEPTH):
        sl = slice(kc * MXU_DEPTH, (kc + 1) * MXU_DEPTH)
        part = _dot(x[:, sl], w_ref[0, sl, :].astype(BF16))
        acc = part if acc is None else acc + part
    return acc


def _gu_kernel(be_ref, nv_ref, xs_ref, wg_ref, wu_ref, bg_ref, bu_ref, o_ref):
    j = pl.program_id(1)
    bm = o_ref.shape[0]

    @pl.when(j < nv_ref[0])
    def _():
        lo, hi = _unpack_row(_load_row_tiles(xs_ref, 0, bm))
        x = jnp.concatenate([lo, hi], axis=1).astype(BF16)
        g = _dot_f32_weights(x, wg_ref) + bg_ref[0]
        u = _dot_f32_weights(x, wu_ref) + bu_ref[0]
        g = jnp.minimum(g, SWIGLU_LIMIT)
        u = jnp.clip(u, -SWIGLU_LIMIT, SWIGLU_LIMIT)
        o_ref[...] = ((u + 1.0) * (g * _sigmoid(SWIGLU_ALPHA * g))).astype(o_ref.dtype)

    @pl.when(j >= nv_ref[0])
    def _():
        o_ref[...] = jnp.zeros_like(o_ref)


def _moe_gate_up(block_e, n_valid, xs2, w_gu, b_gu, bm, tn):
    n_rows = xs2.shape[0] // ROW_TILE
    n_blocks = n_rows // bm
    nt = D_FF // tn
    grid_spec = pltpu.PrefetchScalarGridSpec(
        num_scalar_prefetch=2,
        grid=(nt, n_blocks),
        in_specs=[pl.BlockSpec((bm * ROW_TILE, LANES), lambda n, j, be, nv: (j, 0)),
                  pl.BlockSpec((1, D_MODEL, tn), lambda n, j, be, nv: (be[j], 0, n)),
                  pl.BlockSpec((1, D_MODEL, tn), lambda n, j, be, nv: (be[j], 0, nt + n)),
                  pl.BlockSpec((1, 1, tn), lambda n, j, be, nv: (be[j], 0, n)),
                  pl.BlockSpec((1, 1, tn), lambda n, j, be, nv: (be[j], 0, nt + n))],
        out_specs=pl.BlockSpec((bm, tn), lambda n, j, be, nv: (j, n)))
    return pl.pallas_call(
        _gu_kernel,
        grid_spec=grid_spec,
        out_shape=jax.ShapeDtypeStruct((n_rows, D_FF), BF16),
        compiler_params=_cparams(("arbitrary", "arbitrary"), VMEM_LIMIT),
        name="moe_gate_up",
    )(block_e, n_valid, xs2, w_gu, w_gu, b_gu, b_gu)


def _dn_kernel(be_ref, nv_ref, a_ref, w_ref, b_ref, o_ref):
    j = pl.program_id(0)
    del be_ref

    @pl.when(j < nv_ref[0])
    def _():
        y = _dot_f32_weights(a_ref[...], w_ref) + b_ref[0]
        _store_row_tiles(o_ref, _pack_row(y))

    @pl.when(j >= nv_ref[0])
    def _():
        o_ref[...] = jnp.zeros_like(o_ref)


def _moe_down(block_e, n_valid, act, w_dn, b_dn, bm):
    n_rows = act.shape[0]
    n_blocks = n_rows // bm
    grid_spec = pltpu.PrefetchScalarGridSpec(
        num_scalar_prefetch=2,
        grid=(n_blocks,),
        in_specs=[pl.BlockSpec((bm, D_FF), lambda j, be, nv: (j, 0)),
                  pl.BlockSpec((1, D_FF, D_MODEL), lambda j, be, nv: (be[j], 0, 0)),
                  pl.BlockSpec((1, 1, D_MODEL), lambda j, be, nv: (be[j], 0, 0))],
        out_specs=pl.BlockSpec((bm * ROW_TILE, LANES), lambda j, be, nv: (j, 0)))
    return pl.pallas_call(
        _dn_kernel,
        grid_spec=grid_spec,
        out_shape=jax.ShapeDtypeStruct((n_rows * ROW_TILE, LANES), jnp.int32),
        compiler_params=_cparams(("arbitrary",), VMEM_LIMIT),
        name="moe_down",
    )(block_e, n_valid, act, w_dn, b_dn)


def _combine_kernel(dest_ref, next_ref, gate_ref, x1_ref, g_ref, b_ref, ys_hbm, o_ref, buf, sem):
    i = pl.program_id(0)
    tt = x1_ref.shape[0]
    slot = i % 2

    def fetch(d_ref, s):
        def body(t, c):
            for kk in range(TOP_K):
                row = pl.multiple_of((kk * tt + t) * ROW_TILE, ROW_TILE)
                pltpu.make_async_copy(ys_hbm.at[d_ref[t * TOP_K + kk]],
                                      buf.at[s, pl.ds(row, ROW_TILE), :], sem.at[s]).start()
            return c
        lax.fori_loop(0, tt, body, 0)

    @pl.when(i == 0)
    def _():
        fetch(dest_ref, 0)

    @pl.when(i + 1 < pl.num_programs(0))
    def _():
        fetch(next_ref, 1 - slot)

    for kk in range(TOP_K):
        view = buf.at[slot, pl.ds(kk * tt * ROW_TILE, tt * ROW_TILE), :]
        pltpu.make_async_copy(view, view, sem.at[slot]).wait()

    gates = gate_ref[...]
    cur = buf.at[slot]
    acc_lo = jnp.zeros((tt, ROW_WORDS), F32)
    acc_hi = jnp.zeros((tt, ROW_WORDS), F32)
    for kk in range(TOP_K):
        lo, hi = _unpack_row(_load_row_tiles(cur, kk * tt, tt))
        gk = gates[:, kk:kk + 1]
        acc_lo = acc_lo + gk * lo
        acc_hi = acc_hi + gk * hi
    ffn = jnp.concatenate([acc_lo, acc_hi], axis=1)
    o_ref[...] = _layer_norm(DEEPNORM_ALPHA * x1_ref[...] + ffn, g_ref[...], b_ref[...])


def _combine_ln(dest_flat, gates, x1, g, b, ys3, tt):
    n = x1.shape[0]
    last = n // tt - 1
    const = lambda shape: pl.BlockSpec(shape, lambda i: (0, 0))
    return pl.pallas_call(
        _combine_kernel,
        grid=(n // tt,),
        in_specs=[pl.BlockSpec((tt * TOP_K,), lambda i: (i,), memory_space=pltpu.SMEM),
                  pl.BlockSpec((tt * TOP_K,), lambda i: (jnp.minimum(i + 1, last),), memory_space=pltpu.SMEM),
                  pl.BlockSpec((tt, LANES), lambda i: (i, 0)),
                  pl.BlockSpec((tt, D_MODEL), lambda i: (i, 0)),
                  const((1, D_MODEL)), const((1, D_MODEL)),
                  pl.BlockSpec(memory_space=pl.ANY)],
        out_specs=pl.BlockSpec((tt, D_MODEL), lambda i: (i, 0)),
        out_shape=jax.ShapeDtypeStruct((n, D_MODEL), F32),
        scratch_shapes=[pltpu.VMEM((2, TOP_K * tt * ROW_TILE, LANES), jnp.int32),
                        pltpu.SemaphoreType.DMA((2,))],
        compiler_params=_cparams(("arbitrary",), VMEM_LIMIT),
        name="moe_combine_ln",
    )(dest_flat, dest_flat, gates, x1, g, b, ys3)


def _pad_cols(a, width):
    return jnp.pad(a, ((0, 0), (0, width - a.shape[1])))


def _lora_layout(a):
    dw = a[:, :DECAY_LORA]
    da = a[:, DECAY_LORA:DECAY_LORA + AAA_LORA]
    dg = a[:, DECAY_LORA + AAA_LORA:]
    return jnp.concatenate([_pad_cols(dw, LANES), _pad_cols(da, LANES), _pad_cols(dg, 2 * LANES)], axis=1)


def _moe_ffn(x1, x1p2, idx, gates, counts, w_gu, b_gu, w_dn, b_dn, ln_g, ln_b, bm, tn, tt):
    n = x1.shape[0]
    nk = n * TOP_K
    n_blocks = nk // bm + N_EXPERTS
    n_rows = n_blocks * bm
    cnt = counts[0, :N_EXPERTS].astype(jnp.int32)
    padded = ((cnt + bm - 1) // bm) * bm
    pends = jnp.cumsum(padded)
    pstarts = jnp.pad(pends - padded, (0, LANES - N_EXPERTS)).astype(F32).reshape(1, LANES)
    block_start = jnp.arange(n_blocks, dtype=jnp.int32) * bm
    block_e = jnp.minimum(jnp.sum(pends[None, :] <= block_start[:, None], axis=1), N_EXPERTS - 1).astype(jnp.int32)
    n_valid = (pends[-1:] // bm).astype(jnp.int32)

    dest_flat = _row_dest(idx, pstarts, tt)[:, :TOP_K].reshape(nk)
    xs3 = _dispatch(dest_flat, x1p2, n_rows, tt)
    act = _moe_gate_up(block_e, n_valid, xs3.reshape(n_rows * ROW_TILE, LANES), w_gu,
                       b_gu.reshape(N_EXPERTS, 1, 2 * D_FF), bm, tn)
    ys2 = _moe_down(block_e, n_valid, act, w_dn, b_dn.reshape(N_EXPERTS, 1, D_MODEL), bm)
    return _combine_ln(dest_flat, gates, x1, ln_g, ln_b, ys2.reshape(n_rows, ROW_TILE, LANES), tt)


def _layer(x, w_in, shift_mu, w0, w_up, a0, a_up, g_up, k_k, k_a, r_k, gn_g, gn_b,
           lq1, lk1, lq2, lk2, subln_g, w_out, ln1_g, ln1_b,
           w_router, b_router, w_gu, b_gu, w_dn, b_dn, ln2_g, ln2_b, lambda_init,
           tm_in=2048, tseq=512, tq=256, tm_out=256, bm=512, tn=1024, tt=256):
    batch, seq, d = x.shape
    n = batch * seq
    rw = 3 * RWKV_WIDTH
    rcols = rw + DECAY_LORA + AAA_LORA + GATE_LORA
    row = lambda a: a.reshape(1, -1)

    xf = x.reshape(n, d)
    xb = xf.astype(BF16)
    w_r = jnp.concatenate([w_in[:, :rw], _lora_layout(w_in[:, rw:rcols])], axis=1).astype(BF16)
    w_d = w_in[:, rcols:].astype(BF16)
    mu = jnp.concatenate([row(shift_mu)[:, :rw], _lora_layout(row(shift_mu)[:, rw:])], axis=1)
    p_r = _matmul(xb, w_r, F32, tm_in, 512)
    p_d = _matmul(xb, w_d, BF16, tm_in, 512)

    pad_rows = lambda a, rows: jnp.pad(a, ((0, rows - a.shape[0]), (0, 0))).astype(BF16)
    h_r = _rwkv(p_r, mu, row(w0), row(a0), row(k_k), row(k_a), row(r_k), row(gn_g), row(gn_b),
                pad_rows(w_up, LANES), pad_rows(a_up, LANES), pad_rows(g_up, 2 * LANES), batch, seq, tseq)
    h_d = _diff_attention(p_d, row(lq1), row(lk1), row(lq2), row(lk2), row(subln_g), lambda_init,
                          batch, seq, tq)

    w_ob = w_out.astype(BF16)
    w_rp = _pad_cols(w_router, LANES)
    b_rp = jnp.concatenate([row(b_router), jnp.full((1, LANES - N_EXPERTS), NEG_BIG, F32)], axis=1)
    x1, x1p2, idx, gates, counts = _outproj_ln_router(
        h_r, h_d, xf, w_ob[:RWKV_WIDTH], w_ob[RWKV_WIDTH:], row(ln1_g), row(ln1_b), w_rp, b_rp, tm_out)
    out = _moe_ffn(x1, x1p2, idx, gates, counts, w_gu, b_gu, w_dn, b_dn, row(ln2_g), row(ln2_b), bm, tn, tt)
    return out.reshape(batch, seq, d)


def kernel(x, w_in, shift_mu, w0, w_up, a0, a_up, g_up, k_k, k_a, r_k, gn_g, gn_b, lq1, lk1, lq2, lk2,
           subln_g, w_out, ln1_g, ln1_b, w_router, b_router, w_gu, b_gu, w_dn, b_dn, ln2_g, ln2_b):
    for l in range(DEPTH):
        lambda_init = 0.8 - 0.6 * math.exp(-0.3 * l)
        x = _layer(x, w_in[l], shift_mu[l], w0[l], w_up[l], a0[l], a_up[l], g_up[l], k_k[l], k_a[l],
                   r_k[l], gn_g[l], gn_b[l], lq1[l], lk1[l], lq2[l], lk2[l], subln_g[l], w_out[l],
                   ln1_g[l], ln1_b[l], w_router[l], b_router[l], w_gu[l], b_gu[l], w_dn[l], b_dn[l],
                   ln2_g[l], ln2_b[l], lambda_init)
    return x
```

```python
import functools
import math

import jax
import jax.numpy as jnp
from jax import lax
from jax.experimental import pallas as pl
from jax.experimental.pallas import tpu as pltpu

F32 = jnp.float32
BF16 = jnp.bfloat16

D_MODEL = 2048
RWKV_HEAD = 64
RWKV_WIDTH = 1024
RWKV_HEADS = 16
DECAY_LORA = 64
AAA_LORA = 64
GATE_LORA = 160
DIFF_HEAD = 64
DIFF_VDIM = 128
DIFF_HEADS = 8
DIFF_WIDTH = 1024
N_EXPERTS = 32
TOP_K = 4
D_FF = 2048
SWIGLU_LIMIT = 7.0
SWIGLU_ALPHA = 1.702
LN_EPS = 1e-5
GN_EPS = RWKV_HEAD * 1e-5
RMS_EPS = 1e-5
NEG_BIG = -1e30
DEPTH = 1
DEEPNORM_ALPHA = (2.0 * DEPTH) ** 0.25

LANES = 128
SUBLANES = 8
ROW_WORDS = D_MODEL // 2
ROW_TILE = ROW_WORDS // LANES
RWKV_GROUP = 256
RWKV_CHUNK = 64
LORA_COLS = 512
VMEM_LIMIT = 56 * 1024 * 1024
HIGH_HALF = -65536
MXU_DEPTH = 256


def _cparams(sem, vmem=None):
    return pltpu.CompilerParams(dimension_semantics=sem, vmem_limit_bytes=vmem)


def _dot(a, b):
    return jnp.dot(a, b, preferred_element_type=F32)


def _dot_nt(a, b):
    return lax.dot_general(a, b, (((1,), (1,)), ((), ())), preferred_element_type=F32)


def _dot_tn(a, b):
    return lax.dot_general(a, b, (((0,), (0,)), ((), ())), preferred_element_type=F32)


def _split3(x):
    h = x.astype(BF16)
    r = x - h.astype(F32)
    m = r.astype(BF16)
    l = (r - m.astype(F32)).astype(BF16)
    return h, m, l


def _dot_exact_rhs(x, ones):
    h, m, l = _split3(x)
    return _dot(h, ones) + _dot(m, ones) + _dot(l, ones)


def _dot_exact_lhs(ones, x):
    h, m, l = _split3(x)
    return _dot(ones, h) + _dot(ones, m) + _dot(ones, l)


def _sigmoid(x):
    return 1.0 / (1.0 + jnp.exp(-x))


def _matmul_kernel(x_ref, w_ref, o_ref):
    o_ref[...] = _dot(x_ref[...], w_ref[...]).astype(o_ref.dtype)


def _matmul(x, w, out_dtype, tm, tn):
    m, k = x.shape
    n = w.shape[1]
    return pl.pallas_call(
        _matmul_kernel,
        grid=(n // tn, m // tm),
        in_specs=[pl.BlockSpec((tm, k), lambda j, i: (i, 0)),
                  pl.BlockSpec((k, tn), lambda j, i: (0, j))],
        out_specs=pl.BlockSpec((tm, tn), lambda j, i: (i, j)),
        out_shape=jax.ShapeDtypeStruct((m, n), out_dtype),
        compiler_params=_cparams(("parallel", "parallel"), VMEM_LIMIT),
        name="in_proj",
    )(x, w)


def _rwkv_kernel(r_ref, k_ref, v_ref, l_ref, mur_ref, muk_ref, muv_ref, mul_ref,
                 w0_ref, a0_ref, kk_ref, ka_ref, rk_ref, gng_ref, gnb_ref,
                 wup_ref, aup_ref, gup_ref, o_ref,
                 pr_s, pk_s, pv_s, pl_s, state_s, r_s, w_s, k_s, v_s, a_s, b_s, g_s):
    s = pl.program_id(2)
    T = r_ref.shape[0]
    G = RWKV_GROUP
    C = RWKV_CHUNK
    NG = r_ref.shape[1] // G
    groups = [slice(g * G, (g + 1) * G) for g in range(NG)]

    @pl.when(s == 0)
    def _():
        state_s[...] = jnp.zeros_like(state_s)
        pr_s[...] = jnp.zeros_like(pr_s)
        pk_s[...] = jnp.zeros_like(pk_s)
        pv_s[...] = jnp.zeros_like(pv_s)
        pl_s[...] = jnp.zeros_like(pl_s)

    row = lax.broadcasted_iota(jnp.int32, (T, 1), 0)

    def shift(ref, prev_s, mu_ref):
        p = ref[...]
        prev = jnp.where(row == 0, prev_s[...], pltpu.roll(p, 1, 0))
        prev_s[...] = p[T - 1:T, :]
        return p + (prev - p) * mu_ref[...]

    r = shift(r_ref, pr_s, mur_ref)
    k = shift(k_ref, pk_s, muk_ref)
    v = shift(v_ref, pv_s, muv_ref)
    lo = shift(l_ref, pl_s, mul_ref)
    dw = lo[:, 0:LANES]
    da = lo[:, LANES:2 * LANES]
    dg = lo[:, 2 * LANES:4 * LANES]

    wpre = w0_ref[...] + _dot(jnp.tanh(dw).astype(BF16), wup_ref[...])
    sp = jnp.maximum(-wpre, 0.0) + jnp.log(1.0 + jnp.exp(-jnp.abs(wpre)))
    wlog = -jnp.exp(-sp - 0.5)
    a_sig = _sigmoid(a0_ref[...] + _dot(da.astype(BF16), aup_ref[...]))
    gate = _dot(_sigmoid(dg).astype(BF16), gup_ref[...])

    gi = lax.broadcasted_iota(jnp.int32, (G, G), 0)
    gj = lax.broadcasted_iota(jnp.int32, (G, G), 1)
    same_head = (gi // RWKV_HEAD) == (gj // RWKV_HEAD)
    head_ones = jnp.where(same_head, 1.0, 0.0).astype(BF16)

    def head_sum(x):
        return jnp.concatenate([_dot_exact_rhs(x[:, g], head_ones) for g in groups], axis=1)

    kk = k * kk_ref[...]
    nrm = jnp.sqrt(head_sum(kk * kk))
    kk = kk / jnp.maximum(nrm, 1e-12)
    k2 = k * (1.0 + (a_sig - 1.0) * ka_ref[...])

    r_s[...] = r
    w_s[...] = wlog
    k_s[...] = k2
    v_s[...] = v
    a_s[...] = -kk
    b_s[...] = kk * a_sig
    g_s[...] = gate

    ci = lax.broadcasted_iota(jnp.int32, (C, C), 0)
    cj = lax.broadcasted_iota(jnp.int32, (C, C), 1)
    tri = jnp.where(cj <= ci, 1.0, 0.0).astype(BF16)
    mt = lax.broadcasted_iota(jnp.int32, (C, G), 0)
    mtp = lax.broadcasted_iota(jnp.int32, (C, G), 1) % C
    strict = mtp < mt
    incl = mtp <= mt
    ceye = jnp.where(mtp == mt, 1.0, 0.0)

    def bd(xc):
        return jnp.where(same_head, jnp.concatenate([xc, xc, xc, xc], axis=0), jnp.zeros((), BF16))

    def each(fn, *lists):
        return [fn(*xs) for xs in zip(*lists)]

    def bf(x):
        return x.astype(BF16)

    def chunk(c, carry):
        sl = pl.ds(pl.multiple_of(c * C, C), C)
        rc = [r_s[sl, g] for g in groups]
        wc = [w_s[sl, g] for g in groups]
        kc = [k_s[sl, g] for g in groups]
        vc = [v_s[sl, g] for g in groups]
        ac = [a_s[sl, g] for g in groups]
        bc = [b_s[sl, g] for g in groups]
        cum = each(lambda w: _dot_exact_lhs(tri, w), wc)
        tot = each(lambda x: x[C - 1:C, :], cum)
        ginv = each(lambda x: jnp.exp(-x), cum)
        gend = each(lambda x, t: jnp.exp(t - x), cum, tot)
        r_c = each(lambda x, g: bf(x * jnp.exp(g)), rc, cum)
        a_c = each(lambda x, g, w: bf(x * jnp.exp(g - w)), ac, cum, wc)
        v_c = each(bf, vc)
        k_bd = each(lambda x, g: bd(bf(x * g)), kc, ginv)
        b_bd = each(lambda x, g: bd(bf(x * g)), bc, ginv)
        ke_c = each(lambda x, g: bf(x * g), kc, gend)
        be_c = each(lambda x, g: bf(x * g), bc, gend)
        v_bd = each(bd, v_c)
        a_bd = each(bd, a_c)

        ar = each(lambda a, r: jnp.concatenate([a, r], axis=0), a_c, r_c)
        arb = each(_dot_nt, ar, b_bd)
        ark = each(_dot_nt, ar, k_bd)
        l_ab = each(lambda x: jnp.where(strict, x[:C], 0.0), arb)
        l_ak = each(lambda x: bf(jnp.where(strict, x[:C], 0.0)), ark)
        l_rb = each(lambda x: bf(jnp.where(incl, x[C:], 0.0)), arb)
        l_rk = each(lambda x: bf(jnp.where(incl, x[C:], 0.0)), ark)

        p = each(bf, l_ab)
        p_bd = each(bd, p)
        tinv = each(lambda x: ceye + x, l_ab)
        for _ in range(int(math.log2(C)) - 1):
            p = each(lambda x, y: bf(_dot(x, y)), p, p_bd)
            p_bd = each(bd, p)
            tinv = each(lambda t, y: t + _dot(bf(t), y), tinv, p_bd)
        tb = each(bf, tinv)

        akv_bd = each(lambda a, b: bd(bf(_dot(a, b))), l_ak, v_bd)
        w_c = each(lambda a, b: bf(_dot(a, b)), tb, a_bd)
        st = [state_s[g] for g in range(NG)]
        stb = each(bf, st)
        u_c = each(lambda w, s0, t, x: bf(_dot_nt(w, s0) + _dot(t, x)), w_c, stb, tb, akv_bd)
        u_bd = each(bd, u_c)
        y = each(lambda r, s0, lb, lk, u, vv: _dot_nt(r, s0) + _dot(jnp.concatenate([lb, lk], axis=1),
                                                                    jnp.concatenate([u, vv], axis=0)),
                 r_c, stb, l_rb, l_rk, u_bd, v_bd)
        new = each(lambda s0, t, u, vv, b, kx: s0 * jnp.exp(t) + jnp.where(
            same_head, _dot_tn(jnp.concatenate([u, vv], axis=0), jnp.concatenate([b, kx], axis=0)), 0.0),
            st, tot, u_c, v_c, be_c, ke_c)
        for g in range(NG):
            state_s[g] = new[g]

        mean = each(lambda x: _dot_exact_rhs(x, head_ones) * (1.0 / RWKV_HEAD), y)
        d = each(lambda x, m: x - m, y, mean)
        var = each(lambda x: _dot_exact_rhs(x * x, head_ones) * (1.0 / RWKV_HEAD), d)
        bonus = each(lambda r, kx, vv, g: _dot_exact_rhs(r * kx * rk_ref[:, g], head_ones) * vv,
                     rc, kc, vc, groups)
        for g, dd, vr, bo in zip(groups, d, var, bonus):
            yn = dd * lax.rsqrt(vr + GN_EPS) * gng_ref[:, g] + gnb_ref[:, g]
            o_ref[sl, g] = ((yn + bo) * g_s[sl, g]).astype(o_ref.dtype)
        return carry

    lax.fori_loop(0, T // C, chunk, 0)


def _rwkv(p_r, mu, w0, a0, k_k, k_a, r_k, gn_g, gn_b, w_up, a_up, g_up, batch, seq, tseq, ng):
    n = batch * seq
    G = ng * RWKV_GROUP
    nq = RWKV_WIDTH // G
    ns = seq // tseq
    lora_blk = 3 * RWKV_WIDTH // LORA_COLS

    def tok(off):
        return pl.BlockSpec((tseq, G), lambda b, q, s: (b * ns + s, off + q))

    def par(off):
        return pl.BlockSpec((1, G), lambda b, q, s: (0, off + q))

    in_specs = [
        tok(0), tok(nq), tok(2 * nq),
        pl.BlockSpec((tseq, LORA_COLS), lambda b, q, s: (b * ns + s, lora_blk)),
        par(0), par(nq), par(2 * nq),
        pl.BlockSpec((1, LORA_COLS), lambda b, q, s: (0, lora_blk)),
        par(0), par(0), par(0), par(0), par(0), par(0), par(0),
        pl.BlockSpec((LANES, G), lambda b, q, s: (0, q)),
        pl.BlockSpec((LANES, G), lambda b, q, s: (0, q)),
        pl.BlockSpec((2 * LANES, G), lambda b, q, s: (0, q)),
    ]
    scratch = [pltpu.VMEM((1, G), F32), pltpu.VMEM((1, G), F32), pltpu.VMEM((1, G), F32),
               pltpu.VMEM((1, LORA_COLS), F32), pltpu.VMEM((ng, RWKV_GROUP, RWKV_GROUP), F32)]
    scratch += [pltpu.VMEM((tseq, G), F32) for _ in range(7)]
    return pl.pallas_call(
        _rwkv_kernel,
        grid=(batch, nq, ns),
        in_specs=in_specs,
        out_specs=pl.BlockSpec((tseq, G), lambda b, q, s: (b * ns + s, q)),
        out_shape=jax.ShapeDtypeStruct((n, RWKV_WIDTH), BF16),
        scratch_shapes=scratch,
        compiler_params=_cparams(("parallel", "parallel", "arbitrary"), VMEM_LIMIT),
        name="rwkv7",
    )(p_r, p_r, p_r, p_r, mu, mu, mu, mu, w0, a0, k_k, k_a, r_k, gn_g, gn_b, w_up, a_up, g_up)


def _attn_kernel(q_ref, k_ref, v_ref, lq1_ref, lk1_ref, lq2_ref, lk2_ref, g_ref, o_ref, vt_s, *, lambda_init):
    i = pl.program_id(2)
    tq = q_ref.shape[0]

    @pl.when(i == 0)
    def _():
        for j in range(vt_s.shape[0]):
            vt_s[j] = v_ref[j * tq:(j + 1) * tq, :].astype(F32).T.astype(BF16)

    drow = lax.broadcasted_iota(jnp.int32, (DIFF_VDIM, 1), 0)
    qt = (q_ref[...].astype(F32) * (DIFF_HEAD ** -0.5)).T
    q1 = jnp.where(drow < DIFF_HEAD, qt, 0.0).astype(BF16)
    q2 = jnp.where(drow >= DIFF_HEAD, qt, 0.0).astype(BF16)
    lam = (jnp.exp(jnp.sum(lq1_ref[...] * lk1_ref[...], axis=-1, keepdims=True))
           - jnp.exp(jnp.sum(lq2_ref[...] * lk2_ref[...], axis=-1, keepdims=True)) + lambda_init)

    def step(j, carry, mask):
        m, l, acc = carry[0::3], carry[1::3], carry[2::3]
        kj = k_ref[pl.ds(pl.multiple_of(j * tq, tq), tq), :]
        vtj = vt_s[j]
        sc = [_dot(kj, qm) for qm in (q1, q2)]
        if mask is not None:
            sc = [jnp.where(mask, s, NEG_BIG) for s in sc]
        m_new = [jnp.maximum(mm, jnp.max(s, axis=0, keepdims=True)) for mm, s in zip(m, sc)]
        alpha = [jnp.exp(mm - mn) for mm, mn in zip(m, m_new)]
        p = [jnp.exp(s - mn) for s, mn in zip(sc, m_new)]
        l = [a * ll + jnp.sum(pp, axis=0, keepdims=True) for a, ll, pp in zip(alpha, l, p)]
        pv = [_dot(vtj, pp.astype(BF16)) for pp in p]
        acc = [a * ac + x for a, ac, x in zip(alpha, acc, pv)]
        return m_new[0], l[0], acc[0], m_new[1], l[1], acc[1]

    zero1 = jnp.zeros((1, tq), F32)
    neg1 = jnp.full((1, tq), NEG_BIG, F32)
    zacc = jnp.zeros((DIFF_VDIM, tq), F32)
    carry = (neg1, zero1, zacc, neg1, zero1, zacc)
    carry = lax.fori_loop(0, i, lambda j, c: step(j, c, None), carry)
    key = lax.broadcasted_iota(jnp.int32, (tq, tq), 0)
    qry = lax.broadcasted_iota(jnp.int32, (tq, tq), 1)
    m1, l1, acc1, m2, l2, acc2 = step(i, carry, key <= qry)
    o = acc1 / l1 - lam * (acc2 / l2)
    o = o * lax.rsqrt(jnp.mean(o * o, axis=0, keepdims=True) + RMS_EPS) * g_ref[...]
    o_ref[...] = (o * (1.0 - lambda_init)).T.astype(o_ref.dtype)


def _diff_attention(p_d, lq1, lk1, lq2, lk2, subln_g, lambda_init, batch, seq, tq):
    n = batch * seq
    nq = seq // tq
    H = DIFF_HEADS
    small = pl.BlockSpec((1, DIFF_HEAD), lambda b, h, i: (0, 0))
    return pl.pallas_call(
        functools.partial(_attn_kernel, lambda_init=lambda_init),
        grid=(batch, H, nq),
        in_specs=[pl.BlockSpec((tq, DIFF_VDIM), lambda b, h, i: (b * nq + i, h)),
                  pl.BlockSpec((seq, DIFF_VDIM), lambda b, h, i: (b, H + h)),
                  pl.BlockSpec((seq, DIFF_VDIM), lambda b, h, i: (b, 2 * H + h)),
                  small, small, small, small,
                  pl.BlockSpec((DIFF_VDIM, 1), lambda b, h, i: (0, 0))],
        out_specs=pl.BlockSpec((tq, DIFF_VDIM), lambda b, h, i: (b * nq + i, h)),
        out_shape=jax.ShapeDtypeStruct((n, DIFF_WIDTH), BF16),
        scratch_shapes=[pltpu.VMEM((nq, DIFF_VDIM, tq), BF16)],
        compiler_params=_cparams(("parallel", "parallel", "arbitrary"), VMEM_LIMIT),
        name="diff_attn",
    )(p_d, p_d, p_d, lq1, lk1, lq2, lk2, subln_g.reshape(DIFF_VDIM, 1))


def _layer_norm(y, g, b):
    mu = jnp.mean(y, axis=-1, keepdims=True)
    d = y - mu
    var = jnp.mean(d * d, axis=-1, keepdims=True)
    return d * lax.rsqrt(var + LN_EPS) * g + b


def _store_row_tiles(ref, words):
    rows = words.shape[0]
    for s in range(ROW_TILE):
        ref[pl.ds(s, rows, stride=ROW_TILE), :] = words[:, s * LANES:(s + 1) * LANES]


def _load_row_tiles(ref, start, rows):
    parts = [ref[pl.ds(start * ROW_TILE + s, rows, stride=ROW_TILE), :] for s in range(ROW_TILE)]
    return jnp.concatenate(parts, axis=1)


def _pack_row(y):
    bits = lax.bitcast_convert_type(y.astype(BF16).astype(F32), jnp.int32)
    lo = lax.shift_right_logical(bits[:, :ROW_WORDS], 16)
    return lo | (bits[:, ROW_WORDS:] & HIGH_HALF)


def _unpack_row(words):
    lo = lax.bitcast_convert_type(lax.shift_left(words, 16), F32)
    hi = lax.bitcast_convert_type(words & HIGH_HALF, F32)
    return lo, hi


def _expert_onehots(idx):
    lane = lax.broadcasted_iota(jnp.int32, idx.shape, 1)
    sels = [lane == idx[:, kk:kk + 1] for kk in range(TOP_K)]
    onehot = jnp.zeros(idx.shape, F32)
    for sel in sels:
        onehot = onehot + jnp.where(sel, 1.0, 0.0)
    return sels, onehot


def _outproj_kernel(hr_ref, hd_ref, x_ref, wt_ref, wb_ref, g_ref, b_ref, wr_ref, br_ref,
                    x1_ref, x1p_ref, idx_ref, gate_ref, cnt_ref, carry_s):
    i = pl.program_id(0)
    tm = x_ref.shape[0]

    @pl.when(i == 0)
    def _():
        carry_s[...] = jnp.zeros_like(carry_s)

    mix = _dot(hr_ref[...], wt_ref[...]) + _dot(hd_ref[...], wb_ref[...])
    x1 = _layer_norm(DEEPNORM_ALPHA * x_ref[...] + mix, g_ref[...], b_ref[...])
    x1_ref[...] = x1
    _store_row_tiles(x1p_ref, _pack_row(x1))

    logits = jnp.dot(x1, wr_ref[...], preferred_element_type=F32,
                     precision=lax.Precision.HIGHEST) + br_ref[...]
    lane = lax.broadcasted_iota(jnp.int32, (tm, LANES), 1).astype(F32)
    work = logits
    onehot = jnp.zeros((tm, LANES), F32)
    vals, idxs = [], []
    for _ in range(TOP_K):
        mx = jnp.max(work, axis=-1, keepdims=True)
        idx = jnp.min(jnp.where(work == mx, lane, float(LANES)), axis=-1, keepdims=True)
        sel = lane == idx
        work = jnp.where(sel, -jnp.inf, work)
        onehot = onehot + jnp.where(sel, 1.0, 0.0)
        vals.append(mx)
        idxs.append(idx)
    exps = [jnp.exp(vv - vals[0]) for vv in vals]
    den = exps[0] + exps[1] + exps[2] + exps[3]
    idx_out = jnp.zeros((tm, LANES), F32)
    gate_out = jnp.zeros((tm, LANES), F32)
    for kk in range(TOP_K):
        slot = lane == float(kk)
        idx_out = jnp.where(slot, idxs[kk], idx_out)
        gate_out = jnp.where(slot, exps[kk] / den, gate_out)
    idx_ref[...] = idx_out.astype(jnp.int32)
    gate_ref[...] = gate_out
    total = carry_s[0:1, :] + jnp.sum(onehot, axis=0, keepdims=True)
    carry_s[...] = jnp.broadcast_to(total, carry_s.shape)
    cnt_ref[...] = jnp.broadcast_to(total, cnt_ref.shape)


def _outproj_ln_router(hr, hd, x, w_top, w_bot, g, b, w_r, b_r, tm):
    n = x.shape[0]
    const = lambda shape: pl.BlockSpec(shape, lambda i: (0, 0))
    rowb = lambda cols: pl.BlockSpec((tm, cols), lambda i: (i, 0))
    out_shape = (jax.ShapeDtypeStruct((n, D_MODEL), F32),
                 jax.ShapeDtypeStruct((n * ROW_TILE, LANES), jnp.int32),
                 jax.ShapeDtypeStruct((n, LANES), jnp.int32),
                 jax.ShapeDtypeStruct((n, LANES), F32),
                 jax.ShapeDtypeStruct((SUBLANES, LANES), F32))
    return pl.pallas_call(
        _outproj_kernel,
        grid=(n // tm,),
        in_specs=[rowb(RWKV_WIDTH), rowb(DIFF_WIDTH), rowb(D_MODEL),
                  const((RWKV_WIDTH, D_MODEL)), const((DIFF_WIDTH, D_MODEL)),
                  const((1, D_MODEL)), const((1, D_MODEL)),
                  const((D_MODEL, LANES)), const((1, LANES))],
        out_specs=(rowb(D_MODEL), pl.BlockSpec((tm * ROW_TILE, LANES), lambda i: (i, 0)),
                   rowb(LANES), rowb(LANES), const((SUBLANES, LANES))),
        out_shape=out_shape,
        scratch_shapes=[pltpu.VMEM((SUBLANES, LANES), F32)],
        compiler_params=_cparams(("arbitrary",), VMEM_LIMIT),
        name="outproj_ln_router",
    )(hr, hd, x, w_top, w_bot, g, b, w_r, b_r)


def _dest_kernel(idx_ref, ps_ref, dest_ref, carry_s):
    i = pl.program_id(0)
    tm = idx_ref.shape[0]

    @pl.when(i == 0)
    def _():
        carry_s[...] = jnp.broadcast_to(ps_ref[...], carry_s.shape)

    sels, onehot = _expert_onehots(idx_ref[...])
    ti = lax.broadcasted_iota(jnp.int32, (tm, tm), 0)
    tj = lax.broadcasted_iota(jnp.int32, (tm, tm), 1)
    before = jnp.where(tj < ti, 1.0, 0.0).astype(BF16)
    cum = _dot(before, onehot.astype(BF16)) + carry_s[0:1, :]
    lane = lax.broadcasted_iota(jnp.int32, (tm, LANES), 1)
    dest = jnp.zeros((tm, LANES), F32)
    for kk in range(TOP_K):
        dk = jnp.sum(jnp.where(sels[kk], cum, 0.0), axis=-1, keepdims=True)
        dest = jnp.where(lane == kk, dk, dest)
    dest_ref[...] = dest.astype(jnp.int32)
    total = carry_s[0:1, :] + jnp.sum(onehot, axis=0, keepdims=True)
    carry_s[...] = jnp.broadcast_to(total, carry_s.shape)


def _row_dest(idx, pstarts, tm):
    n = idx.shape[0]
    return pl.pallas_call(
        _dest_kernel,
        grid=(n // tm,),
        in_specs=[pl.BlockSpec((tm, LANES), lambda i: (i, 0)), pl.BlockSpec((1, LANES), lambda i: (0, 0))],
        out_specs=pl.BlockSpec((tm, LANES), lambda i: (i, 0)),
        out_shape=jax.ShapeDtypeStruct((n, LANES), jnp.int32),
        scratch_shapes=[pltpu.VMEM((SUBLANES, LANES), F32)],
        compiler_params=_cparams(("arbitrary",)),
        name="moe_row_dest",
    )(idx, pstarts)


def _wait_rows(ref, rows, sem):
    view = ref.at[pl.ds(0, rows)]
    pltpu.make_async_copy(view, view, sem).wait()


def _dispatch_kernel(dest_ref, x_ref, xs_in, xs_hbm, sem):
    del xs_in
    tt = x_ref.shape[0] // ROW_TILE

    def issue(t, c):
        src = x_ref.at[pl.ds(pl.multiple_of(t * ROW_TILE, ROW_TILE), ROW_TILE), :]
        for kk in range(TOP_K):
            pltpu.make_async_copy(src, xs_hbm.at[dest_ref[t * TOP_K + kk]], sem).start()
        return c

    lax.fori_loop(0, tt, issue, 0)
    for _ in range(TOP_K):
        _wait_rows(xs_hbm, tt, sem)


def _dispatch(dest_flat, x1p2, n_rows, tt):
    n = x1p2.shape[0] // ROW_TILE
    zeros = jnp.zeros((n_rows, ROW_TILE, LANES), jnp.int32)
    return pl.pallas_call(
        _dispatch_kernel,
        grid=(n // tt,),
        in_specs=[pl.BlockSpec((tt * TOP_K,), lambda i: (i,), memory_space=pltpu.SMEM),
                  pl.BlockSpec((tt * ROW_TILE, LANES), lambda i: (i, 0)),
                  pl.BlockSpec(memory_space=pl.ANY)],
        out_specs=pl.BlockSpec(memory_space=pl.ANY),
        out_shape=jax.ShapeDtypeStruct((n_rows, ROW_TILE, LANES), jnp.int32),
        scratch_shapes=[pltpu.SemaphoreType.DMA],
        input_output_aliases={2: 0},
        compiler_params=_cparams(("arbitrary",)),
        name="moe_dispatch",
    )(dest_flat, x1p2, zeros)


def _dot_f32_weights(x, w_ref):
    acc = None
    for kc in range(x.shape[1] // MXU_DEPTH):
        sl = slice(kc * MXU_DEPTH, (kc + 1) * MXU_DEPTH)
        part = _dot(x[:, sl], w_ref[0, sl, :].astype(BF16))
        acc = part if acc is None else acc + part
    return acc


def _gu_kernel(be_ref, nv_ref, xs_ref, wg_ref, wu_ref, bg_ref, bu_ref, o_ref):
    j = pl.program_id(1)
    bm = o_ref.shape[0]

    @pl.when(j < nv_ref[0])
    def _():
        lo, hi = _unpack_row(_load_row_tiles(xs_ref, 0, bm))
        x = jnp.concatenate([lo, hi], axis=1).astype(BF16)
        g = _dot_f32_weights(x, wg_ref) + bg_ref[0]
        u = _dot_f32_weights(x, wu_ref) + bu_ref[0]
        g = jnp.minimum(g, SWIGLU_LIMIT)
        u = jnp.clip(u, -SWIGLU_LIMIT, SWIGLU_LIMIT)
        o_ref[...] = ((u + 1.0) * (g * _sigmoid(SWIGLU_ALPHA * g))).astype(o_ref.dtype)

    @pl.when(j >= nv_ref[0])
    def _():
        o_ref[...] = jnp.zeros_like(o_ref)


def _moe_gate_up(block_e, n_valid, xs2, w_gu, b_gu, bm, tn):
    n_rows = xs2.shape[0] // ROW_TILE
    n_blocks = n_rows // bm
    nt = D_FF // tn
    grid_spec = pltpu.PrefetchScalarGridSpec(
        num_scalar_prefetch=2,
        grid=(nt, n_blocks),
        in_specs=[pl.BlockSpec((bm * ROW_TILE, LANES), lambda n, j, be, nv: (j, 0)),
                  pl.BlockSpec((1, D_MODEL, tn), lambda n, j, be, nv: (be[j], 0, n)),
                  pl.BlockSpec((1, D_MODEL, tn), lambda n, j, be, nv: (be[j], 0, nt + n)),
                  pl.BlockSpec((1, 1, tn), lambda n, j, be, nv: (be[j], 0, n)),
                  pl.BlockSpec((1, 1, tn), lambda n, j, be, nv: (be[j], 0, nt + n))],
        out_specs=pl.BlockSpec((bm, tn), lambda n, j, be, nv: (j, n)))
    return pl.pallas_call(
        _gu_kernel,
        grid_spec=grid_spec,
        out_shape=jax.ShapeDtypeStruct((n_rows, D_FF), BF16),
        compiler_params=_cparams(("arbitrary", "arbitrary"), VMEM_LIMIT),
        name="moe_gate_up",
    )(block_e, n_valid, xs2, w_gu, w_gu, b_gu, b_gu)


def _dn_kernel(be_ref, nv_ref, a_ref, w_ref, b_ref, o_ref):
    j = pl.program_id(0)
    del be_ref

    @pl.when(j < nv_ref[0])
    def _():
        y = _dot_f32_weights(a_ref[...], w_ref) + b_ref[0]
        _store_row_tiles(o_ref, _pack_row(y))

    @pl.when(j >= nv_ref[0])
    def _():
        o_ref[...] = jnp.zeros_like(o_ref)


def _moe_down(block_e, n_valid, act, w_dn, b_dn, bm):
    n_rows = act.shape[0]
    n_blocks = n_rows // bm
    grid_spec = pltpu.PrefetchScalarGridSpec(
        num_scalar_prefetch=2,
        grid=(n_blocks,),
        in_specs=[pl.BlockSpec((bm, D_FF), lambda j, be, nv: (j, 0)),
                  pl.BlockSpec((1, D_FF, D_MODEL), lambda j, be, nv: (be[j], 0, 0)),
                  pl.BlockSpec((1, 1, D_MODEL), lambda j, be, nv: (be[j], 0, 0))],
        out_specs=pl.BlockSpec((bm * ROW_TILE, LANES), lambda j, be, nv: (j, 0)))
    return pl.pallas_call(
        _dn_kernel,
        grid_spec=grid_spec,
        out_shape=jax.ShapeDtypeStruct((n_rows * ROW_TILE, LANES), jnp.int32),
        compiler_params=_cparams(("arbitrary",), VMEM_LIMIT),
        name="moe_down",
    )(block_e, n_valid, act, w_dn, b_dn)


def _combine_kernel(dest_ref, next_ref, gate_ref, x1_ref, g_ref, b_ref, ys_hbm, o_ref, buf, sem):
    i = pl.program_id(0)
    tt = x1_ref.shape[0]
    slot = i % 2

    def fetch(d_ref, s):
        def body(t, c):
            for kk in range(TOP_K):
                row = pl.multiple_of((kk * tt + t) * ROW_TILE, ROW_TILE)
                pltpu.make_async_copy(ys_hbm.at[d_ref[t * TOP_K + kk]],
                                      buf.at[s, pl.ds(row, ROW_TILE), :], sem.at[s]).start()
            return c
        lax.fori_loop(0, tt, body, 0)

    @pl.when(i == 0)
    def _():
        fetch(dest_ref, 0)

    @pl.when(i + 1 < pl.num_programs(0))
    def _():
        fetch(next_ref, 1 - slot)

    for kk in range(TOP_K):
        view = buf.at[slot, pl.ds(kk * tt * ROW_TILE, tt * ROW_TILE), :]
        pltpu.make_async_copy(view, view, sem.at[slot]).wait()

    gates = gate_ref[...]
    cur = buf.at[slot]
    acc_lo = jnp.zeros((tt, ROW_WORDS), F32)
    acc_hi = jnp.zeros((tt, ROW_WORDS), F32)
    for kk in range(TOP_K):
        lo, hi = _unpack_row(_load_row_tiles(cur, kk * tt, tt))
        gk = gates[:, kk:kk + 1]
        acc_lo = acc_lo + gk * lo
        acc_hi = acc_hi + gk * hi
    ffn = jnp.concatenate([acc_lo, acc_hi], axis=1)
    o_ref[...] = _layer_norm(DEEPNORM_ALPHA * x1_ref[...] + ffn, g_ref[...], b_ref[...])


def _combine_ln(dest_flat, gates, x1, g, b, ys3, tt):
    n = x1.shape[0]
    last = n // tt - 1
    const = lambda shape: pl.BlockSpec(shape, lambda i: (0, 0))
    return pl.pallas_call(
        _combine_kernel,
        grid=(n // tt,),
        in_specs=[pl.BlockSpec((tt * TOP_K,), lambda i: (i,), memory_space=pltpu.SMEM),
                  pl.BlockSpec((tt * TOP_K,), lambda i: (jnp.minimum(i + 1, last),), memory_space=pltpu.SMEM),
                  pl.BlockSpec((tt, LANES), lambda i: (i, 0)),
                  pl.BlockSpec((tt, D_MODEL), lambda i: (i, 0)),
                  const((1, D_MODEL)), const((1, D_MODEL)),
                  pl.BlockSpec(memory_space=pl.ANY)],
        out_specs=pl.BlockSpec((tt, D_MODEL), lambda i: (i, 0)),
        out_shape=jax.ShapeDtypeStruct((n, D_MODEL), F32),
        scratch_shapes=[pltpu.VMEM((2, TOP_K * tt * ROW_TILE, LANES), jnp.int32),
                        pltpu.SemaphoreType.DMA((2,))],
        compiler_params=_cparams(("arbitrary",), VMEM_LIMIT),
        name="moe_combine_ln",
    )(dest_flat, dest_flat, gates, x1, g, b, ys3)


def _pad_cols(a, width):
    return jnp.pad(a, ((0, 0), (0, width - a.shape[1])))


def _lora_layout(a):
    dw = a[:, :DECAY_LORA]
    da = a[:, DECAY_LORA:DECAY_LORA + AAA_LORA]
    dg = a[:, DECAY_LORA + AAA_LORA:]
    return jnp.concatenate([_pad_cols(dw, LANES), _pad_cols(da, LANES), _pad_cols(dg, 2 * LANES)], axis=1)


def _moe_ffn(x1, x1p2, idx, gates, counts, w_gu, b_gu, w_dn, b_dn, ln_g, ln_b, bm, tn, tt):
    n = x1.shape[0]
    nk = n * TOP_K
    n_blocks = nk // bm + N_EXPERTS
    n_rows = n_blocks * bm
    cnt = counts[0, :N_EXPERTS].astype(jnp.int32)
    padded = ((cnt + bm - 1) // bm) * bm
    pends = jnp.cumsum(padded)
    pstarts = jnp.pad(pends - padded, (0, LANES - N_EXPERTS)).astype(F32).reshape(1, LANES)
    block_start = jnp.arange(n_blocks, dtype=jnp.int32) * bm
    block_e = jnp.minimum(jnp.sum(pends[None, :] <= block_start[:, None], axis=1), N_EXPERTS - 1).astype(jnp.int32)
    n_valid = (pends[-1:] // bm).astype(jnp.int32)

    dest_flat = _row_dest(idx, pstarts, tt)[:, :TOP_K].reshape(nk)
    xs3 = _dispatch(dest_flat, x1p2, n_rows, tt)
    act = _moe_gate_up(block_e, n_valid, xs3.reshape(n_rows * ROW_TILE, LANES), w_gu,
                       b_gu.reshape(N_EXPERTS, 1, 2 * D_FF), bm, tn)
    ys2 = _moe_down(block_e, n_valid, act, w_dn, b_dn.reshape(N_EXPERTS, 1, D_MODEL), bm)
    return _combine_ln(dest_flat, gates, x1, ln_g, ln_b, ys2.reshape(n_rows, ROW_TILE, LANES), tt)


def _layer(x, w_in, shift_mu, w0, w_up, a0, a_up, g_up, k_k, k_a, r_k, gn_g, gn_b,
           lq1, lk1, lq2, lk2, subln_g, w_out, ln1_g, ln1_b,
           w_router, b_router, w_gu, b_gu, w_dn, b_dn, ln2_g, ln2_b, lambda_init,
           tm_in=2048, tseq=512, ng=4, tq=512, tm_out=256, bm=512, tn=1024, tt=256):
    batch, seq, d = x.shape
    n = batch * seq
    rw = 3 * RWKV_WIDTH
    rcols = rw + DECAY_LORA + AAA_LORA + GATE_LORA
    row = lambda a: a.reshape(1, -1)

    xf = x.reshape(n, d)
    xb = xf.astype(BF16)
    w_r = jnp.concatenate([w_in[:, :rw], _lora_layout(w_in[:, rw:rcols])], axis=1).astype(BF16)
    w_d = w_in[:, rcols:].astype(BF16)
    mu = jnp.concatenate([row(shift_mu)[:, :rw], _lora_layout(row(shift_mu)[:, rw:])], axis=1)
    p_r = _matmul(xb, w_r, F32, tm_in, 512)
    p_d = _matmul(xb, w_d, BF16, tm_in, 512)

    pad_rows = lambda a, rows: jnp.pad(a, ((0, rows - a.shape[0]), (0, 0))).astype(BF16)
    h_r = _rwkv(p_r, mu, row(w0), row(a0), row(k_k), row(k_a), row(r_k), row(gn_g), row(gn_b),
                pad_rows(w_up, LANES), pad_rows(a_up, LANES), pad_rows(g_up, 2 * LANES), batch, seq, tseq, ng)
    h_d = _diff_attention(p_d, row(lq1), row(lk1), row(lq2), row(lk2), row(subln_g), lambda_init,
                          batch, seq, tq)

    w_ob = w_out.astype(BF16)
    w_rp = _pad_cols(w_router, LANES)
    b_rp = jnp.concatenate([row(b_router), jnp.full((1, LANES - N_EXPERTS), NEG_BIG, F32)], axis=1)
    x1, x1p2, idx, gates, counts = _outproj_ln_router(
        h_r, h_d, xf, w_ob[:RWKV_WIDTH], w_ob[RWKV_WIDTH:], row(ln1_g), row(ln1_b), w_rp, b_rp, tm_out)
    out = _moe_ffn(x1, x1p2, idx, gates, counts, w_gu, b_gu, w_dn, b_dn, row(ln2_g), row(ln2_b), bm, tn, tt)
    return out.reshape(batch, seq, d)


def kernel(x, w_in, shift_mu, w0, w_up, a0, a_up, g_up, k_k, k_a, r_k, gn_g, gn_b, lq1, lk1, lq2, lk2,
           subln_g, w_out, ln1_g, ln1_b, w_router, b_router, w_gu, b_gu, w_dn, b_dn, ln2_g, ln2_b):
    for l in range(DEPTH):
        lambda_init = 0.8 - 0.6 * math.exp(-0.3 * l)
        x = _layer(x, w_in[l], shift_mu[l], w0[l], w_up[l], a0[l], a_up[l], g_up[l], k_k[l], k_a[l],
                   r_k[l], gn_g[l], gn_b[l], lq1[l], lk1[l], lq2[l], lk2[l], subln_g[l], w_out[l],
                   ln1_g[l], ln1_b[l], w_router[l], b_router[l], w_gu[l], b_gu[l], w_dn[l], b_dn[l],
                   ln2_g[l], ln2_b[l], lambda_init)
    return x
```

```python
import functools
import math

import jax
import jax.numpy as jnp
from jax import lax
from jax.experimental import pallas as pl
from jax.experimental.pallas import tpu as pltpu

F32 = jnp.float32
BF16 = jnp.bfloat16

D_MODEL = 2048
RWKV_HEAD = 64
RWKV_WIDTH = 1024
RWKV_HEADS = 16
DECAY_LORA = 64
AAA_LORA = 64
GATE_LORA = 160
DIFF_HEAD = 64
DIFF_VDIM = 128
DIFF_HEADS = 8
DIFF_WIDTH = 1024
N_EXPERTS = 32
TOP_K = 4
D_FF = 2048
SWIGLU_LIMIT = 7.0
SWIGLU_ALPHA = 1.702
LN_EPS = 1e-5
GN_EPS = RWKV_HEAD * 1e-5
RMS_EPS = 1e-5
NEG_BIG = -1e30
DEPTH = 1
DEEPNORM_ALPHA = (2.0 * DEPTH) ** 0.25

LANES = 128
SUBLANES = 8
ROW_WORDS = D_MODEL // 2
ROW_TILE = ROW_WORDS // LANES
RWKV_GROUP = 256
RWKV_CHUNK = 64
LORA_COLS = 512
VMEM_LIMIT = 56 * 1024 * 1024
HIGH_HALF = -65536
MXU_DEPTH = 256
OUTPROJ_PARTS = 2


def _cparams(sem, vmem=None):
    return pltpu.CompilerParams(dimension_semantics=sem, vmem_limit_bytes=vmem)


def _dot(a, b):
    return jnp.dot(a, b, preferred_element_type=F32)


def _dot_nt(a, b):
    return lax.dot_general(a, b, (((1,), (1,)), ((), ())), preferred_element_type=F32)


def _dot_tn(a, b):
    return lax.dot_general(a, b, (((0,), (0,)), ((), ())), preferred_element_type=F32)


def _split3(x):
    h = x.astype(BF16)
    r = x - h.astype(F32)
    m = r.astype(BF16)
    l = (r - m.astype(F32)).astype(BF16)
    return h, m, l


def _dot_exact_rhs(x, ones):
    h, m, l = _split3(x)
    return _dot(h, ones) + _dot(m, ones) + _dot(l, ones)


def _dot_exact_lhs(ones, x):
    h, m, l = _split3(x)
    return _dot(ones, h) + _dot(ones, m) + _dot(ones, l)


def _sigmoid(x):
    return 1.0 / (1.0 + jnp.exp(-x))


def _matmul_kernel(x_ref, w_ref, o_ref):
    o_ref[...] = _dot(x_ref[...], w_ref[...]).astype(o_ref.dtype)


def _matmul(x, w, out_dtype, tm, tn):
    m, k = x.shape
    n = w.shape[1]
    return pl.pallas_call(
        _matmul_kernel,
        grid=(n // tn, m // tm),
        in_specs=[pl.BlockSpec((tm, k), lambda j, i: (i, 0)),
                  pl.BlockSpec((k, tn), lambda j, i: (0, j))],
        out_specs=pl.BlockSpec((tm, tn), lambda j, i: (i, j)),
        out_shape=jax.ShapeDtypeStruct((m, n), out_dtype),
        compiler_params=_cparams(("parallel", "parallel"), VMEM_LIMIT),
        name="in_proj",
    )(x, w)


def _rwkv_kernel(r_ref, k_ref, v_ref, l_ref, mur_ref, muk_ref, muv_ref, mul_ref,
                 w0_ref, a0_ref, kk_ref, ka_ref, rk_ref, gng_ref, gnb_ref,
                 wup_ref, aup_ref, gup_ref, o_ref,
                 pr_s, pk_s, pv_s, pl_s, state_s, r_s, w_s, k_s, v_s, a_s, b_s, g_s):
    s = pl.program_id(2)
    T = r_ref.shape[0]
    G = RWKV_GROUP
    C = RWKV_CHUNK
    NG = r_ref.shape[1] // G
    groups = [slice(g * G, (g + 1) * G) for g in range(NG)]

    @pl.when(s == 0)
    def _():
        state_s[...] = jnp.zeros_like(state_s)
        pr_s[...] = jnp.zeros_like(pr_s)
        pk_s[...] = jnp.zeros_like(pk_s)
        pv_s[...] = jnp.zeros_like(pv_s)
        pl_s[...] = jnp.zeros_like(pl_s)

    row = lax.broadcasted_iota(jnp.int32, (T, 1), 0)

    def shift(ref, prev_s, mu_ref):
        p = ref[...]
        prev = jnp.where(row == 0, prev_s[...], pltpu.roll(p, 1, 0))
        prev_s[...] = p[T - 1:T, :]
        return p + (prev - p) * mu_ref[...]

    r = shift(r_ref, pr_s, mur_ref)
    k = shift(k_ref, pk_s, muk_ref)
    v = shift(v_ref, pv_s, muv_ref)
    lo = shift(l_ref, pl_s, mul_ref)
    dw = lo[:, 0:LANES]
    da = lo[:, LANES:2 * LANES]
    dg = lo[:, 2 * LANES:4 * LANES]

    wpre = w0_ref[...] + _dot(jnp.tanh(dw).astype(BF16), wup_ref[...])
    sp = jnp.maximum(-wpre, 0.0) + jnp.log(1.0 + jnp.exp(-jnp.abs(wpre)))
    wlog = -jnp.exp(-sp - 0.5)
    a_sig = _sigmoid(a0_ref[...] + _dot(da.astype(BF16), aup_ref[...]))
    gate = _dot(_sigmoid(dg).astype(BF16), gup_ref[...])

    gi = lax.broadcasted_iota(jnp.int32, (G, G), 0)
    gj = lax.broadcasted_iota(jnp.int32, (G, G), 1)
    same_head = (gi // RWKV_HEAD) == (gj // RWKV_HEAD)
    head_ones = jnp.where(same_head, 1.0, 0.0).astype(BF16)

    def head_sum(x):
        return jnp.concatenate([_dot_exact_rhs(x[:, g], head_ones) for g in groups], axis=1)

    kk = k * kk_ref[...]
    nrm = jnp.sqrt(head_sum(kk * kk))
    kk = kk / jnp.maximum(nrm, 1e-12)
    k2 = k * (1.0 + (a_sig - 1.0) * ka_ref[...])

    r_s[...] = r
    w_s[...] = wlog
    k_s[...] = k2
    v_s[...] = v
    a_s[...] = -kk
    b_s[...] = kk * a_sig
    g_s[...] = gate

    ci = lax.broadcasted_iota(jnp.int32, (C, C), 0)
    cj = lax.broadcasted_iota(jnp.int32, (C, C), 1)
    tri = jnp.where(cj <= ci, 1.0, 0.0).astype(BF16)
    mt = lax.broadcasted_iota(jnp.int32, (C, G), 0)
    mtp = lax.broadcasted_iota(jnp.int32, (C, G), 1) % C
    strict = mtp < mt
    incl = mtp <= mt
    ceye = jnp.where(mtp == mt, 1.0, 0.0)

    def bd(xc):
        return jnp.where(same_head, jnp.concatenate([xc, xc, xc, xc], axis=0), jnp.zeros((), BF16))

    def each(fn, *lists):
        return [fn(*xs) for xs in zip(*lists)]

    def bf(x):
        return x.astype(BF16)

    def chunk(c, carry):
        sl = pl.ds(pl.multiple_of(c * C, C), C)
        rc = [r_s[sl, g] for g in groups]
        wc = [w_s[sl, g] for g in groups]
        kc = [k_s[sl, g] for g in groups]
        vc = [v_s[sl, g] for g in groups]
        ac = [a_s[sl, g] for g in groups]
        bc = [b_s[sl, g] for g in groups]
        cum = each(lambda w: _dot_exact_lhs(tri, w), wc)
        tot = each(lambda x: x[C - 1:C, :], cum)
        ginv = each(lambda x: jnp.exp(-x), cum)
        gend = each(lambda x, t: jnp.exp(t - x), cum, tot)
        r_c = each(lambda x, g: bf(x * jnp.exp(g)), rc, cum)
        a_c = each(lambda x, g, w: bf(x * jnp.exp(g - w)), ac, cum, wc)
        v_c = each(bf, vc)
        k_bd = each(lambda x, g: bd(bf(x * g)), kc, ginv)
        b_bd = each(lambda x, g: bd(bf(x * g)), bc, ginv)
        ke_c = each(lambda x, g: bf(x * g), kc, gend)
        be_c = each(lambda x, g: bf(x * g), bc, gend)
        v_bd = each(bd, v_c)
        a_bd = each(bd, a_c)

        ar = each(lambda a, r: jnp.concatenate([a, r], axis=0), a_c, r_c)
        arb = each(_dot_nt, ar, b_bd)
        ark = each(_dot_nt, ar, k_bd)
        l_ab = each(lambda x: jnp.where(strict, x[:C], 0.0), arb)
        l_ak = each(lambda x: bf(jnp.where(strict, x[:C], 0.0)), ark)
        l_rb = each(lambda x: bf(jnp.where(incl, x[C:], 0.0)), arb)
        l_rk = each(lambda x: bf(jnp.where(incl, x[C:], 0.0)), ark)

        p = each(bf, l_ab)
        p_bd = each(bd, p)
        tinv = each(lambda x: ceye + x, l_ab)
        for _ in range(int(math.log2(C)) - 1):
            p = each(lambda x, y: bf(_dot(x, y)), p, p_bd)
            p_bd = each(bd, p)
            tinv = each(lambda t, y: t + _dot(bf(t), y), tinv, p_bd)
        tb = each(bf, tinv)

        akv_bd = each(lambda a, b: bd(bf(_dot(a, b))), l_ak, v_bd)
        w_c = each(lambda a, b: bf(_dot(a, b)), tb, a_bd)
        st = [state_s[g] for g in range(NG)]
        stb = each(bf, st)
        u_c = each(lambda w, s0, t, x: bf(_dot_nt(w, s0) + _dot(t, x)), w_c, stb, tb, akv_bd)
        u_bd = each(bd, u_c)
        y = each(lambda r, s0, lb, lk, u, vv: _dot_nt(r, s0) + _dot(jnp.concatenate([lb, lk], axis=1),
                                                                    jnp.concatenate([u, vv], axis=0)),
                 r_c, stb, l_rb, l_rk, u_bd, v_bd)
        new = each(lambda s0, t, u, vv, b, kx: s0 * jnp.exp(t) + jnp.where(
            same_head, _dot_tn(jnp.concatenate([u, vv], axis=0), jnp.concatenate([b, kx], axis=0)), 0.0),
            st, tot, u_c, v_c, be_c, ke_c)
        for g in range(NG):
            state_s[g] = new[g]

        mean = each(lambda x: _dot_exact_rhs(x, head_ones) * (1.0 / RWKV_HEAD), y)
        d = each(lambda x, m: x - m, y, mean)
        var = each(lambda x: _dot_exact_rhs(x * x, head_ones) * (1.0 / RWKV_HEAD), d)
        bonus = each(lambda r, kx, vv, g: _dot_exact_rhs(r * kx * rk_ref[:, g], head_ones) * vv,
                     rc, kc, vc, groups)
        for g, dd, vr, bo in zip(groups, d, var, bonus):
            yn = dd * lax.rsqrt(vr + GN_EPS) * gng_ref[:, g] + gnb_ref[:, g]
            o_ref[sl, g] = ((yn + bo) * g_s[sl, g]).astype(o_ref.dtype)
        return carry

    lax.fori_loop(0, T // C, chunk, 0)


def _rwkv(p_r, mu, w0, a0, k_k, k_a, r_k, gn_g, gn_b, w_up, a_up, g_up, batch, seq, tseq, ng):
    n = batch * seq
    G = ng * RWKV_GROUP
    nq = RWKV_WIDTH // G
    ns = seq // tseq
    lora_blk = 3 * RWKV_WIDTH // LORA_COLS

    def tok(off):
        return pl.BlockSpec((tseq, G), lambda b, q, s: (b * ns + s, off + q))

    def par(off):
        return pl.BlockSpec((1, G), lambda b, q, s: (0, off + q))

    in_specs = [
        tok(0), tok(nq), tok(2 * nq),
        pl.BlockSpec((tseq, LORA_COLS), lambda b, q, s: (b * ns + s, lora_blk)),
        par(0), par(nq), par(2 * nq),
        pl.BlockSpec((1, LORA_COLS), lambda b, q, s: (0, lora_blk)),
        par(0), par(0), par(0), par(0), par(0), par(0), par(0),
        pl.BlockSpec((LANES, G), lambda b, q, s: (0, q)),
        pl.BlockSpec((LANES, G), lambda b, q, s: (0, q)),
        pl.BlockSpec((2 * LANES, G), lambda b, q, s: (0, q)),
    ]
    scratch = [pltpu.VMEM((1, G), F32), pltpu.VMEM((1, G), F32), pltpu.VMEM((1, G), F32),
               pltpu.VMEM((1, LORA_COLS), F32), pltpu.VMEM((ng, RWKV_GROUP, RWKV_GROUP), F32)]
    scratch += [pltpu.VMEM((tseq, G), F32) for _ in range(7)]
    return pl.pallas_call(
        _rwkv_kernel,
        grid=(batch, nq, ns),
        in_specs=in_specs,
        out_specs=pl.BlockSpec((tseq, G), lambda b, q, s: (b * ns + s, q)),
        out_shape=jax.ShapeDtypeStruct((n, RWKV_WIDTH), BF16),
        scratch_shapes=scratch,
        compiler_params=_cparams(("parallel", "parallel", "arbitrary"), VMEM_LIMIT),
        name="rwkv7",
    )(p_r, p_r, p_r, p_r, mu, mu, mu, mu, w0, a0, k_k, k_a, r_k, gn_g, gn_b, w_up, a_up, g_up)


def _attn_kernel(q_ref, k_ref, v_ref, lq1_ref, lk1_ref, lq2_ref, lk2_ref, g_ref, o_ref, vt_s, *, lambda_init):
    i = pl.program_id(2)
    tq = q_ref.shape[0]

    @pl.when(i == 0)
    def _():
        for j in range(vt_s.shape[0]):
            vt_s[j] = v_ref[j * tq:(j + 1) * tq, :].astype(F32).T.astype(BF16)

    drow = lax.broadcasted_iota(jnp.int32, (DIFF_VDIM, 1), 0)
    qt = (q_ref[...].astype(F32) * (DIFF_HEAD ** -0.5)).T
    q1 = jnp.where(drow < DIFF_HEAD, qt, 0.0).astype(BF16)
    q2 = jnp.where(drow >= DIFF_HEAD, qt, 0.0).astype(BF16)
    lam = (jnp.exp(jnp.sum(lq1_ref[...] * lk1_ref[...], axis=-1, keepdims=True))
           - jnp.exp(jnp.sum(lq2_ref[...] * lk2_ref[...], axis=-1, keepdims=True)) + lambda_init)

    key = lax.broadcasted_iota(jnp.int32, (tq, tq), 0)
    qry = lax.broadcasted_iota(jnp.int32, (tq, tq), 1)

    def scores(j, diagonal):
        kj = k_ref[j * tq:(j + 1) * tq, :]
        sc = (_dot(kj, q1), _dot(kj, q2))
        if diagonal:
            sc = tuple(jnp.where(key <= qry, s, NEG_BIG) for s in sc)
        return sc

    def absorb(j, sc, stats):
        m, l, acc = stats[0:2], stats[2:4], stats[4:6]
        vtj = vt_s[j]
        m_new = [jnp.maximum(mm, jnp.max(s, axis=0, keepdims=True)) for mm, s in zip(m, sc)]
        alpha = [jnp.exp(mm - mn) for mm, mn in zip(m, m_new)]
        p = [jnp.exp(s - mn) for s, mn in zip(sc, m_new)]
        l = [a * ll + jnp.sum(pp, axis=0, keepdims=True) for a, ll, pp in zip(alpha, l, p)]
        pv = [_dot(vtj, pp.astype(BF16)) for pp in p]
        acc = [a * ac + x for a, ac, x in zip(alpha, acc, pv)]
        return tuple(m_new) + tuple(l) + tuple(acc)

    zero1 = jnp.zeros((1, tq), F32)
    neg1 = jnp.full((1, tq), NEG_BIG, F32)
    zacc = jnp.zeros((DIFF_VDIM, tq), F32)
    init = (neg1, neg1, zero1, zero1, zacc, zacc)

    for iv in range(vt_s.shape[0]):
        @pl.when(i == iv)
        def _(iv=iv):
            stats = init
            sc = scores(0, iv == 0)
            for j in range(iv + 1):
                nxt = scores(j + 1, j + 1 == iv) if j < iv else None
                stats = absorb(j, sc, stats)
                sc = nxt
            l, acc = stats[2:4], stats[4:6]
            o = acc[0] / l[0] - lam * (acc[1] / l[1])
            o = o * lax.rsqrt(jnp.mean(o * o, axis=0, keepdims=True) + RMS_EPS) * g_ref[...]
            o_ref[...] = (o * (1.0 - lambda_init)).T.astype(o_ref.dtype)


def _diff_attention(p_d, lq1, lk1, lq2, lk2, subln_g, lambda_init, batch, seq, tq):
    n = batch * seq
    nq = seq // tq
    H = DIFF_HEADS
    small = pl.BlockSpec((1, DIFF_HEAD), lambda b, h, i: (0, 0))
    return pl.pallas_call(
        functools.partial(_attn_kernel, lambda_init=lambda_init),
        grid=(batch, H, nq),
        in_specs=[pl.BlockSpec((tq, DIFF_VDIM), lambda b, h, i: (b * nq + i, h)),
                  pl.BlockSpec((seq, DIFF_VDIM), lambda b, h, i: (b, H + h)),
                  pl.BlockSpec((seq, DIFF_VDIM), lambda b, h, i: (b, 2 * H + h)),
                  small, small, small, small,
                  pl.BlockSpec((DIFF_VDIM, 1), lambda b, h, i: (0, 0))],
        out_specs=pl.BlockSpec((tq, DIFF_VDIM), lambda b, h, i: (b * nq + i, h)),
        out_shape=jax.ShapeDtypeStruct((n, DIFF_WIDTH), BF16),
        scratch_shapes=[pltpu.VMEM((nq, DIFF_VDIM, tq), BF16)],
        compiler_params=_cparams(("parallel", "parallel", "arbitrary"), VMEM_LIMIT),
        name="diff_attn",
    )(p_d, p_d, p_d, lq1, lk1, lq2, lk2, subln_g.reshape(DIFF_VDIM, 1))


def _layer_norm(y, g, b):
    mu = jnp.mean(y, axis=-1, keepdims=True)
    d = y - mu
    var = jnp.mean(d * d, axis=-1, keepdims=True)
    return d * lax.rsqrt(var + LN_EPS) * g + b


def _store_row_tiles(ref, words, start=0, sublane=0):
    rows = words.shape[0]
    for s in range(words.shape[1] // LANES):
        ref[pl.ds(start * ROW_TILE + sublane + s, rows, stride=ROW_TILE), :] = words[:, s * LANES:(s + 1) * LANES]


def _load_row_tiles(ref, start, rows):
    parts = [ref[pl.ds(start * ROW_TILE + s, rows, stride=ROW_TILE), :] for s in range(ROW_TILE)]
    return jnp.concatenate(parts, axis=1)


def _pack_pair(lo, hi):
    bits = lambda y: lax.bitcast_convert_type(y.astype(BF16).astype(F32), jnp.int32)
    return lax.shift_right_logical(bits(lo), 16) | (bits(hi) & HIGH_HALF)


def _pack_row(y):
    return _pack_pair(y[:, :ROW_WORDS], y[:, ROW_WORDS:])


def _unpack_row(words):
    lo = lax.bitcast_convert_type(lax.shift_left(words, 16), F32)
    hi = lax.bitcast_convert_type(words & HIGH_HALF, F32)
    return lo, hi


def _expert_onehots(idx):
    lane = lax.broadcasted_iota(jnp.int32, idx.shape, 1)
    sels = [lane == idx[:, kk:kk + 1] for kk in range(TOP_K)]
    onehot = jnp.zeros(idx.shape, F32)
    for sel in sels:
        onehot = onehot + jnp.where(sel, 1.0, 0.0)
    return sels, onehot


def _outproj_kernel(hr_ref, hd_ref, x_ref, wt_ref, wb_ref, g_ref, b_ref, wrh_ref, wrl_ref, br_ref,
                    x1_ref, x1p_ref, idx_ref, gate_ref, cnt_ref, carry_s):
    i = pl.program_id(0)
    tm = x_ref.shape[0]
    th = tm // OUTPROJ_PARTS
    parts = [pl.ds(h * th, th) for h in range(OUTPROJ_PARTS)]

    @pl.when(i == 0)
    def _():
        carry_s[...] = jnp.zeros_like(carry_s)

    def each(fn, *lists):
        return [fn(*xs) for xs in zip(*lists)]

    mix = [_dot(hr_ref[p, :], wt_ref[...]) + _dot(hd_ref[p, :], wb_ref[...]) for p in parts]
    x1 = [_layer_norm(DEEPNORM_ALPHA * x_ref[p, :] + m, g_ref[...], b_ref[...]) for p, m in zip(parts, mix)]
    for h, (p, y) in enumerate(zip(parts, x1)):
        x1_ref[p, :] = y
        _store_row_tiles(x1p_ref, _pack_row(y), h * th)

    xh = each(lambda y: y.astype(BF16), x1)
    xl = each(lambda y, hh: (y - hh.astype(F32)).astype(BF16), x1, xh)
    work = each(lambda hh, ll: _dot(hh, wrh_ref[...]) + _dot(ll, wrh_ref[...]) + _dot(hh, wrl_ref[...])
                + br_ref[...], xh, xl)
    lane = lax.broadcasted_iota(jnp.int32, (th, LANES), 1).astype(F32)
    onehot = [jnp.zeros((th, LANES), F32) for _ in parts]
    vals, idxs = [], []
    for _ in range(TOP_K):
        mx = each(lambda w: jnp.max(w, axis=-1, keepdims=True), work)
        idx = each(lambda w, m: jnp.min(jnp.where(w == m, lane, float(LANES)), axis=-1, keepdims=True), work, mx)
        sel = each(lambda ix: lane == ix, idx)
        work = each(lambda s, w: jnp.where(s, -jnp.inf, w), sel, work)
        onehot = each(lambda o, s: o + jnp.where(s, 1.0, 0.0), onehot, sel)
        vals.append(mx)
        idxs.append(idx)
    total = carry_s[0:1, :]
    for h, p in enumerate(parts):
        exps = [jnp.exp(vv[h] - vals[0][h]) for vv in vals]
        den = exps[0] + exps[1] + exps[2] + exps[3]
        idx_out = jnp.zeros((th, LANES), F32)
        gate_out = jnp.zeros((th, LANES), F32)
        for kk in range(TOP_K):
            slot = lane == float(kk)
            idx_out = jnp.where(slot, idxs[kk][h], idx_out)
            gate_out = jnp.where(slot, exps[kk] / den, gate_out)
        idx_ref[p, :] = idx_out.astype(jnp.int32)
        gate_ref[p, :] = gate_out
        total = total + jnp.sum(onehot[h], axis=0, keepdims=True)
    carry_s[...] = jnp.broadcast_to(total, carry_s.shape)
    cnt_ref[...] = jnp.broadcast_to(total, cnt_ref.shape)


def _outproj_ln_router(hr, hd, x, w_top, w_bot, g, b, w_r, b_r, tm):
    n = x.shape[0]
    w_rh = w_r.astype(BF16)
    w_rl = (w_r - w_rh.astype(F32)).astype(BF16)
    const = lambda shape: pl.BlockSpec(shape, lambda i: (0, 0))
    rowb = lambda cols: pl.BlockSpec((tm, cols), lambda i: (i, 0))
    out_shape = (jax.ShapeDtypeStruct((n, D_MODEL), F32),
                 jax.ShapeDtypeStruct((n * ROW_TILE, LANES), jnp.int32),
                 jax.ShapeDtypeStruct((n, LANES), jnp.int32),
                 jax.ShapeDtypeStruct((n, LANES), F32),
                 jax.ShapeDtypeStruct((SUBLANES, LANES), F32))
    return pl.pallas_call(
        _outproj_kernel,
        grid=(n // tm,),
        in_specs=[rowb(RWKV_WIDTH), rowb(DIFF_WIDTH), rowb(D_MODEL),
                  const((RWKV_WIDTH, D_MODEL)), const((DIFF_WIDTH, D_MODEL)),
                  const((1, D_MODEL)), const((1, D_MODEL)),
                  const((D_MODEL, LANES)), const((D_MODEL, LANES)), const((1, LANES))],
        out_specs=(rowb(D_MODEL), pl.BlockSpec((tm * ROW_TILE, LANES), lambda i: (i, 0)),
                   rowb(LANES), rowb(LANES), const((SUBLANES, LANES))),
        out_shape=out_shape,
        scratch_shapes=[pltpu.VMEM((SUBLANES, LANES), F32)],
        compiler_params=_cparams(("arbitrary",), VMEM_LIMIT),
        name="outproj_ln_router",
    )(hr, hd, x, w_top, w_bot, g, b, w_rh, w_rl, b_r)


def _dest_kernel(idx_ref, ps_ref, dest_ref, carry_s):
    i = pl.program_id(0)
    tm = idx_ref.shape[0]

    @pl.when(i == 0)
    def _():
        carry_s[...] = jnp.broadcast_to(ps_ref[...], carry_s.shape)

    sels, onehot = _expert_onehots(idx_ref[...])
    ti = lax.broadcasted_iota(jnp.int32, (tm, tm), 0)
    tj = lax.broadcasted_iota(jnp.int32, (tm, tm), 1)
    before = jnp.where(tj < ti, 1.0, 0.0).astype(BF16)
    cum = _dot(before, onehot.astype(BF16)) + carry_s[0:1, :]
    lane = lax.broadcasted_iota(jnp.int32, (tm, LANES), 1)
    dest = jnp.zeros((tm, LANES), F32)
    for kk in range(TOP_K):
        dk = jnp.sum(jnp.where(sels[kk], cum, 0.0), axis=-1, keepdims=True)
        dest = jnp.where(lane == kk, dk, dest)
    dest_ref[...] = dest.astype(jnp.int32)
    total = carry_s[0:1, :] + jnp.sum(onehot, axis=0, keepdims=True)
    carry_s[...] = jnp.broadcast_to(total, carry_s.shape)


def _row_dest(idx, pstarts, tm):
    n = idx.shape[0]
    return pl.pallas_call(
        _dest_kernel,
        grid=(n // tm,),
        in_specs=[pl.BlockSpec((tm, LANES), lambda i: (i, 0)), pl.BlockSpec((1, LANES), lambda i: (0, 0))],
        out_specs=pl.BlockSpec((tm, LANES), lambda i: (i, 0)),
        out_shape=jax.ShapeDtypeStruct((n, LANES), jnp.int32),
        scratch_shapes=[pltpu.VMEM((SUBLANES, LANES), F32)],
        compiler_params=_cparams(("arbitrary",)),
        name="moe_row_dest",
    )(idx, pstarts)


def _wait_rows(ref, rows, sem):
    view = ref.at[pl.ds(0, rows)]
    pltpu.make_async_copy(view, view, sem).wait()


def _dispatch_kernel(dest_ref, padoff_ref, padlen_ref, x_ref, xs_hbm, zero_s, sem, zsem):
    tt = x_ref.shape[0] // ROW_TILE

    def pad_copies(e, go):
        off = padoff_ref[e]
        ln = padlen_ref[e]
        for bit in range(zero_s.shape[0].bit_length()):
            size = 1 << bit

            @pl.when((ln >> bit) & 1 == 1)
            def _():
                go(pltpu.make_async_copy(zero_s.at[pl.ds(0, size)],
                                         xs_hbm.at[pl.ds(off + (ln & (size - 1)), size)], zsem))

    def tail_copy(piece):
        rows = zero_s.shape[0]
        first = pl.multiple_of(padoff_ref[N_EXPERTS] + piece * rows, rows)
        return pltpu.make_async_copy(zero_s, xs_hbm.at[pl.ds(first, rows)], zsem)

    @pl.when(pl.program_id(0) == 0)
    def _():
        zero_s[...] = jnp.zeros_like(zero_s)
        for go in (lambda cp: cp.start(), lambda cp: cp.wait()):
            def body(e, c, go=go):
                pad_copies(e, go)
                return c
            lax.fori_loop(0, N_EXPERTS, body, 0)

            def tail(piece, c, go=go):
                go(tail_copy(piece))
                return c
            lax.fori_loop(0, padlen_ref[N_EXPERTS], tail, 0)

    def issue(t, c):
        src = x_ref.at[pl.ds(pl.multiple_of(t * ROW_TILE, ROW_TILE), ROW_TILE), :]
        for kk in range(TOP_K):
            pltpu.make_async_copy(src, xs_hbm.at[dest_ref[t * TOP_K + kk]], sem).start()
        return c

    lax.fori_loop(0, tt, issue, 0)
    for _ in range(TOP_K):
        _wait_rows(xs_hbm, tt, sem)


def _dispatch(dest_flat, pad_off, pad_len, x1p2, n_rows, tt, bm):
    n = x1p2.shape[0] // ROW_TILE
    table = pl.BlockSpec((LANES,), lambda i: (0,), memory_space=pltpu.SMEM)
    return pl.pallas_call(
        _dispatch_kernel,
        grid=(n // tt,),
        in_specs=[pl.BlockSpec((tt * TOP_K,), lambda i: (i,), memory_space=pltpu.SMEM), table, table,
                  pl.BlockSpec((tt * ROW_TILE, LANES), lambda i: (i, 0))],
        out_specs=pl.BlockSpec(memory_space=pl.ANY),
        out_shape=jax.ShapeDtypeStruct((n_rows, ROW_TILE, LANES), jnp.int32),
        scratch_shapes=[pltpu.VMEM((bm // 2, ROW_TILE, LANES), jnp.int32),
                        pltpu.SemaphoreType.DMA, pltpu.SemaphoreType.DMA],
        compiler_params=_cparams(("arbitrary",)),
        name="moe_dispatch",
    )(dest_flat, pad_off, pad_len, x1p2)


def _dot_f32_weights(xk, w_ref, cols):
    acc = None
    for kc, x in enumerate(xk):
        part = _dot(x, w_ref[0, kc * MXU_DEPTH:(kc + 1) * MXU_DEPTH, cols].astype(BF16))
        acc = part if acc is None else acc + part
    return acc


def _gu_kernel(be_ref, nv_ref, xs_ref, wg_ref, wu_ref, bg_ref, bu_ref, o_ref):
    j = pl.program_id(1)
    bm, tn = o_ref.shape
    half_k = D_MODEL // MXU_DEPTH // 2

    @pl.when(j < nv_ref[0])
    def _():
        def x_slab(kc):
            s0 = (kc % half_k) * (MXU_DEPTH // LANES)
            w = jnp.concatenate([xs_ref[pl.ds(s0 + s, bm, stride=ROW_TILE), :]
                                 for s in range(MXU_DEPTH // LANES)], axis=1)
            lo, hi = _unpack_row(w)
            return (lo if kc < half_k else hi).astype(BF16)

        xk = [x_slab(kc) for kc in range(2 * half_k)]
        for cols in (slice(0, tn // 2), slice(tn // 2, tn)):
            g = _dot_f32_weights(xk, wg_ref, cols) + bg_ref[0, :, cols]
            u = _dot_f32_weights(xk, wu_ref, cols) + bu_ref[0, :, cols]
            g = jnp.minimum(g, SWIGLU_LIMIT)
            u = jnp.clip(u, -SWIGLU_LIMIT, SWIGLU_LIMIT)
            o_ref[:, cols] = ((u + 1.0) * (g * _sigmoid(SWIGLU_ALPHA * g))).astype(o_ref.dtype)

    @pl.when(j >= nv_ref[0])
    def _():
        o_ref[...] = jnp.zeros_like(o_ref)


def _moe_gate_up(block_e, n_valid, xs2, w_gu, b_gu, bm, tn):
    n_rows = xs2.shape[0] // ROW_TILE
    n_blocks = n_rows // bm
    nt = D_FF // tn
    grid_spec = pltpu.PrefetchScalarGridSpec(
        num_scalar_prefetch=2,
        grid=(nt, n_blocks),
        in_specs=[pl.BlockSpec((bm * ROW_TILE, LANES), lambda n, j, be, nv: (j, 0)),
                  pl.BlockSpec((1, D_MODEL, tn), lambda n, j, be, nv: (be[j], 0, n)),
                  pl.BlockSpec((1, D_MODEL, tn), lambda n, j, be, nv: (be[j], 0, nt + n)),
                  pl.BlockSpec((1, 1, tn), lambda n, j, be, nv: (be[j], 0, n)),
                  pl.BlockSpec((1, 1, tn), lambda n, j, be, nv: (be[j], 0, nt + n))],
        out_specs=pl.BlockSpec((bm, tn), lambda n, j, be, nv: (j, n)))
    return pl.pallas_call(
        _gu_kernel,
        grid_spec=grid_spec,
        out_shape=jax.ShapeDtypeStruct((n_rows, D_FF), BF16),
        compiler_params=_cparams(("arbitrary", "arbitrary"), VMEM_LIMIT),
        name="moe_gate_up",
    )(block_e, n_valid, xs2, w_gu, w_gu, b_gu, b_gu)


def _dn_kernel(be_ref, nv_ref, a_ref, w_ref, b_ref, o_ref):
    j = pl.program_id(0)
    del be_ref

    @pl.when(j < nv_ref[0])
    def _():
        ak = [a_ref[:, kc * MXU_DEPTH:(kc + 1) * MXU_DEPTH] for kc in range(D_FF // MXU_DEPTH)]
        hw = ROW_WORDS // 2
        for h in range(2):
            lo = slice(h * hw, (h + 1) * hw)
            hi = slice(ROW_WORDS + h * hw, ROW_WORDS + (h + 1) * hw)
            y_lo = _dot_f32_weights(ak, w_ref, lo) + b_ref[0, :, lo]
            y_hi = _dot_f32_weights(ak, w_ref, hi) + b_ref[0, :, hi]
            _store_row_tiles(o_ref, _pack_pair(y_lo, y_hi), 0, h * (hw // LANES))

    @pl.when(j >= nv_ref[0])
    def _():
        o_ref[...] = jnp.zeros_like(o_ref)


def _moe_down(block_e, n_valid, act, w_dn, b_dn, bm):
    n_rows = act.shape[0]
    n_blocks = n_rows // bm
    grid_spec = pltpu.PrefetchScalarGridSpec(
        num_scalar_prefetch=2,
        grid=(n_blocks,),
        in_specs=[pl.BlockSpec((bm, D_FF), lambda j, be, nv: (j, 0)),
                  pl.BlockSpec((1, D_FF, D_MODEL), lambda j, be, nv: (be[j], 0, 0)),
                  pl.BlockSpec((1, 1, D_MODEL), lambda j, be, nv: (be[j], 0, 0))],
        out_specs=pl.BlockSpec((bm * ROW_TILE, LANES), lambda j, be, nv: (j, 0)))
    return pl.pallas_call(
        _dn_kernel,
        grid_spec=grid_spec,
        out_shape=jax.ShapeDtypeStruct((n_rows * ROW_TILE, LANES), jnp.int32),
        compiler_params=_cparams(("arbitrary",), VMEM_LIMIT),
        name="moe_down",
    )(block_e, n_valid, act, w_dn, b_dn)


def _combine_kernel(dest_ref, next_ref, gate_ref, x1_ref, g_ref, b_ref, ys_hbm, o_ref, buf, sem):
    i = pl.program_id(0)
    tt = x1_ref.shape[0]
    slot = i % 2

    def fetch(d_ref, s):
        def body(t, c):
            for kk in range(TOP_K):
                row = pl.multiple_of((kk * tt + t) * ROW_TILE, ROW_TILE)
                pltpu.make_async_copy(ys_hbm.at[d_ref[t * TOP_K + kk]],
                                      buf.at[s, pl.ds(row, ROW_TILE), :], sem.at[s]).start()
            return c
        lax.fori_loop(0, tt, body, 0)

    @pl.when(i == 0)
    def _():
        fetch(dest_ref, 0)

    @pl.when(i + 1 < pl.num_programs(0))
    def _():
        fetch(next_ref, 1 - slot)

    for kk in range(TOP_K):
        view = buf.at[slot, pl.ds(kk * tt * ROW_TILE, tt * ROW_TILE), :]
        pltpu.make_async_copy(view, view, sem.at[slot]).wait()

    gates = gate_ref[...]
    cur = buf.at[slot]
    acc_lo = jnp.zeros((tt, ROW_WORDS), F32)
    acc_hi = jnp.zeros((tt, ROW_WORDS), F32)
    for kk in range(TOP_K):
        lo, hi = _unpack_row(_load_row_tiles(cur, kk * tt, tt))
        gk = gates[:, kk:kk + 1]
        acc_lo = acc_lo + gk * lo
        acc_hi = acc_hi + gk * hi
    ffn = jnp.concatenate([acc_lo, acc_hi], axis=1)
    o_ref[...] = _layer_norm(DEEPNORM_ALPHA * x1_ref[...] + ffn, g_ref[...], b_ref[...])


def _combine_ln(dest_flat, gates, x1, g, b, ys3, tt):
    n = x1.shape[0]
    last = n // tt - 1
    const = lambda shape: pl.BlockSpec(shape, lambda i: (0, 0))
    return pl.pallas_call(
        _combine_kernel,
        grid=(n // tt,),
        in_specs=[pl.BlockSpec((tt * TOP_K,), lambda i: (i,), memory_space=pltpu.SMEM),
                  pl.BlockSpec((tt * TOP_K,), lambda i: (jnp.minimum(i + 1, last),), memory_space=pltpu.SMEM),
                  pl.BlockSpec((tt, LANES), lambda i: (i, 0)),
                  pl.BlockSpec((tt, D_MODEL), lambda i: (i, 0)),
                  const((1, D_MODEL)), const((1, D_MODEL)),
                  pl.BlockSpec(memory_space=pl.ANY)],
        out_specs=pl.BlockSpec((tt, D_MODEL), lambda i: (i, 0)),
        out_shape=jax.ShapeDtypeStruct((n, D_MODEL), F32),
        scratch_shapes=[pltpu.VMEM((2, TOP_K * tt * ROW_TILE, LANES), jnp.int32),
                        pltpu.SemaphoreType.DMA((2,))],
        compiler_params=_cparams(("arbitrary",), VMEM_LIMIT),
        name="moe_combine_ln",
    )(dest_flat, dest_flat, gates, x1, g, b, ys3)


def _pad_cols(a, width):
    return jnp.pad(a, ((0, 0), (0, width - a.shape[1])))


def _lora_layout(a):
    dw = a[:, :DECAY_LORA]
    da = a[:, DECAY_LORA:DECAY_LORA + AAA_LORA]
    dg = a[:, DECAY_LORA + AAA_LORA:]
    return jnp.concatenate([_pad_cols(dw, LANES), _pad_cols(da, LANES), _pad_cols(dg, 2 * LANES)], axis=1)


def _moe_ffn(x1, x1p2, idx, gates, counts, w_gu, b_gu, w_dn, b_dn, ln_g, ln_b, bm, tn, tt):
    n = x1.shape[0]
    nk = n * TOP_K
    n_blocks = nk // bm + N_EXPERTS
    n_rows = n_blocks * bm
    cnt = counts[0, :N_EXPERTS].astype(jnp.int32)
    padded = ((cnt + bm - 1) // bm) * bm
    pends = jnp.cumsum(padded)
    lanes = lambda a: jnp.pad(a, (0, LANES - a.shape[0]))
    pstarts = lanes(pends - padded).astype(F32).reshape(1, LANES)
    pad_off = lanes(jnp.concatenate([pends - padded + cnt, pends[-1:]])).astype(jnp.int32)
    pad_len = lanes(jnp.concatenate([padded - cnt, (n_rows - pends[-1:]) // (bm // 2)])).astype(jnp.int32)
    block_start = jnp.arange(n_blocks, dtype=jnp.int32) * bm
    block_e = jnp.minimum(jnp.sum(pends[None, :] <= block_start[:, None], axis=1), N_EXPERTS - 1).astype(jnp.int32)
    n_valid = (pends[-1:] // bm).astype(jnp.int32)

    dest_flat = _row_dest(idx, pstarts, tt)[:, :TOP_K].reshape(nk)
    xs3 = _dispatch(dest_flat, pad_off, pad_len, x1p2, n_rows, tt, bm)
    act = _moe_gate_up(block_e, n_valid, xs3.reshape(n_rows * ROW_TILE, LANES), w_gu,
                       b_gu.reshape(N_EXPERTS, 1, 2 * D_FF), bm, tn)
    ys2 = _moe_down(block_e, n_valid, act, w_dn, b_dn.reshape(N_EXPERTS, 1, D_MODEL), bm)
    return _combine_ln(dest_flat, gates, x1, ln_g, ln_b, ys2.reshape(n_rows, ROW_TILE, LANES), tt)


def _layer(x, w_in, shift_mu, w0, w_up, a0, a_up, g_up, k_k, k_a, r_k, gn_g, gn_b,
           lq1, lk1, lq2, lk2, subln_g, w_out, ln1_g, ln1_b,
           w_router, b_router, w_gu, b_gu, w_dn, b_dn, ln2_g, ln2_b, lambda_init,
           tm_in=2048, tseq=512, ng=4, tq=512, tm_out=512, bm=512, tn=1024, tt=256):
    batch, seq, d = x.shape
    n = batch * seq
    rw = 3 * RWKV_WIDTH
    rcols = rw + DECAY_LORA + AAA_LORA + GATE_LORA
    row = lambda a: a.reshape(1, -1)

    xf = x.reshape(n, d)
    xb = xf.astype(BF16)
    w_r = jnp.concatenate([w_in[:, :rw], _lora_layout(w_in[:, rw:rcols])], axis=1).astype(BF16)
    w_d = w_in[:, rcols:].astype(BF16)
    mu = jnp.concatenate([row(shift_mu)[:, :rw], _lora_layout(row(shift_mu)[:, rw:])], axis=1)
    p_r = _matmul(xb, w_r, F32, tm_in, 512)
    p_d = _matmul(xb, w_d, BF16, tm_in, 512)

    pad_rows = lambda a, rows: jnp.pad(a, ((0, rows - a.shape[0]), (0, 0))).astype(BF16)
    h_r = _rwkv(p_r, mu, row(w0), row(a0), row(k_k), row(k_a), row(r_k), row(gn_g), row(gn_b),
                pad_rows(w_up, LANES), pad_rows(a_up, LANES), pad_rows(g_up, 2 * LANES), batch, seq, tseq, ng)
    h_d = _diff_attention(p_d, row(lq1), row(lk1), row(lq2), row(lk2), row(subln_g), lambda_init,
                          batch, seq, tq)

    w_ob = w_out.astype(BF16)
    w_rp = _pad_cols(w_router, LANES)
    b_rp = jnp.concatenate([row(b_router), jnp.full((1, LANES - N_EXPERTS), NEG_BIG, F32)], axis=1)
    x1, x1p2, idx, gates, counts = _outproj_ln_router(
        h_r, h_d, xf, w_ob[:RWKV_WIDTH], w_ob[RWKV_WIDTH:], row(ln1_g), row(ln1_b), w_rp, b_rp, tm_out)
    out = _moe_ffn(x1, x1p2, idx, gates, counts, w_gu, b_gu, w_dn, b_dn, row(ln2_g), row(ln2_b), bm, tn, tt)
    return out.reshape(batch, seq, d)


def kernel(x, w_in, shift_mu, w0, w_up, a0, a_up, g_up, k_k, k_a, r_k, gn_g, gn_b, lq1, lk1, lq2, lk2,
           subln_g, w_out, ln1_g, ln1_b, w_router, b_router, w_gu, b_gu, w_dn, b_dn, ln2_g, ln2_b):
    for l in range(DEPTH):
        lambda_init = 0.8 - 0.6 * math.exp(-0.3 * l)
        x = _layer(x, w_in[l], shift_mu[l], w0[l], w_up[l], a0[l], a_up[l], g_up[l], k_k[l], k_a[l],
                   r_k[l], gn_g[l], gn_b[l], lq1[l], lk1[l], lq2[l], lk2[l], subln_g[l], w_out[l],
                   ln1_g[l], ln1_b[l], w_router[l], b_router[l], w_gu[l], b_gu[l], w_dn[l], b_dn[l],
                   ln2_g[l], ln2_b[l], lambda_init)
    return x
```

```python
import functools
import math

import jax
import jax.numpy as jnp
from jax import lax
from jax.experimental import pallas as pl
from jax.experimental.pallas import tpu as pltpu

F32 = jnp.float32
BF16 = jnp.bfloat16

D_MODEL = 2048
RWKV_HEAD = 64
RWKV_WIDTH = 1024
RWKV_HEADS = 16
DECAY_LORA = 64
AAA_LORA = 64
GATE_LORA = 160
DIFF_HEAD = 64
DIFF_VDIM = 128
DIFF_HEADS = 8
DIFF_WIDTH = 1024
N_EXPERTS = 32
TOP_K = 4
D_FF = 2048
SWIGLU_LIMIT = 7.0
SWIGLU_ALPHA = 1.702
LN_EPS = 1e-5
GN_EPS = RWKV_HEAD * 1e-5
RMS_EPS = 1e-5
NEG_BIG = -1e30
DEPTH = 1
DEEPNORM_ALPHA = (2.0 * DEPTH) ** 0.25

LANES = 128
SUBLANES = 8
ROW_WORDS = D_MODEL // 2
ROW_TILE = ROW_WORDS // LANES
RWKV_GROUP = 256
RWKV_CHUNK = 64
LORA_COLS = 512
VMEM_LIMIT = 56 * 1024 * 1024
HIGH_HALF = -65536
MXU_DEPTH = 256
OUTPROJ_PARTS = 2


def _cparams(sem, vmem=None):
    return pltpu.CompilerParams(dimension_semantics=sem, vmem_limit_bytes=vmem)


def _dot(a, b):
    return jnp.dot(a, b, preferred_element_type=F32)


def _dot_nt(a, b):
    return lax.dot_general(a, b, (((1,), (1,)), ((), ())), preferred_element_type=F32)


def _dot_tn(a, b):
    return lax.dot_general(a, b, (((0,), (0,)), ((), ())), preferred_element_type=F32)


def _split3(x):
    h = x.astype(BF16)
    r = x - h.astype(F32)
    m = r.astype(BF16)
    l = (r - m.astype(F32)).astype(BF16)
    return h, m, l


def _dot_exact_rhs(x, ones):
    h, m, l = _split3(x)
    return _dot(h, ones) + _dot(m, ones) + _dot(l, ones)


def _dot_exact_lhs(ones, x):
    h, m, l = _split3(x)
    return _dot(ones, h) + _dot(ones, m) + _dot(ones, l)


def _sigmoid(x):
    return 1.0 / (1.0 + jnp.exp(-x))


def _matmul_kernel(x_ref, w_ref, o_ref):
    o_ref[...] = _dot(x_ref[...], w_ref[...]).astype(o_ref.dtype)


def _matmul(x, w, out_dtype, tm, tn):
    m, k = x.shape
    n = w.shape[1]
    return pl.pallas_call(
        _matmul_kernel,
        grid=(n // tn, m // tm),
        in_specs=[pl.BlockSpec((tm, k), lambda j, i: (i, 0)),
                  pl.BlockSpec((k, tn), lambda j, i: (0, j))],
        out_specs=pl.BlockSpec((tm, tn), lambda j, i: (i, j)),
        out_shape=jax.ShapeDtypeStruct((m, n), out_dtype),
        compiler_params=_cparams(("parallel", "parallel"), VMEM_LIMIT),
        name="in_proj",
    )(x, w)


def _rwkv_kernel(r_ref, k_ref, v_ref, l_ref, mur_ref, muk_ref, muv_ref, mul_ref,
                 w0_ref, a0_ref, kk_ref, ka_ref, rk_ref, gng_ref, gnb_ref,
                 wup_ref, aup_ref, gup_ref, o_ref,
                 pr_s, pk_s, pv_s, pl_s, state_s, r_s, w_s, k_s, v_s, a_s, b_s, g_s,
                 y_s, rc_p, lrb_p, lrk_p, tb_p, wc_p, akv_p, be_p, ke_p, dec_p):
    s = pl.program_id(2)
    T = r_ref.shape[0]
    G = RWKV_GROUP
    C = RWKV_CHUNK
    NG = r_ref.shape[1] // G
    groups = [slice(g * G, (g + 1) * G) for g in range(NG)]

    @pl.when(s == 0)
    def _():
        state_s[...] = jnp.zeros_like(state_s)
        pr_s[...] = jnp.zeros_like(pr_s)
        pk_s[...] = jnp.zeros_like(pk_s)
        pv_s[...] = jnp.zeros_like(pv_s)
        pl_s[...] = jnp.zeros_like(pl_s)

    row = lax.broadcasted_iota(jnp.int32, (T, 1), 0)

    def shift(ref, prev_s, mu_ref):
        p = ref[...]
        prev = jnp.where(row == 0, prev_s[...], pltpu.roll(p, 1, 0))
        prev_s[...] = p[T - 1:T, :]
        return p + (prev - p) * mu_ref[...]

    r = shift(r_ref, pr_s, mur_ref)
    k = shift(k_ref, pk_s, muk_ref)
    v = shift(v_ref, pv_s, muv_ref)
    lo = shift(l_ref, pl_s, mul_ref)
    dw = lo[:, 0:LANES]
    da = lo[:, LANES:2 * LANES]
    dg = lo[:, 2 * LANES:4 * LANES]

    wpre = w0_ref[...] + _dot(jnp.tanh(dw).astype(BF16), wup_ref[...])
    wlog = -math.exp(-0.5) * _sigmoid(wpre)
    a_sig = _sigmoid(a0_ref[...] + _dot(da.astype(BF16), aup_ref[...]))
    gate = _dot(_sigmoid(dg).astype(BF16), gup_ref[...])

    gi = lax.broadcasted_iota(jnp.int32, (G, G), 0)
    gj = lax.broadcasted_iota(jnp.int32, (G, G), 1)
    same_head = (gi // RWKV_HEAD) == (gj // RWKV_HEAD)
    head_ones = jnp.where(same_head, 1.0, 0.0).astype(BF16)

    def head_sum(x):
        return jnp.concatenate([_dot_exact_rhs(x[:, g], head_ones) for g in groups], axis=1)

    kk = k * kk_ref[...]
    nrm = jnp.sqrt(head_sum(kk * kk))
    kk = kk / jnp.maximum(nrm, 1e-12)
    k2 = k * (1.0 + (a_sig - 1.0) * ka_ref[...])

    r_s[...] = r
    w_s[...] = wlog
    k_s[...] = k2
    v_s[...] = v
    a_s[...] = -kk
    b_s[...] = kk * a_sig
    g_s[...] = gate

    ci = lax.broadcasted_iota(jnp.int32, (C, C), 0)
    cj = lax.broadcasted_iota(jnp.int32, (C, C), 1)
    tri = jnp.where(cj <= ci, 1.0, 0.0).astype(BF16)
    mt = lax.broadcasted_iota(jnp.int32, (C, G), 0)
    mtp = lax.broadcasted_iota(jnp.int32, (C, G), 1) % C
    strict = mtp < mt
    incl = mtp <= mt
    ceye = jnp.where(mtp == mt, 1.0, 0.0)

    def bd(xc):
        return jnp.where(same_head, jnp.concatenate([xc, xc, xc, xc], axis=0), jnp.zeros((), BF16))

    def each(fn, *lists):
        return [fn(*xs) for xs in zip(*lists)]

    def bf(x):
        return x.astype(BF16)

    def prepare(cp, carry):
        streams = [(pl.ds(pl.multiple_of((2 * cp + h) * C, C), C), g) for h in range(2) for g in groups]
        rows = [2 * cp + h for h in range(2) for _ in groups]
        rc = [r_s[sl, g] for sl, g in streams]
        wc = [w_s[sl, g] for sl, g in streams]
        kc = [k_s[sl, g] for sl, g in streams]
        vc = [v_s[sl, g] for sl, g in streams]
        ac = [a_s[sl, g] for sl, g in streams]
        bc = [b_s[sl, g] for sl, g in streams]
        cum = each(lambda w: _dot_exact_lhs(tri, w), wc)
        tot = each(lambda x: x[C - 1:C, :], cum)
        ginv = each(lambda x: jnp.exp(-x), cum)
        gend = each(lambda x, t: jnp.exp(t - x), cum, tot)
        r_c = each(lambda x, g: bf(x * jnp.exp(g)), rc, cum)
        a_c = each(lambda x, g, w: bf(x * jnp.exp(g - w)), ac, cum, wc)
        v_c = each(bf, vc)
        k_bd = each(lambda x, g: bd(bf(x * g)), kc, ginv)
        b_bd = each(lambda x, g: bd(bf(x * g)), bc, ginv)
        ke_c = each(lambda x, g: bf(x * g), kc, gend)
        be_c = each(lambda x, g: bf(x * g), bc, gend)
        v_bd = each(bd, v_c)
        a_bd = each(bd, a_c)

        ar = each(lambda a, r: jnp.concatenate([a, r], axis=0), a_c, r_c)
        arb = each(_dot_nt, ar, b_bd)
        ark = each(_dot_nt, ar, k_bd)
        l_ab = each(lambda x: jnp.where(strict, x[:C], 0.0), arb)
        l_ak = each(lambda x: bf(jnp.where(strict, x[:C], 0.0)), ark)
        l_rb = each(lambda x: bf(jnp.where(incl, x[C:], 0.0)), arb)
        l_rk = each(lambda x: bf(jnp.where(incl, x[C:], 0.0)), ark)

        p = each(bf, l_ab)
        p_bd = each(bd, p)
        tinv = each(lambda x: ceye + x, l_ab)
        for _ in range(int(math.log2(C)) - 1):
            p = each(lambda x, y: bf(_dot(x, y)), p, p_bd)
            p_bd = each(bd, p)
            tinv = each(lambda t, y: t + _dot(bf(t), y), tinv, p_bd)
        tb = each(bf, tinv)
        akv_c = each(lambda a, b: bf(_dot(a, b)), l_ak, v_bd)
        w_c = each(lambda a, b: bf(_dot(a, b)), tb, a_bd)
        for ref, vals in zip((rc_p, lrb_p, lrk_p, tb_p, wc_p, akv_p, be_p, ke_p),
                             (r_c, l_rb, l_rk, tb, w_c, akv_c, be_c, ke_c)):
            for (sl, g), val in zip(streams, vals):
                ref[sl, g] = val
        for (_, g), row, t in zip(streams, rows, tot):
            dec_p[pl.ds(row, 1), g] = jnp.exp(t)
        return carry

    lax.fori_loop(0, T // C // 2, prepare, 0)

    def chunk(c, carry):
        sl = pl.ds(pl.multiple_of(c * C, C), C)
        load = lambda ref: [ref[sl, g] for g in groups]
        r_c, l_rb, l_rk, tb, w_c, be_c, ke_c = (load(ref) for ref in
                                                (rc_p, lrb_p, lrk_p, tb_p, wc_p, be_p, ke_p))
        akv_bd = each(bd, load(akv_p))
        v_c = each(bf, load(v_s))
        v_bd = each(bd, v_c)
        dec = [dec_p[pl.ds(c, 1), g] for g in groups]
        st = [state_s[g] for g in range(NG)]
        stb = each(bf, st)
        u_c = each(lambda w, s0, t, x: bf(_dot_nt(w, s0) + _dot(t, x)), w_c, stb, tb, akv_bd)
        u_bd = each(bd, u_c)
        y = each(lambda r, s0, lb, lk, u, vv: _dot_nt(r, s0) + _dot(jnp.concatenate([lb, lk], axis=1),
                                                                    jnp.concatenate([u, vv], axis=0)),
                 r_c, stb, l_rb, l_rk, u_bd, v_bd)
        new = each(lambda s0, t, u, vv, b, kx: s0 * t + jnp.where(
            same_head, _dot_tn(jnp.concatenate([u, vv], axis=0), jnp.concatenate([b, kx], axis=0)), 0.0),
            st, dec, u_c, v_c, be_c, ke_c)
        for g in range(NG):
            state_s[g] = new[g]
        for g, yy in zip(groups, y):
            y_s[sl, g] = yy
        return carry

    lax.fori_loop(0, T // C, chunk, 0)

    def finish(cp, carry):
        streams = [(pl.ds(pl.multiple_of((2 * cp + h) * C, C), C), g) for h in range(2) for g in groups]
        y = [y_s[sl, g] for sl, g in streams]
        mean = each(lambda x: _dot_exact_rhs(x, head_ones) * (1.0 / RWKV_HEAD), y)
        d = each(lambda x, m: x - m, y, mean)
        var = each(lambda x: _dot_exact_rhs(x * x, head_ones) * (1.0 / RWKV_HEAD), d)
        bonus = [_dot_exact_rhs(r_s[sl, g] * k_s[sl, g] * rk_ref[:, g], head_ones) * v_s[sl, g]
                 for sl, g in streams]
        for (sl, g), dd, vr, bo in zip(streams, d, var, bonus):
            yn = dd * lax.rsqrt(vr + GN_EPS) * gng_ref[:, g] + gnb_ref[:, g]
            o_ref[sl, g] = ((yn + bo) * g_s[sl, g]).astype(o_ref.dtype)
        return carry

    lax.fori_loop(0, T // C // 2, finish, 0)


def _rwkv(p_r, mu, w0, a0, k_k, k_a, r_k, gn_g, gn_b, w_up, a_up, g_up, batch, seq, tseq, ng):
    n = batch * seq
    G = ng * RWKV_GROUP
    nq = RWKV_WIDTH // G
    ns = seq // tseq
    lora_blk = 3 * RWKV_WIDTH // LORA_COLS

    def tok(off):
        return pl.BlockSpec((tseq, G), lambda b, q, s: (b * ns + s, off + q))

    def par(off):
        return pl.BlockSpec((1, G), lambda b, q, s: (0, off + q))

    in_specs = [
        tok(0), tok(nq), tok(2 * nq),
        pl.BlockSpec((tseq, LORA_COLS), lambda b, q, s: (b * ns + s, lora_blk)),
        par(0), par(nq), par(2 * nq),
        pl.BlockSpec((1, LORA_COLS), lambda b, q, s: (0, lora_blk)),
        par(0), par(0), par(0), par(0), par(0), par(0), par(0),
        pl.BlockSpec((LANES, G), lambda b, q, s: (0, q)),
        pl.BlockSpec((LANES, G), lambda b, q, s: (0, q)),
        pl.BlockSpec((2 * LANES, G), lambda b, q, s: (0, q)),
    ]
    scratch = [pltpu.VMEM((1, G), F32), pltpu.VMEM((1, G), F32), pltpu.VMEM((1, G), F32),
               pltpu.VMEM((1, LORA_COLS), F32), pltpu.VMEM((ng, RWKV_GROUP, RWKV_GROUP), F32)]
    scratch += [pltpu.VMEM((tseq, G), F32) for _ in range(8)]
    scratch += [pltpu.VMEM((tseq, G), BF16) for _ in range(8)]
    scratch += [pltpu.VMEM((tseq // RWKV_CHUNK, G), F32)]
    return pl.pallas_call(
        _rwkv_kernel,
        grid=(batch, nq, ns),
        in_specs=in_specs,
        out_specs=pl.BlockSpec((tseq, G), lambda b, q, s: (b * ns + s, q)),
        out_shape=jax.ShapeDtypeStruct((n, RWKV_WIDTH), BF16),
        scratch_shapes=scratch,
        compiler_params=_cparams(("parallel", "parallel", "arbitrary"), VMEM_LIMIT),
        name="rwkv7",
    )(p_r, p_r, p_r, p_r, mu, mu, mu, mu, w0, a0, k_k, k_a, r_k, gn_g, gn_b, w_up, a_up, g_up)


def _attn_kernel(q_ref, k_ref, v_ref, lq1_ref, lk1_ref, lq2_ref, lk2_ref, g_ref, o_ref, vt_s, *, lambda_init):
    i = pl.program_id(2)
    tq = q_ref.shape[0]

    @pl.when(i == 0)
    def _():
        for j in range(vt_s.shape[0]):
            vt_s[j] = v_ref[j * tq:(j + 1) * tq, :].astype(F32).T.astype(BF16)

    drow = lax.broadcasted_iota(jnp.int32, (DIFF_VDIM, 1), 0)
    qt = (q_ref[...].astype(F32) * (DIFF_HEAD ** -0.5)).T
    q1 = jnp.where(drow < DIFF_HEAD, qt, 0.0).astype(BF16)
    q2 = jnp.where(drow >= DIFF_HEAD, qt, 0.0).astype(BF16)
    lam = (jnp.exp(jnp.sum(lq1_ref[...] * lk1_ref[...], axis=-1, keepdims=True))
           - jnp.exp(jnp.sum(lq2_ref[...] * lk2_ref[...], axis=-1, keepdims=True)) + lambda_init)

    key = lax.broadcasted_iota(jnp.int32, (tq, tq), 0)
    qry = lax.broadcasted_iota(jnp.int32, (tq, tq), 1)

    def scores(j, diagonal):
        kj = k_ref[j * tq:(j + 1) * tq, :]
        sc = (_dot(kj, q1), _dot(kj, q2))
        if diagonal:
            sc = tuple(jnp.where(key <= qry, s, NEG_BIG) for s in sc)
        return sc

    def absorb(j, sc, stats):
        m, l, acc = stats[0:2], stats[2:4], stats[4:6]
        vtj = vt_s[j]
        m_new = [jnp.maximum(mm, jnp.max(s, axis=0, keepdims=True)) for mm, s in zip(m, sc)]
        alpha = [jnp.exp(mm - mn) for mm, mn in zip(m, m_new)]
        p = [jnp.exp(s - mn) for s, mn in zip(sc, m_new)]
        l = [a * ll + jnp.sum(pp, axis=0, keepdims=True) for a, ll, pp in zip(alpha, l, p)]
        pv = [_dot(vtj, pp.astype(BF16)) for pp in p]
        acc = [a * ac + x for a, ac, x in zip(alpha, acc, pv)]
        return tuple(m_new) + tuple(l) + tuple(acc)

    zero1 = jnp.zeros((1, tq), F32)
    neg1 = jnp.full((1, tq), NEG_BIG, F32)
    zacc = jnp.zeros((DIFF_VDIM, tq), F32)
    init = (neg1, neg1, zero1, zero1, zacc, zacc)

    for iv in range(vt_s.shape[0]):
        @pl.when(i == iv)
        def _(iv=iv):
            stats = init
            sc = scores(0, iv == 0)
            for j in range(iv + 1):
                nxt = scores(j + 1, j + 1 == iv) if j < iv else None
                stats = absorb(j, sc, stats)
                sc = nxt
            l, acc = stats[2:4], stats[4:6]
            o = acc[0] / l[0] - lam * (acc[1] / l[1])
            o = o * lax.rsqrt(jnp.mean(o * o, axis=0, keepdims=True) + RMS_EPS) * g_ref[...]
            o_ref[...] = (o * (1.0 - lambda_init)).T.astype(o_ref.dtype)


def _diff_attention(p_d, lq1, lk1, lq2, lk2, subln_g, lambda_init, batch, seq, tq):
    n = batch * seq
    nq = seq // tq
    H = DIFF_HEADS
    small = pl.BlockSpec((1, DIFF_HEAD), lambda b, h, i: (0, 0))
    return pl.pallas_call(
        functools.partial(_attn_kernel, lambda_init=lambda_init),
        grid=(batch, H, nq),
        in_specs=[pl.BlockSpec((tq, DIFF_VDIM), lambda b, h, i: (b * nq + i, h)),
                  pl.BlockSpec((seq, DIFF_VDIM), lambda b, h, i: (b, H + h)),
                  pl.BlockSpec((seq, DIFF_VDIM), lambda b, h, i: (b, 2 * H + h)),
                  small, small, small, small,
                  pl.BlockSpec((DIFF_VDIM, 1), lambda b, h, i: (0, 0))],
        out_specs=pl.BlockSpec((tq, DIFF_VDIM), lambda b, h, i: (b * nq + i, h)),
        out_shape=jax.ShapeDtypeStruct((n, DIFF_WIDTH), BF16),
        scratch_shapes=[pltpu.VMEM((nq, DIFF_VDIM, tq), BF16)],
        compiler_params=_cparams(("parallel", "parallel", "arbitrary"), VMEM_LIMIT),
        name="diff_attn",
    )(p_d, p_d, p_d, lq1, lk1, lq2, lk2, subln_g.reshape(DIFF_VDIM, 1))


def _layer_norm(y, g, b):
    mu = jnp.mean(y, axis=-1, keepdims=True)
    d = y - mu
    var = jnp.mean(d * d, axis=-1, keepdims=True)
    return d * lax.rsqrt(var + LN_EPS) * g + b


def _store_row_tiles(ref, words, start=0, sublane=0):
    rows = words.shape[0]
    for s in range(words.shape[1] // LANES):
        ref[pl.ds(start * ROW_TILE + sublane + s, rows, stride=ROW_TILE), :] = words[:, s * LANES:(s + 1) * LANES]


def _load_row_tiles(ref, start, rows):
    parts = [ref[pl.ds(start * ROW_TILE + s, rows, stride=ROW_TILE), :] for s in range(ROW_TILE)]
    return jnp.concatenate(parts, axis=1)


def _pack_pair(lo, hi):
    bits = lambda y: lax.bitcast_convert_type(y.astype(BF16).astype(F32), jnp.int32)
    return lax.shift_right_logical(bits(lo), 16) | (bits(hi) & HIGH_HALF)


def _pack_row(y):
    return _pack_pair(y[:, :ROW_WORDS], y[:, ROW_WORDS:])


def _unpack_row(words):
    lo = lax.bitcast_convert_type(lax.shift_left(words, 16), F32)
    hi = lax.bitcast_convert_type(words & HIGH_HALF, F32)
    return lo, hi


def _expert_onehots(idx):
    lane = lax.broadcasted_iota(jnp.int32, idx.shape, 1)
    sels = [lane == idx[:, kk:kk + 1] for kk in range(TOP_K)]
    onehot = jnp.zeros(idx.shape, F32)
    for sel in sels:
        onehot = onehot + jnp.where(sel, 1.0, 0.0)
    return sels, onehot


def _outproj_kernel(hr_ref, hd_ref, x_ref, wt_ref, wb_ref, g_ref, b_ref, wrh_ref, wrl_ref, br_ref,
                    x1_ref, x1p_ref, idx_ref, gate_ref, cnt_ref, carry_s):
    i = pl.program_id(0)
    tm = x_ref.shape[0]
    th = tm // OUTPROJ_PARTS
    parts = [pl.ds(h * th, th) for h in range(OUTPROJ_PARTS)]

    @pl.when(i == 0)
    def _():
        carry_s[...] = jnp.zeros_like(carry_s)

    def each(fn, *lists):
        return [fn(*xs) for xs in zip(*lists)]

    mix = [_dot(hr_ref[p, :], wt_ref[...]) + _dot(hd_ref[p, :], wb_ref[...]) for p in parts]
    x1 = [_layer_norm(DEEPNORM_ALPHA * x_ref[p, :] + m, g_ref[...], b_ref[...]) for p, m in zip(parts, mix)]
    for h, (p, y) in enumerate(zip(parts, x1)):
        x1_ref[p, :] = y
        _store_row_tiles(x1p_ref, _pack_row(y), h * th)

    xh = each(lambda y: y.astype(BF16), x1)
    xl = each(lambda y, hh: (y - hh.astype(F32)).astype(BF16), x1, xh)
    work = each(lambda hh, ll: _dot(hh, wrh_ref[...]) + _dot(ll, wrh_ref[...]) + _dot(hh, wrl_ref[...])
                + br_ref[...], xh, xl)
    lane = lax.broadcasted_iota(jnp.int32, (th, LANES), 1).astype(F32)
    onehot = [jnp.zeros((th, LANES), F32) for _ in parts]
    vals, idxs = [], []
    for _ in range(TOP_K):
        mx = each(lambda w: jnp.max(w, axis=-1, keepdims=True), work)
        idx = each(lambda w, m: jnp.min(jnp.where(w == m, lane, float(LANES)), axis=-1, keepdims=True), work, mx)
        sel = each(lambda ix: lane == ix, idx)
        work = each(lambda s, w: jnp.where(s, -jnp.inf, w), sel, work)
        onehot = each(lambda o, s: o + jnp.where(s, 1.0, 0.0), onehot, sel)
        vals.append(mx)
        idxs.append(idx)
    total = carry_s[0:1, :]
    for h, p in enumerate(parts):
        exps = [jnp.exp(vv[h] - vals[0][h]) for vv in vals]
        den = exps[0] + exps[1] + exps[2] + exps[3]
        idx_out = jnp.zeros((th, LANES), F32)
        gate_out = jnp.zeros((th, LANES), F32)
        for kk in range(TOP_K):
            slot = lane == float(kk)
            idx_out = jnp.where(slot, idxs[kk][h], idx_out)
            gate_out = jnp.where(slot, exps[kk] / den, gate_out)
        idx_ref[p, :] = idx_out.astype(jnp.int32)
        gate_ref[p, :] = gate_out
        total = total + jnp.sum(onehot[h], axis=0, keepdims=True)
    carry_s[...] = jnp.broadcast_to(total, carry_s.shape)
    cnt_ref[...] = jnp.broadcast_to(total, cnt_ref.shape)


def _outproj_ln_router(hr, hd, x, w_top, w_bot, g, b, w_r, b_r, tm):
    n = x.shape[0]
    w_rh = w_r.astype(BF16)
    w_rl = (w_r - w_rh.astype(F32)).astype(BF16)
    const = lambda shape: pl.BlockSpec(shape, lambda i: (0, 0))
    rowb = lambda cols: pl.BlockSpec((tm, cols), lambda i: (i, 0))
    out_shape = (jax.ShapeDtypeStruct((n, D_MODEL), F32),
                 jax.ShapeDtypeStruct((n * ROW_TILE, LANES), jnp.int32),
                 jax.ShapeDtypeStruct((n, LANES), jnp.int32),
                 jax.ShapeDtypeStruct((n, LANES), F32),
                 jax.ShapeDtypeStruct((SUBLANES, LANES), F32))
    return pl.pallas_call(
        _outproj_kernel,
        grid=(n // tm,),
        in_specs=[rowb(RWKV_WIDTH), rowb(DIFF_WIDTH), rowb(D_MODEL),
                  const((RWKV_WIDTH, D_MODEL)), const((DIFF_WIDTH, D_MODEL)),
                  const((1, D_MODEL)), const((1, D_MODEL)),
                  const((D_MODEL, LANES)), const((D_MODEL, LANES)), const((1, LANES))],
        out_specs=(rowb(D_MODEL), pl.BlockSpec((tm * ROW_TILE, LANES), lambda i: (i, 0)),
                   rowb(LANES), rowb(LANES), const((SUBLANES, LANES))),
        out_shape=out_shape,
        scratch_shapes=[pltpu.VMEM((SUBLANES, LANES), F32)],
        compiler_params=_cparams(("arbitrary",), VMEM_LIMIT),
        name="outproj_ln_router",
    )(hr, hd, x, w_top, w_bot, g, b, w_rh, w_rl, b_r)


def _dest_kernel(idx_ref, ps_ref, dest_ref, carry_s):
    i = pl.program_id(0)
    tm = idx_ref.shape[0]

    @pl.when(i == 0)
    def _():
        carry_s[...] = jnp.broadcast_to(ps_ref[...], carry_s.shape)

    sels, onehot = _expert_onehots(idx_ref[...])
    ti = lax.broadcasted_iota(jnp.int32, (tm, tm), 0)
    tj = lax.broadcasted_iota(jnp.int32, (tm, tm), 1)
    before = jnp.where(tj < ti, 1.0, 0.0).astype(BF16)
    cum = _dot(before, onehot.astype(BF16)) + carry_s[0:1, :]
    lane = lax.broadcasted_iota(jnp.int32, (tm, LANES), 1)
    dest = jnp.zeros((tm, LANES), F32)
    for kk in range(TOP_K):
        dk = jnp.sum(jnp.where(sels[kk], cum, 0.0), axis=-1, keepdims=True)
        dest = jnp.where(lane == kk, dk, dest)
    dest_ref[...] = dest.astype(jnp.int32)
    total = carry_s[0:1, :] + jnp.sum(onehot, axis=0, keepdims=True)
    carry_s[...] = jnp.broadcast_to(total, carry_s.shape)


def _row_dest(idx, pstarts, tm):
    n = idx.shape[0]
    return pl.pallas_call(
        _dest_kernel,
        grid=(n // tm,),
        in_specs=[pl.BlockSpec((tm, LANES), lambda i: (i, 0)), pl.BlockSpec((1, LANES), lambda i: (0, 0))],
        out_specs=pl.BlockSpec((tm, LANES), lambda i: (i, 0)),
        out_shape=jax.ShapeDtypeStruct((n, LANES), jnp.int32),
        scratch_shapes=[pltpu.VMEM((SUBLANES, LANES), F32)],
        compiler_params=_cparams(("arbitrary",)),
        name="moe_row_dest",
    )(idx, pstarts)


def _wait_rows(ref, rows, sem):
    view = ref.at[pl.ds(0, rows)]
    pltpu.make_async_copy(view, view, sem).wait()


def _dispatch_kernel(dest_ref, padoff_ref, padlen_ref, x_ref, xs_hbm, zero_s, sem, zsem):
    tt = x_ref.shape[0] // ROW_TILE

    def pad_copies(e, go):
        off = padoff_ref[e]
        ln = padlen_ref[e]
        for bit in range(zero_s.shape[0].bit_length()):
            size = 1 << bit

            @pl.when((ln >> bit) & 1 == 1)
            def _():
                go(pltpu.make_async_copy(zero_s.at[pl.ds(0, size)],
                                         xs_hbm.at[pl.ds(off + (ln & (size - 1)), size)], zsem))

    def tail_copy(piece):
        rows = zero_s.shape[0]
        first = pl.multiple_of(padoff_ref[N_EXPERTS] + piece * rows, rows)
        return pltpu.make_async_copy(zero_s, xs_hbm.at[pl.ds(first, rows)], zsem)

    @pl.when(pl.program_id(0) == 0)
    def _():
        zero_s[...] = jnp.zeros_like(zero_s)
        for go in (lambda cp: cp.start(), lambda cp: cp.wait()):
            def body(e, c, go=go):
                pad_copies(e, go)
                return c
            lax.fori_loop(0, N_EXPERTS, body, 0)

            def tail(piece, c, go=go):
                go(tail_copy(piece))
                return c
            lax.fori_loop(0, padlen_ref[N_EXPERTS], tail, 0)

    def issue(t, c):
        src = x_ref.at[pl.ds(pl.multiple_of(t * ROW_TILE, ROW_TILE), ROW_TILE), :]
        for kk in range(TOP_K):
            pltpu.make_async_copy(src, xs_hbm.at[dest_ref[t * TOP_K + kk]], sem).start(priority=kk % 2)
        return c

    lax.fori_loop(0, tt, issue, 0)
    for _ in range(TOP_K):
        _wait_rows(xs_hbm, tt, sem)


def _dispatch(dest_flat, pad_off, pad_len, x1p2, n_rows, tt, bm):
    n = x1p2.shape[0] // ROW_TILE
    table = pl.BlockSpec((LANES,), lambda i: (0,), memory_space=pltpu.SMEM)
    return pl.pallas_call(
        _dispatch_kernel,
        grid=(n // tt,),
        in_specs=[pl.BlockSpec((tt * TOP_K,), lambda i: (i,), memory_space=pltpu.SMEM), table, table,
                  pl.BlockSpec((tt * ROW_TILE, LANES), lambda i: (i, 0))],
        out_specs=pl.BlockSpec(memory_space=pl.ANY),
        out_shape=jax.ShapeDtypeStruct((n_rows, ROW_TILE, LANES), jnp.int32),
        scratch_shapes=[pltpu.VMEM((bm // 2, ROW_TILE, LANES), jnp.int32),
                        pltpu.SemaphoreType.DMA, pltpu.SemaphoreType.DMA],
        compiler_params=_cparams(("arbitrary",)),
        name="moe_dispatch",
    )(dest_flat, pad_off, pad_len, x1p2)


def _dot_f32_weights(xk, w_ref, cols):
    acc = None
    for kc, x in enumerate(xk):
        part = _dot(x, w_ref[0, kc * MXU_DEPTH:(kc + 1) * MXU_DEPTH, cols].astype(BF16))
        acc = part if acc is None else acc + part
    return acc


def _gu_kernel(be_ref, nv_ref, xs_ref, wg_ref, wu_ref, bg_ref, bu_ref, o_ref):
    j = pl.program_id(1)
    bm = o_ref.shape[0]

    @pl.when(j < nv_ref[0])
    def _():
        lo, hi = _unpack_row(_load_row_tiles(xs_ref, 0, bm))
        x = jnp.concatenate([lo, hi], axis=1).astype(BF16)
        xk = [x[:, kc * MXU_DEPTH:(kc + 1) * MXU_DEPTH] for kc in range(D_MODEL // MXU_DEPTH)]
        g = _dot_f32_weights(xk, wg_ref, slice(None)) + bg_ref[0]
        u = _dot_f32_weights(xk, wu_ref, slice(None)) + bu_ref[0]
        g = jnp.minimum(g, SWIGLU_LIMIT)
        u = jnp.clip(u, -SWIGLU_LIMIT, SWIGLU_LIMIT)
        o_ref[...] = ((u + 1.0) * (g * _sigmoid(SWIGLU_ALPHA * g))).astype(o_ref.dtype)

    @pl.when(j >= nv_ref[0])
    def _():
        o_ref[...] = jnp.zeros_like(o_ref)


def _moe_gate_up(block_e, n_valid, xs2, w_gu, b_gu, bm, tn):
    n_rows = xs2.shape[0] // ROW_TILE
    n_blocks = n_rows // bm
    nt = D_FF // tn
    grid_spec = pltpu.PrefetchScalarGridSpec(
        num_scalar_prefetch=2,
        grid=(nt, n_blocks),
        in_specs=[pl.BlockSpec((bm * ROW_TILE, LANES), lambda n, j, be, nv: (j, 0)),
                  pl.BlockSpec((1, D_MODEL, tn), lambda n, j, be, nv: (be[j], 0, n)),
                  pl.BlockSpec((1, D_MODEL, tn), lambda n, j, be, nv: (be[j], 0, nt + n)),
                  pl.BlockSpec((1, 1, tn), lambda n, j, be, nv: (be[j], 0, n)),
                  pl.BlockSpec((1, 1, tn), lambda n, j, be, nv: (be[j], 0, nt + n))],
        out_specs=pl.BlockSpec((bm, tn), lambda n, j, be, nv: (j, n)))
    return pl.pallas_call(
        _gu_kernel,
        grid_spec=grid_spec,
        out_shape=jax.ShapeDtypeStruct((n_rows, D_FF), BF16),
        compiler_params=_cparams(("arbitrary", "arbitrary"), VMEM_LIMIT),
        name="moe_gate_up",
    )(block_e, n_valid, xs2, w_gu, w_gu, b_gu, b_gu)


def _dn_kernel(be_ref, nv_ref, a_ref, w_ref, b_ref, o_ref):
    j = pl.program_id(0)
    del be_ref

    @pl.when(j < nv_ref[0])
    def _():
        ak = [a_ref[:, kc * MXU_DEPTH:(kc + 1) * MXU_DEPTH] for kc in range(D_FF // MXU_DEPTH)]
        hw = ROW_WORDS // 2
        for h in range(2):
            lo = slice(h * hw, (h + 1) * hw)
            hi = slice(ROW_WORDS + h * hw, ROW_WORDS + (h + 1) * hw)
            y_lo = _dot_f32_weights(ak, w_ref, lo) + b_ref[0, :, lo]
            y_hi = _dot_f32_weights(ak, w_ref, hi) + b_ref[0, :, hi]
            _store_row_tiles(o_ref, _pack_pair(y_lo, y_hi), 0, h * (hw // LANES))

    @pl.when(j >= nv_ref[0])
    def _():
        o_ref[...] = jnp.zeros_like(o_ref)


def _moe_down(block_e, n_valid, act, w_dn, b_dn, bm):
    n_rows = act.shape[0]
    n_blocks = n_rows // bm
    grid_spec = pltpu.PrefetchScalarGridSpec(
        num_scalar_prefetch=2,
        grid=(n_blocks,),
        in_specs=[pl.BlockSpec((bm, D_FF), lambda j, be, nv: (j, 0)),
                  pl.BlockSpec((1, D_FF, D_MODEL), lambda j, be, nv: (be[j], 0, 0)),
                  pl.BlockSpec((1, 1, D_MODEL), lambda j, be, nv: (be[j], 0, 0))],
        out_specs=pl.BlockSpec((bm * ROW_TILE, LANES), lambda j, be, nv: (j, 0)))
    return pl.pallas_call(
        _dn_kernel,
        grid_spec=grid_spec,
        out_shape=jax.ShapeDtypeStruct((n_rows * ROW_TILE, LANES), jnp.int32),
        compiler_params=_cparams(("arbitrary",), VMEM_LIMIT),
        name="moe_down",
    )(block_e, n_valid, act, w_dn, b_dn)


def _combine_kernel(dest_ref, next_ref, gate_ref, x1_ref, g_ref, b_ref, ys_hbm, o_ref, buf, sem):
    i = pl.program_id(0)
    tt = x1_ref.shape[0]
    slot = i % 2

    def fetch(d_ref, s):
        def body(t, c):
            for kk in range(TOP_K):
                row = pl.multiple_of((kk * tt + t) * ROW_TILE, ROW_TILE)
                pltpu.make_async_copy(ys_hbm.at[d_ref[t * TOP_K + kk]],
                                      buf.at[s, pl.ds(row, ROW_TILE), :], sem.at[s]).start(priority=kk % 2)
            return c
        lax.fori_loop(0, tt, body, 0)

    @pl.when(i == 0)
    def _():
        fetch(dest_ref, 0)

    @pl.when(i + 1 < pl.num_programs(0))
    def _():
        fetch(next_ref, 1 - slot)

    for kk in range(TOP_K):
        view = buf.at[slot, pl.ds(kk * tt * ROW_TILE, tt * ROW_TILE), :]
        pltpu.make_async_copy(view, view, sem.at[slot]).wait()

    gates = gate_ref[...]
    cur = buf.at[slot]
    acc_lo = jnp.zeros((tt, ROW_WORDS), F32)
    acc_hi = jnp.zeros((tt, ROW_WORDS), F32)
    for kk in range(TOP_K):
        lo, hi = _unpack_row(_load_row_tiles(cur, kk * tt, tt))
        gk = gates[:, kk:kk + 1]
        acc_lo = acc_lo + gk * lo
        acc_hi = acc_hi + gk * hi
    ffn = jnp.concatenate([acc_lo, acc_hi], axis=1)
    o_ref[...] = _layer_norm(DEEPNORM_ALPHA * x1_ref[...] + ffn, g_ref[...], b_ref[...])


def _combine_ln(dest_flat, gates, x1, g, b, ys3, tt):
    n = x1.shape[0]
    last = n // tt - 1
    const = lambda shape: pl.BlockSpec(shape, lambda i: (0, 0))
    return pl.pallas_call(
        _combine_kernel,
        grid=(n // tt,),
        in_specs=[pl.BlockSpec((tt * TOP_K,), lambda i: (i,), memory_space=pltpu.SMEM),
                  pl.BlockSpec((tt * TOP_K,), lambda i: (jnp.minimum(i + 1, last),), memory_space=pltpu.SMEM),
                  pl.BlockSpec((tt, LANES), lambda i: (i, 0)),
                  pl.BlockSpec((tt, D_MODEL), lambda i: (i, 0)),
                  const((1, D_MODEL)), const((1, D_MODEL)),
                  pl.BlockSpec(memory_space=pl.ANY)],
        out_specs=pl.BlockSpec((tt, D_MODEL), lambda i: (i, 0)),
        out_shape=jax.ShapeDtypeStruct((n, D_MODEL), F32),
        scratch_shapes=[pltpu.VMEM((2, TOP_K * tt * ROW_TILE, LANES), jnp.int32),
                        pltpu.SemaphoreType.DMA((2,))],
        compiler_params=_cparams(("arbitrary",), VMEM_LIMIT),
        name="moe_combine_ln",
    )(dest_flat, dest_flat, gates, x1, g, b, ys3)


def _pad_cols(a, width):
    return jnp.pad(a, ((0, 0), (0, width - a.shape[1])))


def _lora_layout(a):
    dw = a[:, :DECAY_LORA]
    da = a[:, DECAY_LORA:DECAY_LORA + AAA_LORA]
    dg = a[:, DECAY_LORA + AAA_LORA:]
    return jnp.concatenate([_pad_cols(dw, LANES), _pad_cols(da, LANES), _pad_cols(dg, 2 * LANES)], axis=1)


def _moe_ffn(x1, x1p2, idx, gates, counts, w_gu, b_gu, w_dn, b_dn, ln_g, ln_b, bm, tn, tt):
    n = x1.shape[0]
    nk = n * TOP_K
    n_blocks = nk // bm + N_EXPERTS
    n_rows = n_blocks * bm
    cnt = counts[0, :N_EXPERTS].astype(jnp.int32)
    padded = ((cnt + bm - 1) // bm) * bm
    pends = jnp.cumsum(padded)
    lanes = lambda a: jnp.pad(a, (0, LANES - a.shape[0]))
    pstarts = lanes(pends - padded).astype(F32).reshape(1, LANES)
    pad_off = lanes(jnp.concatenate([pends - padded + cnt, pends[-1:]])).astype(jnp.int32)
    pad_len = lanes(jnp.concatenate([padded - cnt, (n_rows - pends[-1:]) // (bm // 2)])).astype(jnp.int32)
    block_start = jnp.arange(n_blocks, dtype=jnp.int32) * bm
    block_e = jnp.minimum(jnp.sum(pends[None, :] <= block_start[:, None], axis=1), N_EXPERTS - 1).astype(jnp.int32)
    n_valid = (pends[-1:] // bm).astype(jnp.int32)

    dest_flat = _row_dest(idx, pstarts, tt)[:, :TOP_K].reshape(nk)
    xs3 = _dispatch(dest_flat, pad_off, pad_len, x1p2, n_rows, tt, bm)
    act = _moe_gate_up(block_e, n_valid, xs3.reshape(n_rows * ROW_TILE, LANES), w_gu,
                       b_gu.reshape(N_EXPERTS, 1, 2 * D_FF), bm, tn)
    ys2 = _moe_down(block_e, n_valid, act, w_dn, b_dn.reshape(N_EXPERTS, 1, D_MODEL), bm)
    return _combine_ln(dest_flat, gates, x1, ln_g, ln_b, ys2.reshape(n_rows, ROW_TILE, LANES), tt)


def _layer(x, w_in, shift_mu, w0, w_up, a0, a_up, g_up, k_k, k_a, r_k, gn_g, gn_b,
           lq1, lk1, lq2, lk2, subln_g, w_out, ln1_g, ln1_b,
           w_router, b_router, w_gu, b_gu, w_dn, b_dn, ln2_g, ln2_b, lambda_init,
           tm_in=2048, tseq=512, ng=4, tq=512, tm_out=512, bm=512, tn=1024, tt=256):
    batch, seq, d = x.shape
    n = batch * seq
    rw = 3 * RWKV_WIDTH
    rcols = rw + DECAY_LORA + AAA_LORA + GATE_LORA
    row = lambda a: a.reshape(1, -1)

    xf = x.reshape(n, d)
    xb = xf.astype(BF16)
    w_r = jnp.concatenate([w_in[:, :rw], _lora_layout(w_in[:, rw:rcols])], axis=1).astype(BF16)
    w_d = w_in[:, rcols:].astype(BF16)
    mu = jnp.concatenate([row(shift_mu)[:, :rw], _lora_layout(row(shift_mu)[:, rw:])], axis=1)
    p_r = _matmul(xb, w_r, F32, tm_in, 512)
    p_d = _matmul(xb, w_d, BF16, tm_in, 512)

    pad_rows = lambda a, rows: jnp.pad(a, ((0, rows - a.shape[0]), (0, 0))).astype(BF16)
    h_r = _rwkv(p_r, mu, row(w0), row(a0), row(k_k), row(k_a), row(r_k), row(gn_g), row(gn_b),
                pad_rows(w_up, LANES), pad_rows(a_up, LANES), pad_rows(g_up, 2 * LANES), batch, seq, tseq, ng)
    h_d = _diff_attention(p_d, row(lq1), row(lk1), row(lq2), row(lk2), row(subln_g), lambda_init,
                          batch, seq, tq)

    w_ob = w_out.astype(BF16)
    w_rp = _pad_cols(w_router, LANES)
    b_rp = jnp.concatenate([row(b_router), jnp.full((1, LANES - N_EXPERTS), NEG_BIG, F32)], axis=1)
    x1, x1p2, idx, gates, counts = _outproj_ln_router(
        h_r, h_d, xf, w_ob[:RWKV_WIDTH], w_ob[RWKV_WIDTH:], row(ln1_g), row(ln1_b), w_rp, b_rp, tm_out)
    out = _moe_ffn(x1, x1p2, idx, gates, counts, w_gu, b_gu, w_dn, b_dn, row(ln2_g), row(ln2_b), bm, tn, tt)
    return out.reshape(batch, seq, d)


def kernel(x, w_in, shift_mu, w0, w_up, a0, a_up, g_up, k_k, k_a, r_k, gn_g, gn_b, lq1, lk1, lq2, lk2,
           subln_g, w_out, ln1_g, ln1_b, w_router, b_router, w_gu, b_gu, w_dn, b_dn, ln2_g, ln2_b):
    for l in range(DEPTH):
        lambda_init = 0.8 - 0.6 * math.exp(-0.3 * l)
        x = _layer(x, w_in[l], shift_mu[l], w0[l], w_up[l], a0[l], a_up[l], g_up[l], k_k[l], k_a[l],
                   r_k[l], gn_g[l], gn_b[l], lq1[l], lk1[l], lq2[l], lk2[l], subln_g[l], w_out[l],
                   ln1_g[l], ln1_b[l], w_router[l], b_router[l], w_gu[l], b_gu[l], w_dn[l], b_dn[l],
                   ln2_g[l], ln2_b[l], lambda_init)
    return x
```

```python
import functools
import math

import jax
import jax.numpy as jnp
from jax import lax
from jax.experimental import pallas as pl
from jax.experimental.pallas import tpu as pltpu

F32 = jnp.float32
BF16 = jnp.bfloat16

D_MODEL = 2048
RWKV_HEAD = 64
RWKV_WIDTH = 1024
RWKV_HEADS = 16
DECAY_LORA = 64
AAA_LORA = 64
GATE_LORA = 160
DIFF_HEAD = 64
DIFF_VDIM = 128
DIFF_HEADS = 8
DIFF_WIDTH = 1024
N_EXPERTS = 32
TOP_K = 4
D_FF = 2048
SWIGLU_LIMIT = 7.0
SWIGLU_ALPHA = 1.702
LN_EPS = 1e-5
GN_EPS = RWKV_HEAD * 1e-5
RMS_EPS = 1e-5
NEG_BIG = -1e30
DEPTH = 1
DEEPNORM_ALPHA = (2.0 * DEPTH) ** 0.25

LANES = 128
SUBLANES = 8
ROW_WORDS = D_MODEL // 2
ROW_TILE = ROW_WORDS // LANES
RWKV_GROUP = 256
RWKV_CHUNK = 64
LORA_COLS = 512
VMEM_LIMIT = 56 * 1024 * 1024
HIGH_HALF = -65536
MXU_DEPTH = 256
OUTPROJ_PARTS = 2
MOE_ROW_PIECES = 4
COMBINE_SUBBLOCKS = 8


def _cparams(sem, vmem=None):
    return pltpu.CompilerParams(dimension_semantics=sem, vmem_limit_bytes=vmem)


def _dot(a, b):
    return jnp.dot(a, b, preferred_element_type=F32)


def _dot_nt(a, b):
    return lax.dot_general(a, b, (((1,), (1,)), ((), ())), preferred_element_type=F32)


def _dot_tn(a, b):
    return lax.dot_general(a, b, (((0,), (0,)), ((), ())), preferred_element_type=F32)


def _split3(x):
    h = x.astype(BF16)
    r = x - h.astype(F32)
    m = r.astype(BF16)
    l = (r - m.astype(F32)).astype(BF16)
    return h, m, l


def _dot_exact_rhs(x, ones):
    h, m, l = _split3(x)
    return _dot(h, ones) + _dot(m, ones) + _dot(l, ones)


def _dot_exact_lhs(ones, x):
    h, m, l = _split3(x)
    return _dot(ones, h) + _dot(ones, m) + _dot(ones, l)


def _sigmoid(x):
    return 1.0 / (1.0 + jnp.exp(-x))


def _matmul_kernel(x_ref, w_ref, o_ref):
    o_ref[...] = _dot(x_ref[...], w_ref[...]).astype(o_ref.dtype)


def _matmul(x, w, out_dtype, tm, tn):
    m, k = x.shape
    n = w.shape[1]
    return pl.pallas_call(
        _matmul_kernel,
        grid=(n // tn, m // tm),
        in_specs=[pl.BlockSpec((tm, k), lambda j, i: (i, 0)),
                  pl.BlockSpec((k, tn), lambda j, i: (0, j))],
        out_specs=pl.BlockSpec((tm, tn), lambda j, i: (i, j)),
        out_shape=jax.ShapeDtypeStruct((m, n), out_dtype),
        compiler_params=_cparams(("parallel", "parallel"), VMEM_LIMIT),
        name="in_proj",
    )(x, w)


def _rwkv_kernel(r_ref, k_ref, v_ref, l_ref, mur_ref, muk_ref, muv_ref, mul_ref,
                 w0_ref, a0_ref, kk_ref, ka_ref, rk_ref, gng_ref, gnb_ref,
                 wup_ref, aup_ref, gup_ref, o_ref,
                 pr_s, pk_s, pv_s, pl_s, state_s, r_s, w_s, k_s, v_s, a_s, b_s, g_s,
                 y_s, rc_p, lrb_p, lrk_p, tb_p, wc_p, akv_p, be_p, ke_p, dec_p):
    s = pl.program_id(2)
    T = r_ref.shape[0]
    G = RWKV_GROUP
    C = RWKV_CHUNK
    NG = r_ref.shape[1] // G
    groups = [slice(g * G, (g + 1) * G) for g in range(NG)]

    @pl.when(s == 0)
    def _():
        state_s[...] = jnp.zeros_like(state_s)
        pr_s[...] = jnp.zeros_like(pr_s)
        pk_s[...] = jnp.zeros_like(pk_s)
        pv_s[...] = jnp.zeros_like(pv_s)
        pl_s[...] = jnp.zeros_like(pl_s)

    row = lax.broadcasted_iota(jnp.int32, (T, 1), 0)

    def shift(ref, prev_s, mu_ref):
        p = ref[...]
        prev = jnp.where(row == 0, prev_s[...], pltpu.roll(p, 1, 0))
        prev_s[...] = p[T - 1:T, :]
        return p + (prev - p) * mu_ref[...]

    r = shift(r_ref, pr_s, mur_ref)
    k = shift(k_ref, pk_s, muk_ref)
    v = shift(v_ref, pv_s, muv_ref)
    lo = shift(l_ref, pl_s, mul_ref)
    dw = lo[:, 0:LANES]
    da = lo[:, LANES:2 * LANES]
    dg = lo[:, 2 * LANES:4 * LANES]

    wpre = w0_ref[...] + _dot(jnp.tanh(dw).astype(BF16), wup_ref[...])
    wlog = -math.exp(-0.5) * _sigmoid(wpre)
    a_sig = _sigmoid(a0_ref[...] + _dot(da.astype(BF16), aup_ref[...]))
    gate = _dot(_sigmoid(dg).astype(BF16), gup_ref[...])

    gi = lax.broadcasted_iota(jnp.int32, (G, G), 0)
    gj = lax.broadcasted_iota(jnp.int32, (G, G), 1)
    same_head = (gi // RWKV_HEAD) == (gj // RWKV_HEAD)
    head_ones = jnp.where(same_head, 1.0, 0.0).astype(BF16)

    def head_sum(x):
        return jnp.concatenate([_dot_exact_rhs(x[:, g], head_ones) for g in groups], axis=1)

    kk = k * kk_ref[...]
    nrm = jnp.sqrt(head_sum(kk * kk))
    kk = kk / jnp.maximum(nrm, 1e-12)
    k2 = k * (1.0 + (a_sig - 1.0) * ka_ref[...])

    r_s[...] = r
    w_s[...] = wlog
    k_s[...] = k2
    v_s[...] = v
    a_s[...] = -kk
    b_s[...] = kk * a_sig
    g_s[...] = gate

    ci = lax.broadcasted_iota(jnp.int32, (C, C), 0)
    cj = lax.broadcasted_iota(jnp.int32, (C, C), 1)
    tri = jnp.where(cj <= ci, 1.0, 0.0).astype(BF16)
    mt = lax.broadcasted_iota(jnp.int32, (C, G), 0)
    mtp = lax.broadcasted_iota(jnp.int32, (C, G), 1) % C
    strict = mtp < mt
    incl = mtp <= mt
    ceye = jnp.where(mtp == mt, 1.0, 0.0)

    def bd(xc):
        return jnp.where(same_head, jnp.concatenate([xc, xc, xc, xc], axis=0), jnp.zeros((), BF16))

    def each(fn, *lists):
        return [fn(*xs) for xs in zip(*lists)]

    def bf(x):
        return x.astype(BF16)

    def prepare(cp, carry):
        streams = [(pl.ds(pl.multiple_of((2 * cp + h) * C, C), C), g) for h in range(2) for g in groups]
        rows = [2 * cp + h for h in range(2) for _ in groups]
        rc = [r_s[sl, g] for sl, g in streams]
        wc = [w_s[sl, g] for sl, g in streams]
        kc = [k_s[sl, g] for sl, g in streams]
        vc = [v_s[sl, g] for sl, g in streams]
        ac = [a_s[sl, g] for sl, g in streams]
        bc = [b_s[sl, g] for sl, g in streams]
        cum = each(lambda w: _dot_exact_lhs(tri, w), wc)
        tot = each(lambda x: x[C - 1:C, :], cum)
        ginv = each(lambda x: jnp.exp(-x), cum)
        gend = each(lambda x, t: jnp.exp(t - x), cum, tot)
        r_c = each(lambda x, g: bf(x * jnp.exp(g)), rc, cum)
        a_c = each(lambda x, g, w: bf(x * jnp.exp(g - w)), ac, cum, wc)
        v_c = each(bf, vc)
        k_bd = each(lambda x, g: bd(bf(x * g)), kc, ginv)
        b_bd = each(lambda x, g: bd(bf(x * g)), bc, ginv)
        ke_c = each(lambda x, g: bf(x * g), kc, gend)
        be_c = each(lambda x, g: bf(x * g), bc, gend)
        v_bd = each(bd, v_c)
        a_bd = each(bd, a_c)

        ar = each(lambda a, r: jnp.concatenate([a, r], axis=0), a_c, r_c)
        arb = each(_dot_nt, ar, b_bd)
        ark = each(_dot_nt, ar, k_bd)
        l_ab = each(lambda x: jnp.where(strict, x[:C], 0.0), arb)
        l_ak = each(lambda x: bf(jnp.where(strict, x[:C], 0.0)), ark)
        l_rb = each(lambda x: bf(jnp.where(incl, x[C:], 0.0)), arb)
        l_rk = each(lambda x: bf(jnp.where(incl, x[C:], 0.0)), ark)

        p = each(bf, l_ab)
        p_bd = each(bd, p)
        tinv = each(lambda x: ceye + x, l_ab)
        for _ in range(int(math.log2(C)) - 1):
            p = each(lambda x, y: bf(_dot(x, y)), p, p_bd)
            p_bd = each(bd, p)
            tinv = each(lambda t, y: t + _dot(bf(t), y), tinv, p_bd)
        tb = each(bf, tinv)
        akv_c = each(lambda a, b: bf(_dot(a, b)), l_ak, v_bd)
        w_c = each(lambda a, b: bf(_dot(a, b)), tb, a_bd)
        for ref, vals in zip((rc_p, lrb_p, lrk_p, tb_p, wc_p, akv_p, be_p, ke_p),
                             (r_c, l_rb, l_rk, tb, w_c, akv_c, be_c, ke_c)):
            for (sl, g), val in zip(streams, vals):
                ref[sl, g] = val
        for (_, g), row, t in zip(streams, rows, tot):
            dec_p[pl.ds(row, 1), g] = jnp.exp(t)
        return carry

    lax.fori_loop(0, T // C // 2, prepare, 0)

    def chunk(c, carry):
        sl = pl.ds(pl.multiple_of(c * C, C), C)
        load = lambda ref: [ref[sl, g] for g in groups]
        r_c, l_rb, l_rk, tb, w_c, be_c, ke_c = (load(ref) for ref in
                                                (rc_p, lrb_p, lrk_p, tb_p, wc_p, be_p, ke_p))
        akv_bd = each(bd, load(akv_p))
        v_c = each(bf, load(v_s))
        v_bd = each(bd, v_c)
        dec = [dec_p[pl.ds(c, 1), g] for g in groups]
        st = [state_s[g] for g in range(NG)]
        stb = each(bf, st)
        u_c = each(lambda w, s0, t, x: bf(_dot_nt(w, s0) + _dot(t, x)), w_c, stb, tb, akv_bd)
        u_bd = each(bd, u_c)
        y = each(lambda r, s0, lb, lk, u, vv: _dot_nt(r, s0) + _dot(jnp.concatenate([lb, lk], axis=1),
                                                                    jnp.concatenate([u, vv], axis=0)),
                 r_c, stb, l_rb, l_rk, u_bd, v_bd)
        new = each(lambda s0, t, u, vv, b, kx: s0 * t + jnp.where(
            same_head, _dot_tn(jnp.concatenate([u, vv], axis=0), jnp.concatenate([b, kx], axis=0)), 0.0),
            st, dec, u_c, v_c, be_c, ke_c)
        for g in range(NG):
            state_s[g] = new[g]
        for g, yy in zip(groups, y):
            y_s[sl, g] = yy
        return carry

    lax.fori_loop(0, T // C, chunk, 0)

    def finish(cp, carry):
        streams = [(pl.ds(pl.multiple_of((2 * cp + h) * C, C), C), g) for h in range(2) for g in groups]
        y = [y_s[sl, g] for sl, g in streams]
        mean = each(lambda x: _dot_exact_rhs(x, head_ones) * (1.0 / RWKV_HEAD), y)
        d = each(lambda x, m: x - m, y, mean)
        var = each(lambda x: _dot_exact_rhs(x * x, head_ones) * (1.0 / RWKV_HEAD), d)
        bonus = [_dot_exact_rhs(r_s[sl, g] * k_s[sl, g] * rk_ref[:, g], head_ones) * v_s[sl, g]
                 for sl, g in streams]
        for (sl, g), dd, vr, bo in zip(streams, d, var, bonus):
            yn = dd * lax.rsqrt(vr + GN_EPS) * gng_ref[:, g] + gnb_ref[:, g]
            o_ref[sl, g] = ((yn + bo) * g_s[sl, g]).astype(o_ref.dtype)
        return carry

    lax.fori_loop(0, T // C // 2, finish, 0)


def _rwkv(p_r, mu, w0, a0, k_k, k_a, r_k, gn_g, gn_b, w_up, a_up, g_up, batch, seq, tseq, ng):
    n = batch * seq
    G = ng * RWKV_GROUP
    nq = RWKV_WIDTH // G
    ns = seq // tseq
    lora_blk = 3 * RWKV_WIDTH // LORA_COLS

    def tok(off):
        return pl.BlockSpec((tseq, G), lambda b, q, s: (b * ns + s, off + q))

    def par(off):
        return pl.BlockSpec((1, G), lambda b, q, s: (0, off + q))

    in_specs = [
        tok(0), tok(nq), tok(2 * nq),
        pl.BlockSpec((tseq, LORA_COLS), lambda b, q, s: (b * ns + s, lora_blk)),
        par(0), par(nq), par(2 * nq),
        pl.BlockSpec((1, LORA_COLS), lambda b, q, s: (0, lora_blk)),
        par(0), par(0), par(0), par(0), par(0), par(0), par(0),
        pl.BlockSpec((LANES, G), lambda b, q, s: (0, q)),
        pl.BlockSpec((LANES, G), lambda b, q, s: (0, q)),
        pl.BlockSpec((2 * LANES, G), lambda b, q, s: (0, q)),
    ]
    scratch = [pltpu.VMEM((1, G), F32), pltpu.VMEM((1, G), F32), pltpu.VMEM((1, G), F32),
               pltpu.VMEM((1, LORA_COLS), F32), pltpu.VMEM((ng, RWKV_GROUP, RWKV_GROUP), F32)]
    scratch += [pltpu.VMEM((tseq, G), F32) for _ in range(8)]
    scratch += [pltpu.VMEM((tseq, G), BF16) for _ in range(8)]
    scratch += [pltpu.VMEM((tseq // RWKV_CHUNK, G), F32)]
    return pl.pallas_call(
        _rwkv_kernel,
        grid=(batch, nq, ns),
        in_specs=in_specs,
        out_specs=pl.BlockSpec((tseq, G), lambda b, q, s: (b * ns + s, q)),
        out_shape=jax.ShapeDtypeStruct((n, RWKV_WIDTH), BF16),
        scratch_shapes=scratch,
        compiler_params=_cparams(("parallel", "parallel", "arbitrary"), VMEM_LIMIT),
        name="rwkv7",
    )(p_r, p_r, p_r, p_r, mu, mu, mu, mu, w0, a0, k_k, k_a, r_k, gn_g, gn_b, w_up, a_up, g_up)


def _attn_kernel(q_ref, k_ref, v_ref, lq1_ref, lk1_ref, lq2_ref, lk2_ref, g_ref, o_ref, vt_s, *, lambda_init):
    i = pl.program_id(2)
    tq = q_ref.shape[0]

    @pl.when(i == 0)
    def _():
        for j in range(vt_s.shape[0]):
            vt_s[j] = v_ref[j * tq:(j + 1) * tq, :].astype(F32).T.astype(BF16)

    drow = lax.broadcasted_iota(jnp.int32, (DIFF_VDIM, 1), 0)
    qt = (q_ref[...].astype(F32) * (DIFF_HEAD ** -0.5)).T
    q1 = jnp.where(drow < DIFF_HEAD, qt, 0.0).astype(BF16)
    q2 = jnp.where(drow >= DIFF_HEAD, qt, 0.0).astype(BF16)
    lam = (jnp.exp(jnp.sum(lq1_ref[...] * lk1_ref[...], axis=-1, keepdims=True))
           - jnp.exp(jnp.sum(lq2_ref[...] * lk2_ref[...], axis=-1, keepdims=True)) + lambda_init)

    key = lax.broadcasted_iota(jnp.int32, (tq, tq), 0)
    qry = lax.broadcasted_iota(jnp.int32, (tq, tq), 1)

    def scores(j, diagonal):
        kj = k_ref[j * tq:(j + 1) * tq, :]
        sc = (_dot(kj, q1), _dot(kj, q2))
        if diagonal:
            sc = tuple(jnp.where(key <= qry, s, NEG_BIG) for s in sc)
        return sc

    def absorb(j, sc, stats):
        m, l, acc = stats[0:2], stats[2:4], stats[4:6]
        vtj = vt_s[j]
        m_new = [jnp.maximum(mm, jnp.max(s, axis=0, keepdims=True)) for mm, s in zip(m, sc)]
        alpha = [jnp.exp(mm - mn) for mm, mn in zip(m, m_new)]
        p = [jnp.exp(s - mn) for s, mn in zip(sc, m_new)]
        l = [a * ll + jnp.sum(pp, axis=0, keepdims=True) for a, ll, pp in zip(alpha, l, p)]
        pv = [_dot(vtj, pp.astype(BF16)) for pp in p]
        acc = [a * ac + x for a, ac, x in zip(alpha, acc, pv)]
        return tuple(m_new) + tuple(l) + tuple(acc)

    zero1 = jnp.zeros((1, tq), F32)
    neg1 = jnp.full((1, tq), NEG_BIG, F32)
    zacc = jnp.zeros((DIFF_VDIM, tq), F32)
    init = (neg1, neg1, zero1, zero1, zacc, zacc)

    for iv in range(vt_s.shape[0]):
        @pl.when(i == iv)
        def _(iv=iv):
            stats = init
            sc = scores(0, iv == 0)
            for j in range(iv + 1):
                nxt = scores(j + 1, j + 1 == iv) if j < iv else None
                stats = absorb(j, sc, stats)
                sc = nxt
            l, acc = stats[2:4], stats[4:6]
            o = acc[0] / l[0] - lam * (acc[1] / l[1])
            o = o * lax.rsqrt(jnp.mean(o * o, axis=0, keepdims=True) + RMS_EPS) * g_ref[...]
            o_ref[...] = (o * (1.0 - lambda_init)).T.astype(o_ref.dtype)


def _diff_attention(p_d, lq1, lk1, lq2, lk2, subln_g, lambda_init, batch, seq, tq):
    n = batch * seq
    nq = seq // tq
    H = DIFF_HEADS
    small = pl.BlockSpec((1, DIFF_HEAD), lambda b, h, i: (0, 0))
    return pl.pallas_call(
        functools.partial(_attn_kernel, lambda_init=lambda_init),
        grid=(batch, H, nq),
        in_specs=[pl.BlockSpec((tq, DIFF_VDIM), lambda b, h, i: (b * nq + i, h)),
                  pl.BlockSpec((seq, DIFF_VDIM), lambda b, h, i: (b, H + h)),
                  pl.BlockSpec((seq, DIFF_VDIM), lambda b, h, i: (b, 2 * H + h)),
                  small, small, small, small,
                  pl.BlockSpec((DIFF_VDIM, 1), lambda b, h, i: (0, 0))],
        out_specs=pl.BlockSpec((tq, DIFF_VDIM), lambda b, h, i: (b * nq + i, h)),
        out_shape=jax.ShapeDtypeStruct((n, DIFF_WIDTH), BF16),
        scratch_shapes=[pltpu.VMEM((nq, DIFF_VDIM, tq), BF16)],
        compiler_params=_cparams(("parallel", "parallel", "arbitrary"), VMEM_LIMIT),
        name="diff_attn",
    )(p_d, p_d, p_d, lq1, lk1, lq2, lk2, subln_g.reshape(DIFF_VDIM, 1))


def _layer_norm(y, g, b):
    mu = jnp.mean(y, axis=-1, keepdims=True)
    d = y - mu
    var = jnp.mean(d * d, axis=-1, keepdims=True)
    return d * lax.rsqrt(var + LN_EPS) * g + b


def _store_row_tiles(ref, words, start=0, sublane=0):
    rows = words.shape[0]
    for s in range(words.shape[1] // LANES):
        ref[pl.ds(start * ROW_TILE + sublane + s, rows, stride=ROW_TILE), :] = words[:, s * LANES:(s + 1) * LANES]


def _load_row_tiles(ref, start, rows):
    parts = [ref[pl.ds(start * ROW_TILE + s, rows, stride=ROW_TILE), :] for s in range(ROW_TILE)]
    return jnp.concatenate(parts, axis=1)


def _pack_pair(lo, hi):
    bits = lambda y: lax.bitcast_convert_type(y.astype(BF16).astype(F32), jnp.int32)
    return lax.shift_right_logical(bits(lo), 16) | (bits(hi) & HIGH_HALF)


def _pack_row(y):
    return _pack_pair(y[:, :ROW_WORDS], y[:, ROW_WORDS:])


def _unpack_row(words):
    lo = lax.bitcast_convert_type(lax.shift_left(words, 16), F32)
    hi = lax.bitcast_convert_type(words & HIGH_HALF, F32)
    return lo, hi


def _expert_onehots(idx):
    lane = lax.broadcasted_iota(jnp.int32, idx.shape, 1)
    sels = [lane == idx[:, kk:kk + 1] for kk in range(TOP_K)]
    onehot = jnp.zeros(idx.shape, F32)
    for sel in sels:
        onehot = onehot + jnp.where(sel, 1.0, 0.0)
    return sels, onehot


def _outproj_kernel(hr_ref, hd_ref, x_ref, wt_ref, wb_ref, g_ref, b_ref, wrh_ref, wrl_ref, br_ref,
                    x1_ref, x1p_ref, idx_ref, gate_ref, cnt_ref, carry_s):
    i = pl.program_id(0)
    tm = x_ref.shape[0]
    th = tm // OUTPROJ_PARTS
    parts = [pl.ds(h * th, th) for h in range(OUTPROJ_PARTS)]

    @pl.when(i == 0)
    def _():
        carry_s[...] = jnp.zeros_like(carry_s)

    def each(fn, *lists):
        return [fn(*xs) for xs in zip(*lists)]

    mix = [_dot(hr_ref[p, :], wt_ref[...]) + _dot(hd_ref[p, :], wb_ref[...]) for p in parts]
    x1 = [_layer_norm(DEEPNORM_ALPHA * x_ref[p, :] + m, g_ref[...], b_ref[...]) for p, m in zip(parts, mix)]
    for h, (p, y) in enumerate(zip(parts, x1)):
        x1_ref[p, :] = y
        _store_row_tiles(x1p_ref, _pack_row(y), h * th)

    xh = each(lambda y: y.astype(BF16), x1)
    xl = each(lambda y, hh: (y - hh.astype(F32)).astype(BF16), x1, xh)
    work = each(lambda hh, ll: _dot(hh, wrh_ref[...]) + _dot(ll, wrh_ref[...]) + _dot(hh, wrl_ref[...])
                + br_ref[...], xh, xl)
    lane = lax.broadcasted_iota(jnp.int32, (th, LANES), 1).astype(F32)
    onehot = [jnp.zeros((th, LANES), F32) for _ in parts]
    vals, idxs = [], []
    for _ in range(TOP_K):
        mx = each(lambda w: jnp.max(w, axis=-1, keepdims=True), work)
        idx = each(lambda w, m: jnp.min(jnp.where(w == m, lane, float(LANES)), axis=-1, keepdims=True), work, mx)
        sel = each(lambda ix: lane == ix, idx)
        work = each(lambda s, w: jnp.where(s, -jnp.inf, w), sel, work)
        onehot = each(lambda o, s: o + jnp.where(s, 1.0, 0.0), onehot, sel)
        vals.append(mx)
        idxs.append(idx)
    total = carry_s[0:1, :]
    for h, p in enumerate(parts):
        exps = [jnp.exp(vv[h] - vals[0][h]) for vv in vals]
        den = exps[0] + exps[1] + exps[2] + exps[3]
        idx_out = jnp.zeros((th, LANES), F32)
        gate_out = jnp.zeros((th, LANES), F32)
        for kk in range(TOP_K):
            slot = lane == float(kk)
            idx_out = jnp.where(slot, idxs[kk][h], idx_out)
            gate_out = jnp.where(slot, exps[kk] / den, gate_out)
        idx_ref[p, :] = idx_out.astype(jnp.int32)
        gate_ref[p, :] = gate_out
        total = total + jnp.sum(onehot[h], axis=0, keepdims=True)
    carry_s[...] = jnp.broadcast_to(total, carry_s.shape)
    cnt_ref[...] = jnp.broadcast_to(total, cnt_ref.shape)


def _outproj_ln_router(hr, hd, x, w_top, w_bot, g, b, w_r, b_r, tm):
    n = x.shape[0]
    w_rh = w_r.astype(BF16)
    w_rl = (w_r - w_rh.astype(F32)).astype(BF16)
    const = lambda shape: pl.BlockSpec(shape, lambda i: (0, 0))
    rowb = lambda cols: pl.BlockSpec((tm, cols), lambda i: (i, 0))
    out_shape = (jax.ShapeDtypeStruct((n, D_MODEL), F32),
                 jax.ShapeDtypeStruct((n * ROW_TILE, LANES), jnp.int32),
                 jax.ShapeDtypeStruct((n, LANES), jnp.int32),
                 jax.ShapeDtypeStruct((n, LANES), F32),
                 jax.ShapeDtypeStruct((SUBLANES, LANES), F32))
    return pl.pallas_call(
        _outproj_kernel,
        grid=(n // tm,),
        in_specs=[rowb(RWKV_WIDTH), rowb(DIFF_WIDTH), rowb(D_MODEL),
                  const((RWKV_WIDTH, D_MODEL)), const((DIFF_WIDTH, D_MODEL)),
                  const((1, D_MODEL)), const((1, D_MODEL)),
                  const((D_MODEL, LANES)), const((D_MODEL, LANES)), const((1, LANES))],
        out_specs=(rowb(D_MODEL), pl.BlockSpec((tm * ROW_TILE, LANES), lambda i: (i, 0)),
                   rowb(LANES), rowb(LANES), const((SUBLANES, LANES))),
        out_shape=out_shape,
        scratch_shapes=[pltpu.VMEM((SUBLANES, LANES), F32)],
        compiler_params=_cparams(("arbitrary",), VMEM_LIMIT),
        name="outproj_ln_router",
    )(hr, hd, x, w_top, w_bot, g, b, w_rh, w_rl, b_r)


def _dest_kernel(idx_ref, ps_ref, dest_ref, carry_s):
    i = pl.program_id(0)
    tm = idx_ref.shape[0]

    @pl.when(i == 0)
    def _():
        carry_s[...] = jnp.broadcast_to(ps_ref[...], carry_s.shape)

    sels, onehot = _expert_onehots(idx_ref[...])
    ti = lax.broadcasted_iota(jnp.int32, (tm, tm), 0)
    tj = lax.broadcasted_iota(jnp.int32, (tm, tm), 1)
    before = jnp.where(tj < ti, 1.0, 0.0).astype(BF16)
    cum = _dot(before, onehot.astype(BF16)) + carry_s[0:1, :]
    lane = lax.broadcasted_iota(jnp.int32, (tm, LANES), 1)
    dest = jnp.zeros((tm, LANES), F32)
    for kk in range(TOP_K):
        dk = jnp.sum(jnp.where(sels[kk], cum, 0.0), axis=-1, keepdims=True)
        dest = jnp.where(lane == kk, dk, dest)
    dest_ref[...] = dest.astype(jnp.int32)
    total = carry_s[0:1, :] + jnp.sum(onehot, axis=0, keepdims=True)
    carry_s[...] = jnp.broadcast_to(total, carry_s.shape)


def _row_dest(idx, pstarts, tm):
    n = idx.shape[0]
    return pl.pallas_call(
        _dest_kernel,
        grid=(n // tm,),
        in_specs=[pl.BlockSpec((tm, LANES), lambda i: (i, 0)), pl.BlockSpec((1, LANES), lambda i: (0, 0))],
        out_specs=pl.BlockSpec((tm, LANES), lambda i: (i, 0)),
        out_shape=jax.ShapeDtypeStruct((n, LANES), jnp.int32),
        scratch_shapes=[pltpu.VMEM((SUBLANES, LANES), F32)],
        compiler_params=_cparams(("arbitrary",)),
        name="moe_row_dest",
    )(idx, pstarts)


def _wait_rows(ref, rows, sem):
    view = ref.at[pl.ds(0, rows)]
    pltpu.make_async_copy(view, view, sem).wait()


def _dispatch_kernel(dest_ref, padoff_ref, padlen_ref, x_ref, xs_hbm, zero_s, sem, zsem):
    tt = x_ref.shape[0] // ROW_TILE

    def pad_copies(e, go):
        off = padoff_ref[e]
        ln = padlen_ref[e]
        for bit in range(zero_s.shape[0].bit_length()):
            size = 1 << bit

            @pl.when((ln >> bit) & 1 == 1)
            def _():
                go(pltpu.make_async_copy(zero_s.at[pl.ds(0, size)],
                                         xs_hbm.at[pl.ds(off + (ln & (size - 1)), size)], zsem))

    def tail_copy(piece):
        rows = zero_s.shape[0]
        first = pl.multiple_of(padoff_ref[N_EXPERTS] + piece * rows, rows)
        return pltpu.make_async_copy(zero_s, xs_hbm.at[pl.ds(first, rows)], zsem)

    @pl.when(pl.program_id(0) == 0)
    def _():
        zero_s[...] = jnp.zeros_like(zero_s)
        for go in (lambda cp: cp.start(), lambda cp: cp.wait()):
            def body(e, c, go=go):
                pad_copies(e, go)
                return c
            lax.fori_loop(0, N_EXPERTS, body, 0)

            def tail(piece, c, go=go):
                go(tail_copy(piece))
                return c
            lax.fori_loop(0, padlen_ref[N_EXPERTS], tail, 0)

    def issue(t, c):
        src = x_ref.at[pl.ds(pl.multiple_of(t * ROW_TILE, ROW_TILE), ROW_TILE), :]
        for kk in range(TOP_K):
            pltpu.make_async_copy(src, xs_hbm.at[dest_ref[t * TOP_K + kk]], sem).start(priority=kk % 2)
        return c

    lax.fori_loop(0, tt, issue, 0)
    for _ in range(TOP_K):
        _wait_rows(xs_hbm, tt, sem)


def _dispatch(dest_flat, pad_off, pad_len, x1p2, n_rows, tt, bm):
    n = x1p2.shape[0] // ROW_TILE
    table = pl.BlockSpec((LANES,), lambda i: (0,), memory_space=pltpu.SMEM)
    return pl.pallas_call(
        _dispatch_kernel,
        grid=(n // tt,),
        in_specs=[pl.BlockSpec((tt * TOP_K,), lambda i: (i,), memory_space=pltpu.SMEM), table, table,
                  pl.BlockSpec((tt * ROW_TILE, LANES), lambda i: (i, 0))],
        out_specs=pl.BlockSpec(memory_space=pl.ANY),
        out_shape=jax.ShapeDtypeStruct((n_rows, ROW_TILE, LANES), jnp.int32),
        scratch_shapes=[pltpu.VMEM((bm // 2, ROW_TILE, LANES), jnp.int32),
                        pltpu.SemaphoreType.DMA, pltpu.SemaphoreType.DMA],
        compiler_params=_cparams(("arbitrary",)),
        name="moe_dispatch",
    )(dest_flat, pad_off, pad_len, x1p2)


def _dot_f32_weights(xk, w_ref, cols):
    acc = None
    for kc, x in enumerate(xk):
        part = _dot(x, w_ref[0, kc * MXU_DEPTH:(kc + 1) * MXU_DEPTH, cols].astype(BF16))
        acc = part if acc is None else acc + part
    return acc


def _for_valid_rows(nr, bm, body, o_ref):
    piece = bm // MOE_ROW_PIECES
    out_rows_per_row = o_ref.shape[0] // bm
    for q in range(1, MOE_ROW_PIECES + 1):
        @pl.when(jnp.logical_and(nr > (q - 1) * piece, nr <= q * piece))
        def _(rows=q * piece):
            body(rows)
            if rows < bm:
                o_ref[rows * out_rows_per_row:, :] = jnp.zeros(
                    ((bm - rows) * out_rows_per_row, o_ref.shape[1]), o_ref.dtype)

    @pl.when(nr == 0)
    def _():
        o_ref[...] = jnp.zeros_like(o_ref)


def _gu_kernel(be_ref, rows_ref, xs_ref, wg_ref, wu_ref, bg_ref, bu_ref, o_ref):
    del be_ref

    def body(rows):
        lo, hi = _unpack_row(_load_row_tiles(xs_ref, 0, rows))
        x = jnp.concatenate([lo, hi], axis=1).astype(BF16)
        xk = [x[:, kc * MXU_DEPTH:(kc + 1) * MXU_DEPTH] for kc in range(D_MODEL // MXU_DEPTH)]
        g = _dot_f32_weights(xk, wg_ref, slice(None)) + bg_ref[0]
        u = _dot_f32_weights(xk, wu_ref, slice(None)) + bu_ref[0]
        g = jnp.minimum(g, SWIGLU_LIMIT)
        u = jnp.clip(u, -SWIGLU_LIMIT, SWIGLU_LIMIT)
        o_ref[0:rows, :] = ((u + 1.0) * (g * _sigmoid(SWIGLU_ALPHA * g))).astype(o_ref.dtype)

    _for_valid_rows(rows_ref[pl.program_id(1)], o_ref.shape[0], body, o_ref)


def _moe_gate_up(block_e, blk_rows, xs2, w_gu, b_gu, bm, tn):
    n_rows = xs2.shape[0] // ROW_TILE
    n_blocks = n_rows // bm
    nt = D_FF // tn
    grid_spec = pltpu.PrefetchScalarGridSpec(
        num_scalar_prefetch=2,
        grid=(nt, n_blocks),
        in_specs=[pl.BlockSpec((bm * ROW_TILE, LANES), lambda n, j, be, nv: (j, 0)),
                  pl.BlockSpec((1, D_MODEL, tn), lambda n, j, be, nv: (be[j], 0, n)),
                  pl.BlockSpec((1, D_MODEL, tn), lambda n, j, be, nv: (be[j], 0, nt + n)),
                  pl.BlockSpec((1, 1, tn), lambda n, j, be, nv: (be[j], 0, n)),
                  pl.BlockSpec((1, 1, tn), lambda n, j, be, nv: (be[j], 0, nt + n))],
        out_specs=pl.BlockSpec((bm, tn), lambda n, j, be, nv: (j, n)))
    return pl.pallas_call(
        _gu_kernel,
        grid_spec=grid_spec,
        out_shape=jax.ShapeDtypeStruct((n_rows, D_FF), BF16),
        compiler_params=_cparams(("arbitrary", "arbitrary"), VMEM_LIMIT),
        name="moe_gate_up",
    )(block_e, blk_rows, xs2, w_gu, w_gu, b_gu, b_gu)


def _dn_kernel(be_ref, rows_ref, a_ref, w_ref, b_ref, o_ref):
    del be_ref

    def body(rows):
        ak = [a_ref[0:rows, kc * MXU_DEPTH:(kc + 1) * MXU_DEPTH] for kc in range(D_FF // MXU_DEPTH)]
        hw = ROW_WORDS // 2
        for h in range(2):
            lo = slice(h * hw, (h + 1) * hw)
            hi = slice(ROW_WORDS + h * hw, ROW_WORDS + (h + 1) * hw)
            y_lo = _dot_f32_weights(ak, w_ref, lo) + b_ref[0, :, lo]
            y_hi = _dot_f32_weights(ak, w_ref, hi) + b_ref[0, :, hi]
            _store_row_tiles(o_ref, _pack_pair(y_lo, y_hi), 0, h * (hw // LANES))

    _for_valid_rows(rows_ref[pl.program_id(0)], a_ref.shape[0], body, o_ref)


def _moe_down(block_e, blk_rows, act, w_dn, b_dn, bm):
    n_rows = act.shape[0]
    n_blocks = n_rows // bm
    grid_spec = pltpu.PrefetchScalarGridSpec(
        num_scalar_prefetch=2,
        grid=(n_blocks,),
        in_specs=[pl.BlockSpec((bm, D_FF), lambda j, be, nv: (j, 0)),
                  pl.BlockSpec((1, D_FF, D_MODEL), lambda j, be, nv: (be[j], 0, 0)),
                  pl.BlockSpec((1, 1, D_MODEL), lambda j, be, nv: (be[j], 0, 0))],
        out_specs=pl.BlockSpec((bm * ROW_TILE, LANES), lambda j, be, nv: (j, 0)))
    return pl.pallas_call(
        _dn_kernel,
        grid_spec=grid_spec,
        out_shape=jax.ShapeDtypeStruct((n_rows * ROW_TILE, LANES), jnp.int32),
        compiler_params=_cparams(("arbitrary",), VMEM_LIMIT),
        name="moe_down",
    )(block_e, blk_rows, act, w_dn, b_dn)


def _combine_kernel(dest_ref, next_ref, gate_ref, x1_ref, g_ref, b_ref, ys_hbm, o_ref, buf_a, buf_b, sem):
    i = pl.program_id(0)
    tt = x1_ref.shape[0]
    sub = tt // COMBINE_SUBBLOCKS
    bufs = (buf_a, buf_b)

    def start_rows(d_ref, s, t):
        for kk in range(TOP_K):
            row = pl.multiple_of((kk * tt + t) * ROW_TILE, ROW_TILE)
            pltpu.make_async_copy(ys_hbm.at[d_ref[t * TOP_K + kk]],
                                  bufs[s].at[pl.ds(row, ROW_TILE), :], sem.at[s]).start(priority=kk % 2)

    def wait_tile(s):
        for kk in range(TOP_K):
            view = bufs[s].at[pl.ds(kk * tt * ROW_TILE, tt * ROW_TILE), :]
            pltpu.make_async_copy(view, view, sem.at[s]).wait()

    @pl.when(i == 0)
    def _():
        def body(t, c):
            start_rows(dest_ref, 0, t)
            return c
        lax.fori_loop(0, tt, body, 0)

    def step(cur, nxt):
        wait_tile(cur)

        def reduce_rows(sb, c):
            r0 = pl.multiple_of(sb * sub, sub)
            for tl in range(sub):
                start_rows(next_ref, nxt, r0 + tl)
            rows = pl.ds(r0, sub)
            gates = gate_ref[rows, :]
            acc_lo = jnp.zeros((sub, ROW_WORDS), F32)
            acc_hi = jnp.zeros((sub, ROW_WORDS), F32)
            for kk in range(TOP_K):
                lo, hi = _unpack_row(_load_row_tiles(bufs[cur], kk * tt + r0, sub))
                gk = gates[:, kk:kk + 1]
                acc_lo = acc_lo + gk * lo
                acc_hi = acc_hi + gk * hi
            ffn = jnp.concatenate([acc_lo, acc_hi], axis=1)
            o_ref[rows, :] = _layer_norm(DEEPNORM_ALPHA * x1_ref[rows, :] + ffn, g_ref[...], b_ref[...])
            return c

        lax.fori_loop(0, COMBINE_SUBBLOCKS, reduce_rows, 0)

        @pl.when(i + 1 == pl.num_programs(0))
        def _():
            wait_tile(nxt)

    for cur in range(2):
        @pl.when(i % 2 == cur)
        def _(cur=cur):
            step(cur, 1 - cur)


def _combine_ln(dest_flat, gates, x1, g, b, ys3, tt):
    n = x1.shape[0]
    last = n // tt - 1
    const = lambda shape: pl.BlockSpec(shape, lambda i: (0, 0))
    return pl.pallas_call(
        _combine_kernel,
        grid=(n // tt,),
        in_specs=[pl.BlockSpec((tt * TOP_K,), lambda i: (i,), memory_space=pltpu.SMEM),
                  pl.BlockSpec((tt * TOP_K,), lambda i: (jnp.minimum(i + 1, last),), memory_space=pltpu.SMEM),
                  pl.BlockSpec((tt, LANES), lambda i: (i, 0)),
                  pl.BlockSpec((tt, D_MODEL), lambda i: (i, 0)),
                  const((1, D_MODEL)), const((1, D_MODEL)),
                  pl.BlockSpec(memory_space=pl.ANY)],
        out_specs=pl.BlockSpec((tt, D_MODEL), lambda i: (i, 0)),
        out_shape=jax.ShapeDtypeStruct((n, D_MODEL), F32),
        scratch_shapes=[pltpu.VMEM((TOP_K * tt * ROW_TILE, LANES), jnp.int32),
                        pltpu.VMEM((TOP_K * tt * ROW_TILE, LANES), jnp.int32),
                        pltpu.SemaphoreType.DMA((2,))],
        compiler_params=_cparams(("arbitrary",), VMEM_LIMIT),
        name="moe_combine_ln",
    )(dest_flat, dest_flat, gates, x1, g, b, ys3)


def _pad_cols(a, width):
    return jnp.pad(a, ((0, 0), (0, width - a.shape[1])))


def _lora_layout(a):
    dw = a[:, :DECAY_LORA]
    da = a[:, DECAY_LORA:DECAY_LORA + AAA_LORA]
    dg = a[:, DECAY_LORA + AAA_LORA:]
    return jnp.concatenate([_pad_cols(dw, LANES), _pad_cols(da, LANES), _pad_cols(dg, 2 * LANES)], axis=1)


def _moe_ffn(x1, x1p2, idx, gates, counts, w_gu, b_gu, w_dn, b_dn, ln_g, ln_b, bm, tn, tt):
    n = x1.shape[0]
    nk = n * TOP_K
    n_blocks = nk // bm + N_EXPERTS
    n_rows = n_blocks * bm
    cnt = counts[0, :N_EXPERTS].astype(jnp.int32)
    padded = ((cnt + bm - 1) // bm) * bm
    pends = jnp.cumsum(padded)
    lanes = lambda a: jnp.pad(a, (0, LANES - a.shape[0]))
    pstarts = lanes(pends - padded).astype(F32).reshape(1, LANES)
    pad_off = lanes(jnp.concatenate([pends - padded + cnt, pends[-1:]])).astype(jnp.int32)
    pad_len = lanes(jnp.concatenate([padded - cnt, (n_rows - pends[-1:]) // (bm // 2)])).astype(jnp.int32)
    block_start = jnp.arange(n_blocks, dtype=jnp.int32) * bm
    block_e = jnp.minimum(jnp.sum(pends[None, :] <= block_start[:, None], axis=1), N_EXPERTS - 1).astype(jnp.int32)
    blk_rows = jnp.clip(cnt[block_e] - (block_start - (pends - padded)[block_e]), 0, bm).astype(jnp.int32)

    dest_flat = _row_dest(idx, pstarts, tt)[:, :TOP_K].reshape(nk)
    xs3 = _dispatch(dest_flat, pad_off, pad_len, x1p2, n_rows, tt, bm)
    act = _moe_gate_up(block_e, blk_rows, xs3.reshape(n_rows * ROW_TILE, LANES), w_gu,
                       b_gu.reshape(N_EXPERTS, 1, 2 * D_FF), bm, tn)
    ys2 = _moe_down(block_e, blk_rows, act, w_dn, b_dn.reshape(N_EXPERTS, 1, D_MODEL), bm)
    return _combine_ln(dest_flat, gates, x1, ln_g, ln_b, ys2.reshape(n_rows, ROW_TILE, LANES), tt)


def _layer(x, w_in, shift_mu, w0, w_up, a0, a_up, g_up, k_k, k_a, r_k, gn_g, gn_b,
           lq1, lk1, lq2, lk2, subln_g, w_out, ln1_g, ln1_b,
           w_router, b_router, w_gu, b_gu, w_dn, b_dn, ln2_g, ln2_b, lambda_init,
           tm_in=2048, tseq=512, ng=4, tq=512, tm_out=512, bm=512, tn=1024, tt=256):
    batch, seq, d = x.shape
    n = batch * seq
    rw = 3 * RWKV_WIDTH
    rcols = rw + DECAY_LORA + AAA_LORA + GATE_LORA
    row = lambda a: a.reshape(1, -1)

    xf = x.reshape(n, d)
    xb = xf.astype(BF16)
    w_r = jnp.concatenate([w_in[:, :rw], _lora_layout(w_in[:, rw:rcols])], axis=1).astype(BF16)
    w_d = w_in[:, rcols:].astype(BF16)
    mu = jnp.concatenate([row(shift_mu)[:, :rw], _lora_layout(row(shift_mu)[:, rw:])], axis=1)
    p_r = _matmul(xb, w_r, F32, tm_in, 512)
    p_d = _matmul(xb, w_d, BF16, tm_in, 512)

    pad_rows = lambda a, rows: jnp.pad(a, ((0, rows - a.shape[0]), (0, 0))).astype(BF16)
    h_r = _rwkv(p_r, mu, row(w0), row(a0), row(k_k), row(k_a), row(r_k), row(gn_g), row(gn_b),
                pad_rows(w_up, LANES), pad_rows(a_up, LANES), pad_rows(g_up, 2 * LANES), batch, seq, tseq, ng)
    h_d = _diff_attention(p_d, row(lq1), row(lk1), row(lq2), row(lk2), row(subln_g), lambda_init,
                          batch, seq, tq)

    w_ob = w_out.astype(BF16)
    w_rp = _pad_cols(w_router, LANES)
    b_rp = jnp.concatenate([row(b_router), jnp.full((1, LANES - N_EXPERTS), NEG_BIG, F32)], axis=1)
    x1, x1p2, idx, gates, counts = _outproj_ln_router(
        h_r, h_d, xf, w_ob[:RWKV_WIDTH], w_ob[RWKV_WIDTH:], row(ln1_g), row(ln1_b), w_rp, b_rp, tm_out)
    out = _moe_ffn(x1, x1p2, idx, gates, counts, w_gu, b_gu, w_dn, b_dn, row(ln2_g), row(ln2_b), bm, tn, tt)
    return out.reshape(batch, seq, d)


def kernel(x, w_in, shift_mu, w0, w_up, a0, a_up, g_up, k_k, k_a, r_k, gn_g, gn_b, lq1, lk1, lq2, lk2,
           subln_g, w_out, ln1_g, ln1_b, w_router, b_router, w_gu, b_gu, w_dn, b_dn, ln2_g, ln2_b):
    for l in range(DEPTH):
        lambda_init = 0.8 - 0.6 * math.exp(-0.3 * l)
        x = _layer(x, w_in[l], shift_mu[l], w0[l], w_up[l], a0[l], a_up[l], g_up[l], k_k[l], k_a[l],
                   r_k[l], gn_g[l], gn_b[l], lq1[l], lk1[l], lq2[l], lk2[l], subln_g[l], w_out[l],
                   ln1_g[l], ln1_b[l], w_router[l], b_router[l], w_gu[l], b_gu[l], w_dn[l], b_dn[l],
                   ln2_g[l], ln2_b[l], lambda_init)
    return x
```

```python
import functools
import math

import jax
import jax.numpy as jnp
from jax import lax
from jax.experimental import pallas as pl
from jax.experimental.pallas import tpu as pltpu

F32 = jnp.float32
BF16 = jnp.bfloat16

D_MODEL = 2048
RWKV_HEAD = 64
RWKV_WIDTH = 1024
RWKV_HEADS = 16
DECAY_LORA = 64
AAA_LORA = 64
GATE_LORA = 160
DIFF_HEAD = 64
DIFF_VDIM = 128
DIFF_HEADS = 8
DIFF_WIDTH = 1024
N_EXPERTS = 32
TOP_K = 4
D_FF = 2048
SWIGLU_LIMIT = 7.0
SWIGLU_ALPHA = 1.702
LN_EPS = 1e-5
GN_EPS = RWKV_HEAD * 1e-5
RMS_EPS = 1e-5
NEG_BIG = -1e30
DEPTH = 1
DEEPNORM_ALPHA = (2.0 * DEPTH) ** 0.25

LANES = 128
SUBLANES = 8
ROW_WORDS = D_MODEL // 2
ROW_TILE = ROW_WORDS // LANES
RWKV_GROUP = 256
RWKV_CHUNK = 64
LORA_COLS = 512
VMEM_LIMIT = 56 * 1024 * 1024
HIGH_HALF = -65536
MXU_DEPTH = 256
OUTPROJ_PARTS = 2
MOE_ROW_PIECES = 4
COMBINE_SUBBLOCKS = 8


def _cparams(sem, vmem=None):
    return pltpu.CompilerParams(dimension_semantics=sem, vmem_limit_bytes=vmem)


def _dot(a, b):
    return jnp.dot(a, b, preferred_element_type=F32)


def _dot_nt(a, b):
    return lax.dot_general(a, b, (((1,), (1,)), ((), ())), preferred_element_type=F32)


def _dot_tn(a, b):
    return lax.dot_general(a, b, (((0,), (0,)), ((), ())), preferred_element_type=F32)


def _split3(x):
    h = x.astype(BF16)
    r = x - h.astype(F32)
    m = r.astype(BF16)
    l = (r - m.astype(F32)).astype(BF16)
    return h, m, l


def _dot_exact_rhs(x, ones):
    h, m, l = _split3(x)
    return _dot(h, ones) + _dot(m, ones) + _dot(l, ones)


def _dot_exact_lhs(ones, x):
    h, m, l = _split3(x)
    return _dot(ones, h) + _dot(ones, m) + _dot(ones, l)


def _sigmoid(x):
    return 1.0 / (1.0 + jnp.exp(-x))


def _matmul_kernel(x_ref, w_ref, o_ref):
    o_ref[...] = _dot(x_ref[...], w_ref[...]).astype(o_ref.dtype)


def _matmul(x, w, out_dtype, tm, tn):
    m, k = x.shape
    n = w.shape[1]
    return pl.pallas_call(
        _matmul_kernel,
        grid=(n // tn, m // tm),
        in_specs=[pl.BlockSpec((tm, k), lambda j, i: (i, 0)),
                  pl.BlockSpec((k, tn), lambda j, i: (0, j))],
        out_specs=pl.BlockSpec((tm, tn), lambda j, i: (i, j)),
        out_shape=jax.ShapeDtypeStruct((m, n), out_dtype),
        compiler_params=_cparams(("parallel", "parallel"), VMEM_LIMIT),
        name="in_proj",
    )(x, w)


def _rwkv_kernel(r_ref, k_ref, v_ref, l_ref, mur_ref, muk_ref, muv_ref, mul_ref,
                 w0_ref, a0_ref, kk_ref, ka_ref, rk_ref, gng_ref, gnb_ref,
                 wup_ref, aup_ref, gup_ref, o_ref,
                 pr_s, pk_s, pv_s, pl_s, state_s, r_s, w_s, k_s, v_s, a_s, b_s, g_s,
                 y_s, rc_p, lrb_p, lrk_p, tb_p, wc_p, akv_p, be_p, ke_p, dec_p):
    s = pl.program_id(2)
    T = r_ref.shape[0]
    G = RWKV_GROUP
    C = RWKV_CHUNK
    NG = r_ref.shape[1] // G
    groups = [slice(g * G, (g + 1) * G) for g in range(NG)]

    @pl.when(s == 0)
    def _():
        state_s[...] = jnp.zeros_like(state_s)
        pr_s[...] = jnp.zeros_like(pr_s)
        pk_s[...] = jnp.zeros_like(pk_s)
        pv_s[...] = jnp.zeros_like(pv_s)
        pl_s[...] = jnp.zeros_like(pl_s)

    row = lax.broadcasted_iota(jnp.int32, (T, 1), 0)

    def shift(ref, prev_s, mu_ref):
        p = ref[...]
        prev = jnp.where(row == 0, prev_s[...], pltpu.roll(p, 1, 0))
        prev_s[...] = p[T - 1:T, :]
        return p + (prev - p) * mu_ref[...]

    r = shift(r_ref, pr_s, mur_ref)
    k = shift(k_ref, pk_s, muk_ref)
    v = shift(v_ref, pv_s, muv_ref)
    lo = shift(l_ref, pl_s, mul_ref)
    dw = lo[:, 0:LANES]
    da = lo[:, LANES:2 * LANES]
    dg = lo[:, 2 * LANES:4 * LANES]

    wpre = w0_ref[...] + _dot(jnp.tanh(dw).astype(BF16), wup_ref[...])
    wlog = -math.exp(-0.5) * _sigmoid(wpre)
    a_sig = _sigmoid(a0_ref[...] + _dot(da.astype(BF16), aup_ref[...]))
    gate = _dot(_sigmoid(dg).astype(BF16), gup_ref[...])

    gi = lax.broadcasted_iota(jnp.int32, (G, G), 0)
    gj = lax.broadcasted_iota(jnp.int32, (G, G), 1)
    same_head = (gi // RWKV_HEAD) == (gj // RWKV_HEAD)
    head_ones = jnp.where(same_head, 1.0, 0.0).astype(BF16)

    def head_sum(x):
        return jnp.concatenate([_dot_exact_rhs(x[:, g], head_ones) for g in groups], axis=1)

    kk = k * kk_ref[...]
    nrm = jnp.sqrt(head_sum(kk * kk))
    kk = kk / jnp.maximum(nrm, 1e-12)
    k2 = k * (1.0 + (a_sig - 1.0) * ka_ref[...])

    r_s[...] = r
    w_s[...] = wlog
    k_s[...] = k2
    v_s[...] = v
    a_s[...] = -kk
    b_s[...] = kk * a_sig
    g_s[...] = gate

    ci = lax.broadcasted_iota(jnp.int32, (C, C), 0)
    cj = lax.broadcasted_iota(jnp.int32, (C, C), 1)
    tri = jnp.where(cj <= ci, 1.0, 0.0).astype(BF16)
    mt = lax.broadcasted_iota(jnp.int32, (C, G), 0)
    mtp = lax.broadcasted_iota(jnp.int32, (C, G), 1) % C
    strict = mtp < mt
    incl = mtp <= mt
    ceye = jnp.where(mtp == mt, 1.0, 0.0)

    def bd(xc):
        return jnp.where(same_head, jnp.concatenate([xc, xc, xc, xc], axis=0), jnp.zeros((), BF16))

    def each(fn, *lists):
        return [fn(*xs) for xs in zip(*lists)]

    def bf(x):
        return x.astype(BF16)

    def prepare(cp, carry):
        streams = [(pl.ds(pl.multiple_of((2 * cp + h) * C, C), C), g) for h in range(2) for g in groups]
        rows = [2 * cp + h for h in range(2) for _ in groups]
        rc = [r_s[sl, g] for sl, g in streams]
        wc = [w_s[sl, g] for sl, g in streams]
        kc = [k_s[sl, g] for sl, g in streams]
        vc = [v_s[sl, g] for sl, g in streams]
        ac = [a_s[sl, g] for sl, g in streams]
        bc = [b_s[sl, g] for sl, g in streams]
        cum = each(lambda w: _dot_exact_lhs(tri, w), wc)
        tot = each(lambda x: x[C - 1:C, :], cum)
        ginv = each(lambda x: jnp.exp(-x), cum)
        gend = each(lambda x, t: jnp.exp(t - x), cum, tot)
        r_c = each(lambda x, g: bf(x * jnp.exp(g)), rc, cum)
        a_c = each(lambda x, g, w: bf(x * jnp.exp(g - w)), ac, cum, wc)
        v_c = each(bf, vc)
        k_bd = each(lambda x, g: bd(bf(x * g)), kc, ginv)
        b_bd = each(lambda x, g: bd(bf(x * g)), bc, ginv)
        ke_c = each(lambda x, g: bf(x * g), kc, gend)
        be_c = each(lambda x, g: bf(x * g), bc, gend)
        v_bd = each(bd, v_c)
        a_bd = each(bd, a_c)

        ar = each(lambda a, r: jnp.concatenate([a, r], axis=0), a_c, r_c)
        arb = each(_dot_nt, ar, b_bd)
        ark = each(_dot_nt, ar, k_bd)
        l_ab = each(lambda x: jnp.where(strict, x[:C], 0.0), arb)
        l_ak = each(lambda x: bf(jnp.where(strict, x[:C], 0.0)), ark)
        l_rb = each(lambda x: bf(jnp.where(incl, x[C:], 0.0)), arb)
        l_rk = each(lambda x: bf(jnp.where(incl, x[C:], 0.0)), ark)

        p = each(bf, l_ab)
        p_bd = each(bd, p)
        tinv = each(lambda x: ceye + x, l_ab)
        for _ in range(int(math.log2(C)) - 1):
            p = each(lambda x, y: bf(_dot(x, y)), p, p_bd)
            p_bd = each(bd, p)
            tinv = each(lambda t, y: t + _dot(bf(t), y), tinv, p_bd)
        tb = each(bf, tinv)
        akv_c = each(lambda a, b: bf(_dot(a, b)), l_ak, v_bd)
        w_c = each(lambda a, b: bf(_dot(a, b)), tb, a_bd)
        for ref, vals in zip((rc_p, lrb_p, lrk_p, tb_p, wc_p, akv_p, be_p, ke_p),
                             (r_c, l_rb, l_rk, tb, w_c, akv_c, be_c, ke_c)):
            for (sl, g), val in zip(streams, vals):
                ref[sl, g] = val
        for (_, g), row, t in zip(streams, rows, tot):
            dec_p[pl.ds(row, 1), g] = jnp.exp(t)
        return carry

    lax.fori_loop(0, T // C // 2, prepare, 0)

    def chunk(c, carry):
        sl = pl.ds(pl.multiple_of(c * C, C), C)
        load = lambda ref: [ref[sl, g] for g in groups]
        r_c, l_rb, l_rk, tb, w_c, be_c, ke_c = (load(ref) for ref in
                                                (rc_p, lrb_p, lrk_p, tb_p, wc_p, be_p, ke_p))
        akv_bd = each(bd, load(akv_p))
        v_c = each(bf, load(v_s))
        v_bd = each(bd, v_c)
        dec = [dec_p[pl.ds(c, 1), g] for g in groups]
        st = [state_s[g] for g in range(NG)]
        stb = each(bf, st)
        u_c = each(lambda w, s0, t, x: bf(_dot_nt(w, s0) + _dot(t, x)), w_c, stb, tb, akv_bd)
        u_bd = each(bd, u_c)
        y = each(lambda r, s0, lb, lk, u, vv: _dot_nt(r, s0) + _dot(jnp.concatenate([lb, lk], axis=1),
                                                                    jnp.concatenate([u, vv], axis=0)),
                 r_c, stb, l_rb, l_rk, u_bd, v_bd)
        new = each(lambda s0, t, u, vv, b, kx: s0 * t + jnp.where(
            same_head, _dot_tn(jnp.concatenate([u, vv], axis=0), jnp.concatenate([b, kx], axis=0)), 0.0),
            st, dec, u_c, v_c, be_c, ke_c)
        for g in range(NG):
            state_s[g] = new[g]
        for g, yy in zip(groups, y):
            y_s[sl, g] = yy
        return carry

    lax.fori_loop(0, T // C, chunk, 0)

    def finish(cp, carry):
        streams = [(pl.ds(pl.multiple_of((2 * cp + h) * C, C), C), g) for h in range(2) for g in groups]
        y = [y_s[sl, g] for sl, g in streams]
        mean = each(lambda x: _dot_exact_rhs(x, head_ones) * (1.0 / RWKV_HEAD), y)
        d = each(lambda x, m: x - m, y, mean)
        var = each(lambda x: _dot_exact_rhs(x * x, head_ones) * (1.0 / RWKV_HEAD), d)
        bonus = [_dot_exact_rhs(r_s[sl, g] * k_s[sl, g] * rk_ref[:, g], head_ones) * v_s[sl, g]
                 for sl, g in streams]
        for (sl, g), dd, vr, bo in zip(streams, d, var, bonus):
            yn = dd * lax.rsqrt(vr + GN_EPS) * gng_ref[:, g] + gnb_ref[:, g]
            o_ref[sl, g] = ((yn + bo) * g_s[sl, g]).astype(o_ref.dtype)
        return carry

    lax.fori_loop(0, T // C // 2, finish, 0)


def _rwkv(p_r, mu, w0, a0, k_k, k_a, r_k, gn_g, gn_b, w_up, a_up, g_up, batch, seq, tseq, ng):
    n = batch * seq
    G = ng * RWKV_GROUP
    nq = RWKV_WIDTH // G
    ns = seq // tseq
    lora_blk = 3 * RWKV_WIDTH // LORA_COLS

    def tok(off):
        return pl.BlockSpec((tseq, G), lambda b, q, s: (b * ns + s, off + q))

    def par(off):
        return pl.BlockSpec((1, G), lambda b, q, s: (0, off + q))

    in_specs = [
        tok(0), tok(nq), tok(2 * nq),
        pl.BlockSpec((tseq, LORA_COLS), lambda b, q, s: (b * ns + s, lora_blk)),
        par(0), par(nq), par(2 * nq),
        pl.BlockSpec((1, LORA_COLS), lambda b, q, s: (0, lora_blk)),
        par(0), par(0), par(0), par(0), par(0), par(0), par(0),
        pl.BlockSpec((LANES, G), lambda b, q, s: (0, q)),
        pl.BlockSpec((LANES, G), lambda b, q, s: (0, q)),
        pl.BlockSpec((2 * LANES, G), lambda b, q, s: (0, q)),
    ]
    scratch = [pltpu.VMEM((1, G), F32), pltpu.VMEM((1, G), F32), pltpu.VMEM((1, G), F32),
               pltpu.VMEM((1, LORA_COLS), F32), pltpu.VMEM((ng, RWKV_GROUP, RWKV_GROUP), F32)]
    scratch += [pltpu.VMEM((tseq, G), F32) for _ in range(8)]
    scratch += [pltpu.VMEM((tseq, G), BF16) for _ in range(8)]
    scratch += [pltpu.VMEM((tseq // RWKV_CHUNK, G), F32)]
    return pl.pallas_call(
        _rwkv_kernel,
        grid=(batch, nq, ns),
        in_specs=in_specs,
        out_specs=pl.BlockSpec((tseq, G), lambda b, q, s: (b * ns + s, q)),
        out_shape=jax.ShapeDtypeStruct((n, RWKV_WIDTH), BF16),
        scratch_shapes=scratch,
        compiler_params=_cparams(("parallel", "parallel", "arbitrary"), VMEM_LIMIT),
        name="rwkv7",
    )(p_r, p_r, p_r, p_r, mu, mu, mu, mu, w0, a0, k_k, k_a, r_k, gn_g, gn_b, w_up, a_up, g_up)


def _attn_kernel(q_ref, k_ref, v_ref, lq1_ref, lk1_ref, lq2_ref, lk2_ref, g_ref, o_ref, vt_s, *, lambda_init):
    i = pl.program_id(2)
    tq = q_ref.shape[0]

    @pl.when(i == 0)
    def _():
        for j in range(vt_s.shape[0]):
            vt_s[j] = v_ref[j * tq:(j + 1) * tq, :].astype(F32).T.astype(BF16)

    drow = lax.broadcasted_iota(jnp.int32, (DIFF_VDIM, 1), 0)
    qt = (q_ref[...].astype(F32) * (DIFF_HEAD ** -0.5)).T
    q1 = jnp.where(drow < DIFF_HEAD, qt, 0.0).astype(BF16)
    q2 = jnp.where(drow >= DIFF_HEAD, qt, 0.0).astype(BF16)
    lam = (jnp.exp(jnp.sum(lq1_ref[...] * lk1_ref[...], axis=-1, keepdims=True))
           - jnp.exp(jnp.sum(lq2_ref[...] * lk2_ref[...], axis=-1, keepdims=True)) + lambda_init)

    key = lax.broadcasted_iota(jnp.int32, (tq, tq), 0)
    qry = lax.broadcasted_iota(jnp.int32, (tq, tq), 1)

    def scores(j, diagonal):
        kj = k_ref[j * tq:(j + 1) * tq, :]
        sc = (_dot(kj, q1), _dot(kj, q2))
        if diagonal:
            sc = tuple(jnp.where(key <= qry, s, NEG_BIG) for s in sc)
        return sc

    def absorb(j, sc, stats):
        m, l, acc = stats[0:2], stats[2:4], stats[4:6]
        vtj = vt_s[j]
        m_new = [jnp.maximum(mm, jnp.max(s, axis=0, keepdims=True)) for mm, s in zip(m, sc)]
        alpha = [jnp.exp(mm - mn) for mm, mn in zip(m, m_new)]
        p = [jnp.exp(s - mn) for s, mn in zip(sc, m_new)]
        l = [a * ll + jnp.sum(pp, axis=0, keepdims=True) for a, ll, pp in zip(alpha, l, p)]
        pv = [_dot(vtj, pp.astype(BF16)) for pp in p]
        acc = [a * ac + x for a, ac, x in zip(alpha, acc, pv)]
        return tuple(m_new) + tuple(l) + tuple(acc)

    zero1 = jnp.zeros((1, tq), F32)
    neg1 = jnp.full((1, tq), NEG_BIG, F32)
    zacc = jnp.zeros((DIFF_VDIM, tq), F32)
    init = (neg1, neg1, zero1, zero1, zacc, zacc)

    for iv in range(vt_s.shape[0]):
        @pl.when(i == iv)
        def _(iv=iv):
            stats = init
            sc = scores(0, iv == 0)
            for j in range(iv + 1):
                nxt = scores(j + 1, j + 1 == iv) if j < iv else None
                stats = absorb(j, sc, stats)
                sc = nxt
            l, acc = stats[2:4], stats[4:6]
            o = acc[0] / l[0] - lam * (acc[1] / l[1])
            o = o * lax.rsqrt(jnp.mean(o * o, axis=0, keepdims=True) + RMS_EPS) * g_ref[...]
            o_ref[...] = (o * (1.0 - lambda_init)).T.astype(o_ref.dtype)


def _diff_attention(p_d, lq1, lk1, lq2, lk2, subln_g, lambda_init, batch, seq, tq):
    n = batch * seq
    nq = seq // tq
    H = DIFF_HEADS
    small = pl.BlockSpec((1, DIFF_HEAD), lambda b, h, i: (0, 0))
    return pl.pallas_call(
        functools.partial(_attn_kernel, lambda_init=lambda_init),
        grid=(batch, H, nq),
        in_specs=[pl.BlockSpec((tq, DIFF_VDIM), lambda b, h, i: (b * nq + i, h)),
                  pl.BlockSpec((seq, DIFF_VDIM), lambda b, h, i: (b, H + h)),
                  pl.BlockSpec((seq, DIFF_VDIM), lambda b, h, i: (b, 2 * H + h)),
                  small, small, small, small,
                  pl.BlockSpec((DIFF_VDIM, 1), lambda b, h, i: (0, 0))],
        out_specs=pl.BlockSpec((tq, DIFF_VDIM), lambda b, h, i: (b * nq + i, h)),
        out_shape=jax.ShapeDtypeStruct((n, DIFF_WIDTH), BF16),
        scratch_shapes=[pltpu.VMEM((nq, DIFF_VDIM, tq), BF16)],
        compiler_params=_cparams(("parallel", "parallel", "arbitrary"), VMEM_LIMIT),
        name="diff_attn",
    )(p_d, p_d, p_d, lq1, lk1, lq2, lk2, subln_g.reshape(DIFF_VDIM, 1))


def _layer_norm(y, g, b):
    mu = jnp.mean(y, axis=-1, keepdims=True)
    d = y - mu
    var = jnp.mean(d * d, axis=-1, keepdims=True)
    return d * lax.rsqrt(var + LN_EPS) * g + b


def _store_row_tiles(ref, words, start=0, sublane=0):
    rows = words.shape[0]
    for s in range(words.shape[1] // LANES):
        ref[pl.ds(start * ROW_TILE + sublane + s, rows, stride=ROW_TILE), :] = words[:, s * LANES:(s + 1) * LANES]


def _load_row_tiles(ref, start, rows):
    parts = [ref[pl.ds(start * ROW_TILE + s, rows, stride=ROW_TILE), :] for s in range(ROW_TILE)]
    return jnp.concatenate(parts, axis=1)


def _pack_pair(lo, hi):
    bits = lambda y: lax.bitcast_convert_type(y.astype(BF16).astype(F32), jnp.int32)
    return lax.shift_right_logical(bits(lo), 16) | (bits(hi) & HIGH_HALF)


def _pack_row(y):
    return _pack_pair(y[:, :ROW_WORDS], y[:, ROW_WORDS:])


def _unpack_row(words):
    lo = lax.bitcast_convert_type(lax.shift_left(words, 16), F32)
    hi = lax.bitcast_convert_type(words & HIGH_HALF, F32)
    return lo, hi


def _expert_onehots(idx):
    lane = lax.broadcasted_iota(jnp.int32, idx.shape, 1)
    sels = [lane == idx[:, kk:kk + 1] for kk in range(TOP_K)]
    onehot = jnp.zeros(idx.shape, F32)
    for sel in sels:
        onehot = onehot + jnp.where(sel, 1.0, 0.0)
    return sels, onehot


def _outproj_kernel(hr_ref, hd_ref, x_ref, wt_ref, wb_ref, g_ref, b_ref, wrh_ref, wrl_ref, br_ref,
                    x1_ref, x1p_ref, idx_ref, gate_ref, cnt_ref, carry_s):
    i = pl.program_id(0)
    tm = x_ref.shape[0]
    th = tm // OUTPROJ_PARTS
    parts = [pl.ds(h * th, th) for h in range(OUTPROJ_PARTS)]

    @pl.when(i == 0)
    def _():
        carry_s[...] = jnp.zeros_like(carry_s)

    def each(fn, *lists):
        return [fn(*xs) for xs in zip(*lists)]

    mix = [_dot(hr_ref[p, :], wt_ref[...]) + _dot(hd_ref[p, :], wb_ref[...]) for p in parts]
    x1 = [_layer_norm(DEEPNORM_ALPHA * x_ref[p, :] + m, g_ref[...], b_ref[...]) for p, m in zip(parts, mix)]
    for h, (p, y) in enumerate(zip(parts, x1)):
        x1_ref[p, :] = y
        _store_row_tiles(x1p_ref, _pack_row(y), h * th)

    xh = each(lambda y: y.astype(BF16), x1)
    xl = each(lambda y, hh: (y - hh.astype(F32)).astype(BF16), x1, xh)
    work = each(lambda hh, ll: _dot(hh, wrh_ref[...]) + _dot(ll, wrh_ref[...]) + _dot(hh, wrl_ref[...])
                + br_ref[...], xh, xl)
    lane = lax.broadcasted_iota(jnp.int32, (th, LANES), 1).astype(F32)
    onehot = [jnp.zeros((th, LANES), F32) for _ in parts]
    vals, idxs = [], []
    for _ in range(TOP_K):
        mx = each(lambda w: jnp.max(w, axis=-1, keepdims=True), work)
        idx = each(lambda w, m: jnp.min(jnp.where(w == m, lane, float(LANES)), axis=-1, keepdims=True), work, mx)
        sel = each(lambda ix: lane == ix, idx)
        work = each(lambda s, w: jnp.where(s, -jnp.inf, w), sel, work)
        onehot = each(lambda o, s: o + jnp.where(s, 1.0, 0.0), onehot, sel)
        vals.append(mx)
        idxs.append(idx)
    total = carry_s[0:1, :]
    for h, p in enumerate(parts):
        exps = [jnp.exp(vv[h] - vals[0][h]) for vv in vals]
        den = exps[0] + exps[1] + exps[2] + exps[3]
        idx_out = jnp.zeros((th, LANES), F32)
        gate_out = jnp.zeros((th, LANES), F32)
        for kk in range(TOP_K):
            slot = lane == float(kk)
            idx_out = jnp.where(slot, idxs[kk][h], idx_out)
            gate_out = jnp.where(slot, exps[kk] / den, gate_out)
        idx_ref[p, :] = idx_out.astype(jnp.int32)
        gate_ref[p, :] = gate_out
        total = total + jnp.sum(onehot[h], axis=0, keepdims=True)
    carry_s[...] = jnp.broadcast_to(total, carry_s.shape)
    cnt_ref[...] = jnp.broadcast_to(total, cnt_ref.shape)


def _outproj_ln_router(hr, hd, x, w_top, w_bot, g, b, w_r, b_r, tm):
    n = x.shape[0]
    w_rh = w_r.astype(BF16)
    w_rl = (w_r - w_rh.astype(F32)).astype(BF16)
    const = lambda shape: pl.BlockSpec(shape, lambda i: (0, 0))
    rowb = lambda cols: pl.BlockSpec((tm, cols), lambda i: (i, 0))
    out_shape = (jax.ShapeDtypeStruct((n, D_MODEL), F32),
                 jax.ShapeDtypeStruct((n * ROW_TILE, LANES), jnp.int32),
                 jax.ShapeDtypeStruct((n, LANES), jnp.int32),
                 jax.ShapeDtypeStruct((n, LANES), F32),
                 jax.ShapeDtypeStruct((SUBLANES, LANES), F32))
    return pl.pallas_call(
        _outproj_kernel,
        grid=(n // tm,),
        in_specs=[rowb(RWKV_WIDTH), rowb(DIFF_WIDTH), rowb(D_MODEL),
                  const((RWKV_WIDTH, D_MODEL)), const((DIFF_WIDTH, D_MODEL)),
                  const((1, D_MODEL)), const((1, D_MODEL)),
                  const((D_MODEL, LANES)), const((D_MODEL, LANES)), const((1, LANES))],
        out_specs=(rowb(D_MODEL), pl.BlockSpec((tm * ROW_TILE, LANES), lambda i: (i, 0)),
                   rowb(LANES), rowb(LANES), const((SUBLANES, LANES))),
        out_shape=out_shape,
        scratch_shapes=[pltpu.VMEM((SUBLANES, LANES), F32)],
        compiler_params=_cparams(("arbitrary",), VMEM_LIMIT),
        name="outproj_ln_router",
    )(hr, hd, x, w_top, w_bot, g, b, w_rh, w_rl, b_r)


def _dest_kernel(idx_ref, ps_ref, dest_ref, carry_s):
    i = pl.program_id(0)
    tm = idx_ref.shape[0]

    @pl.when(i == 0)
    def _():
        carry_s[...] = jnp.broadcast_to(ps_ref[...], carry_s.shape)

    sels, onehot = _expert_onehots(idx_ref[...])
    ti = lax.broadcasted_iota(jnp.int32, (tm, tm), 0)
    tj = lax.broadcasted_iota(jnp.int32, (tm, tm), 1)
    before = jnp.where(tj < ti, 1.0, 0.0).astype(BF16)
    cum = _dot(before, onehot.astype(BF16)) + carry_s[0:1, :]
    lane = lax.broadcasted_iota(jnp.int32, (tm, LANES), 1)
    dest = jnp.zeros((tm, LANES), F32)
    for kk in range(TOP_K):
        dk = jnp.sum(jnp.where(sels[kk], cum, 0.0), axis=-1, keepdims=True)
        dest = jnp.where(lane == kk, dk, dest)
    dest_ref[...] = dest.astype(jnp.int32)
    total = carry_s[0:1, :] + jnp.sum(onehot, axis=0, keepdims=True)
    carry_s[...] = jnp.broadcast_to(total, carry_s.shape)


def _row_dest(idx, pstarts, tm):
    n = idx.shape[0]
    return pl.pallas_call(
        _dest_kernel,
        grid=(n // tm,),
        in_specs=[pl.BlockSpec((tm, LANES), lambda i: (i, 0)), pl.BlockSpec((1, LANES), lambda i: (0, 0))],
        out_specs=pl.BlockSpec((tm, LANES), lambda i: (i, 0)),
        out_shape=jax.ShapeDtypeStruct((n, LANES), jnp.int32),
        scratch_shapes=[pltpu.VMEM((SUBLANES, LANES), F32)],
        compiler_params=_cparams(("arbitrary",)),
        name="moe_row_dest",
    )(idx, pstarts)


def _wait_rows(ref, rows, sem):
    view = ref.at[pl.ds(0, rows)]
    pltpu.make_async_copy(view, view, sem).wait()


def _dispatch_kernel(dest_ref, padoff_ref, padlen_ref, x_ref, xs_hbm, zero_s, sem, zsem):
    tt = x_ref.shape[0] // ROW_TILE

    def pad_copies(e, go):
        off = padoff_ref[e]
        ln = padlen_ref[e]
        for bit in range(zero_s.shape[0].bit_length()):
            size = 1 << bit

            @pl.when((ln >> bit) & 1 == 1)
            def _():
                go(pltpu.make_async_copy(zero_s.at[pl.ds(0, size)],
                                         xs_hbm.at[pl.ds(off + (ln & (size - 1)), size)], zsem))

    def tail_copy(piece):
        rows = zero_s.shape[0]
        first = pl.multiple_of(padoff_ref[N_EXPERTS] + piece * rows, rows)
        return pltpu.make_async_copy(zero_s, xs_hbm.at[pl.ds(first, rows)], zsem)

    @pl.when(pl.program_id(0) == 0)
    def _():
        zero_s[...] = jnp.zeros_like(zero_s)
        for go in (lambda cp: cp.start(), lambda cp: cp.wait()):
            def body(e, c, go=go):
                pad_copies(e, go)
                return c
            lax.fori_loop(0, N_EXPERTS, body, 0)

            def tail(piece, c, go=go):
                go(tail_copy(piece))
                return c
            lax.fori_loop(0, padlen_ref[N_EXPERTS], tail, 0)

    def issue(t, c):
        src = x_ref.at[pl.ds(pl.multiple_of(t * ROW_TILE, ROW_TILE), ROW_TILE), :]
        for kk in range(TOP_K):
            pltpu.make_async_copy(src, xs_hbm.at[dest_ref[t * TOP_K + kk]], sem).start(priority=kk % 2)
        return c

    lax.fori_loop(0, tt, issue, 0)
    for _ in range(TOP_K):
        _wait_rows(xs_hbm, tt, sem)


def _dispatch(dest_flat, pad_off, pad_len, x1p2, n_rows, tt, bm):
    n = x1p2.shape[0] // ROW_TILE
    table = pl.BlockSpec((LANES,), lambda i: (0,), memory_space=pltpu.SMEM)
    return pl.pallas_call(
        _dispatch_kernel,
        grid=(n // tt,),
        in_specs=[pl.BlockSpec((tt * TOP_K,), lambda i: (i,), memory_space=pltpu.SMEM), table, table,
                  pl.BlockSpec((tt * ROW_TILE, LANES), lambda i: (i, 0))],
        out_specs=pl.BlockSpec(memory_space=pl.ANY),
        out_shape=jax.ShapeDtypeStruct((n_rows, ROW_TILE, LANES), jnp.int32),
        scratch_shapes=[pltpu.VMEM((bm // 2, ROW_TILE, LANES), jnp.int32),
                        pltpu.SemaphoreType.DMA, pltpu.SemaphoreType.DMA],
        compiler_params=_cparams(("arbitrary",)),
        name="moe_dispatch",
    )(dest_flat, pad_off, pad_len, x1p2)


def _dot_f32_weights(xk, w_ref, cols):
    acc = None
    for kc, x in enumerate(xk):
        part = _dot(x, w_ref[0, kc * MXU_DEPTH:(kc + 1) * MXU_DEPTH, cols].astype(BF16))
        acc = part if acc is None else acc + part
    return acc


def _for_valid_rows(nr, bm, body, o_ref):
    piece = bm // MOE_ROW_PIECES
    out_rows_per_row = o_ref.shape[0] // bm
    for q in range(1, MOE_ROW_PIECES + 1):
        @pl.when(jnp.logical_and(nr > (q - 1) * piece, nr <= q * piece))
        def _(first=bm - q * piece):
            body(first)
            if first > 0:
                o_ref[0:first * out_rows_per_row, :] = jnp.zeros(
                    (first * out_rows_per_row, o_ref.shape[1]), o_ref.dtype)

    @pl.when(nr == 0)
    def _():
        o_ref[...] = jnp.zeros_like(o_ref)


def _gu_kernel(be_ref, rows_ref, xs_ref, wg_ref, wu_ref, bg_ref, bu_ref, o_ref):
    del be_ref

    bm = o_ref.shape[0]

    def body(first):
        lo, hi = _unpack_row(_load_row_tiles(xs_ref, first, bm - first))
        x = jnp.concatenate([lo, hi], axis=1).astype(BF16)
        xk = [x[:, kc * MXU_DEPTH:(kc + 1) * MXU_DEPTH] for kc in range(D_MODEL // MXU_DEPTH)]
        g = _dot_f32_weights(xk, wg_ref, slice(None)) + bg_ref[0]
        u = _dot_f32_weights(xk, wu_ref, slice(None)) + bu_ref[0]
        g = jnp.minimum(g, SWIGLU_LIMIT)
        u = jnp.clip(u, -SWIGLU_LIMIT, SWIGLU_LIMIT)
        o_ref[first:, :] = ((u + 1.0) * (g * _sigmoid(SWIGLU_ALPHA * g))).astype(o_ref.dtype)

    _for_valid_rows(rows_ref[pl.program_id(1)], bm, body, o_ref)


def _moe_gate_up(block_e, blk_rows, xs2, w_gu, b_gu, bm, tn):
    n_rows = xs2.shape[0] // ROW_TILE
    n_blocks = n_rows // bm
    nt = D_FF // tn
    grid_spec = pltpu.PrefetchScalarGridSpec(
        num_scalar_prefetch=2,
        grid=(nt, n_blocks),
        in_specs=[pl.BlockSpec((bm * ROW_TILE, LANES), lambda n, j, be, nv: (j, 0)),
                  pl.BlockSpec((1, D_MODEL, tn), lambda n, j, be, nv: (be[j], 0, n)),
                  pl.BlockSpec((1, D_MODEL, tn), lambda n, j, be, nv: (be[j], 0, nt + n)),
                  pl.BlockSpec((1, 1, tn), lambda n, j, be, nv: (be[j], 0, n)),
                  pl.BlockSpec((1, 1, tn), lambda n, j, be, nv: (be[j], 0, nt + n))],
        out_specs=pl.BlockSpec((bm, tn), lambda n, j, be, nv: (j, n)))
    return pl.pallas_call(
        _gu_kernel,
        grid_spec=grid_spec,
        out_shape=jax.ShapeDtypeStruct((n_rows, D_FF), BF16),
        compiler_params=_cparams(("arbitrary", "arbitrary"), VMEM_LIMIT),
        name="moe_gate_up",
    )(block_e, blk_rows, xs2, w_gu, w_gu, b_gu, b_gu)


def _dn_kernel(be_ref, rows_ref, a_ref, w_ref, b_ref, o_ref):
    del be_ref

    def body(first):
        ak = [a_ref[first:, kc * MXU_DEPTH:(kc + 1) * MXU_DEPTH] for kc in range(D_FF // MXU_DEPTH)]
        hw = ROW_WORDS // 2
        for h in range(2):
            lo = slice(h * hw, (h + 1) * hw)
            hi = slice(ROW_WORDS + h * hw, ROW_WORDS + (h + 1) * hw)
            y_lo = _dot_f32_weights(ak, w_ref, lo) + b_ref[0, :, lo]
            y_hi = _dot_f32_weights(ak, w_ref, hi) + b_ref[0, :, hi]
            _store_row_tiles(o_ref, _pack_pair(y_lo, y_hi), first, h * (hw // LANES))

    _for_valid_rows(rows_ref[pl.program_id(0)], a_ref.shape[0], body, o_ref)


def _moe_down(block_e, blk_rows, act, w_dn, b_dn, bm):
    n_rows = act.shape[0]
    n_blocks = n_rows // bm
    grid_spec = pltpu.PrefetchScalarGridSpec(
        num_scalar_prefetch=2,
        grid=(n_blocks,),
        in_specs=[pl.BlockSpec((bm, D_FF), lambda j, be, nv: (j, 0)),
                  pl.BlockSpec((1, D_FF, D_MODEL), lambda j, be, nv: (be[j], 0, 0)),
                  pl.BlockSpec((1, 1, D_MODEL), lambda j, be, nv: (be[j], 0, 0))],
        out_specs=pl.BlockSpec((bm * ROW_TILE, LANES), lambda j, be, nv: (j, 0)))
    return pl.pallas_call(
        _dn_kernel,
        grid_spec=grid_spec,
        out_shape=jax.ShapeDtypeStruct((n_rows * ROW_TILE, LANES), jnp.int32),
        compiler_params=_cparams(("arbitrary",), VMEM_LIMIT),
        name="moe_down",
    )(block_e, blk_rows, act, w_dn, b_dn)


def _combine_kernel(dest_ref, next_ref, gate_ref, x1_ref, g_ref, b_ref, ys_hbm, o_ref, buf_a, buf_b, sem):
    i = pl.program_id(0)
    tt = x1_ref.shape[0]
    sub = tt // COMBINE_SUBBLOCKS
    bufs = (buf_a, buf_b)

    def start_rows(d_ref, s, t):
        for kk in range(TOP_K):
            row = pl.multiple_of((kk * tt + t) * ROW_TILE, ROW_TILE)
            pltpu.make_async_copy(ys_hbm.at[d_ref[t * TOP_K + kk]],
                                  bufs[s].at[pl.ds(row, ROW_TILE), :], sem.at[s]).start(priority=kk % 2)

    def wait_tile(s):
        for kk in range(TOP_K):
            view = bufs[s].at[pl.ds(kk * tt * ROW_TILE, tt * ROW_TILE), :]
            pltpu.make_async_copy(view, view, sem.at[s]).wait()

    @pl.when(i == 0)
    def _():
        def body(t, c):
            start_rows(dest_ref, 0, t)
            return c
        lax.fori_loop(0, tt, body, 0)

    def step(cur, nxt):
        wait_tile(cur)

        def reduce_rows(sb, c):
            r0 = pl.multiple_of(sb * sub, sub)
            for tl in range(sub):
                start_rows(next_ref, nxt, r0 + tl)
            rows = pl.ds(r0, sub)
            gates = gate_ref[rows, :]
            acc_lo = jnp.zeros((sub, ROW_WORDS), F32)
            acc_hi = jnp.zeros((sub, ROW_WORDS), F32)
            for kk in range(TOP_K):
                lo, hi = _unpack_row(_load_row_tiles(bufs[cur], kk * tt + r0, sub))
                gk = gates[:, kk:kk + 1]
                acc_lo = acc_lo + gk * lo
                acc_hi = acc_hi + gk * hi
            ffn = jnp.concatenate([acc_lo, acc_hi], axis=1)
            o_ref[rows, :] = _layer_norm(DEEPNORM_ALPHA * x1_ref[rows, :] + ffn, g_ref[...], b_ref[...])
            return c

        lax.fori_loop(0, COMBINE_SUBBLOCKS, reduce_rows, 0)

        @pl.when(i + 1 == pl.num_programs(0))
        def _():
            wait_tile(nxt)

    for cur in range(2):
        @pl.when(i % 2 == cur)
        def _(cur=cur):
            step(cur, 1 - cur)


def _combine_ln(dest_flat, gates, x1, g, b, ys3, tt):
    n = x1.shape[0]
    last = n // tt - 1
    const = lambda shape: pl.BlockSpec(shape, lambda i: (0, 0))
    return pl.pallas_call(
        _combine_kernel,
        grid=(n // tt,),
        in_specs=[pl.BlockSpec((tt * TOP_K,), lambda i: (i,), memory_space=pltpu.SMEM),
                  pl.BlockSpec((tt * TOP_K,), lambda i: (jnp.minimum(i + 1, last),), memory_space=pltpu.SMEM),
                  pl.BlockSpec((tt, LANES), lambda i: (i, 0)),
                  pl.BlockSpec((tt, D_MODEL), lambda i: (i, 0)),
                  const((1, D_MODEL)), const((1, D_MODEL)),
                  pl.BlockSpec(memory_space=pl.ANY)],
        out_specs=pl.BlockSpec((tt, D_MODEL), lambda i: (i, 0)),
        out_shape=jax.ShapeDtypeStruct((n, D_MODEL), F32),
        scratch_shapes=[pltpu.VMEM((TOP_K * tt * ROW_TILE, LANES), jnp.int32),
                        pltpu.VMEM((TOP_K * tt * ROW_TILE, LANES), jnp.int32),
                        pltpu.SemaphoreType.DMA((2,))],
        compiler_params=_cparams(("arbitrary",), VMEM_LIMIT),
        name="moe_combine_ln",
    )(dest_flat, dest_flat, gates, x1, g, b, ys3)


def _pad_cols(a, width):
    return jnp.pad(a, ((0, 0), (0, width - a.shape[1])))


def _lora_layout(a):
    dw = a[:, :DECAY_LORA]
    da = a[:, DECAY_LORA:DECAY_LORA + AAA_LORA]
    dg = a[:, DECAY_LORA + AAA_LORA:]
    return jnp.concatenate([_pad_cols(dw, LANES), _pad_cols(da, LANES), _pad_cols(dg, 2 * LANES)], axis=1)


def _moe_ffn(x1, x1p2, idx, gates, counts, w_gu, b_gu, w_dn, b_dn, ln_g, ln_b, bm, tn, tt):
    n = x1.shape[0]
    nk = n * TOP_K
    n_blocks = nk // bm + N_EXPERTS
    n_rows = n_blocks * bm
    cnt = counts[0, :N_EXPERTS].astype(jnp.int32)
    padded = ((cnt + bm - 1) // bm) * bm
    pends = jnp.cumsum(padded)
    lanes = lambda a: jnp.pad(a, (0, LANES - a.shape[0]))
    vstart = pends - cnt
    pad_off = lanes(jnp.concatenate([pends - padded, pends[-1:]])).astype(jnp.int32)
    pad_len = lanes(jnp.concatenate([padded - cnt, (n_rows - pends[-1:]) // (bm // 2)])).astype(jnp.int32)
    block_start = jnp.arange(n_blocks, dtype=jnp.int32)[:, None] * bm
    block_e = jnp.minimum(jnp.sum(pends[None, :] <= block_start, axis=1), N_EXPERTS - 1).astype(jnp.int32)
    owned = jnp.logical_and(block_start >= (pends - padded)[None, :], block_start < pends[None, :])
    blk_rows = jnp.sum(jnp.where(owned, jnp.clip(block_start + bm - vstart[None, :], 0, bm), 0),
                       axis=1).astype(jnp.int32)

    dest_flat = _row_dest(idx, lanes(vstart).astype(F32).reshape(1, LANES), tt)[:, :TOP_K].reshape(nk)
    xs3 = _dispatch(dest_flat, pad_off, pad_len, x1p2, n_rows, tt, bm)
    act = _moe_gate_up(block_e, blk_rows, xs3.reshape(n_rows * ROW_TILE, LANES), w_gu,
                       b_gu.reshape(N_EXPERTS, 1, 2 * D_FF), bm, tn)
    ys2 = _moe_down(block_e, blk_rows, act, w_dn, b_dn.reshape(N_EXPERTS, 1, D_MODEL), bm)
    return _combine_ln(dest_flat, gates, x1, ln_g, ln_b, ys2.reshape(n_rows, ROW_TILE, LANES), tt)


def _layer(x, w_in, shift_mu, w0, w_up, a0, a_up, g_up, k_k, k_a, r_k, gn_g, gn_b,
           lq1, lk1, lq2, lk2, subln_g, w_out, ln1_g, ln1_b,
           w_router, b_router, w_gu, b_gu, w_dn, b_dn, ln2_g, ln2_b, lambda_init,
           tm_in=2048, tseq=512, ng=4, tq=512, tm_out=512, bm=512, tn=1024, tt=256):
    batch, seq, d = x.shape
    n = batch * seq
    rw = 3 * RWKV_WIDTH
    rcols = rw + DECAY_LORA + AAA_LORA + GATE_LORA
    row = lambda a: a.reshape(1, -1)

    xf = x.reshape(n, d)
    xb = xf.astype(BF16)
    w_r = jnp.concatenate([w_in[:, :rw], _lora_layout(w_in[:, rw:rcols])], axis=1).astype(BF16)
    w_d = w_in[:, rcols:].astype(BF16)
    mu = jnp.concatenate([row(shift_mu)[:, :rw], _lora_layout(row(shift_mu)[:, rw:])], axis=1)
    p_r = _matmul(xb, w_r, F32, tm_in, 512)
    p_d = _matmul(xb, w_d, BF16, tm_in, 512)

    pad_rows = lambda a, rows: jnp.pad(a, ((0, rows - a.shape[0]), (0, 0))).astype(BF16)
    h_r = _rwkv(p_r, mu, row(w0), row(a0), row(k_k), row(k_a), row(r_k), row(gn_g), row(gn_b),
                pad_rows(w_up, LANES), pad_rows(a_up, LANES), pad_rows(g_up, 2 * LANES), batch, seq, tseq, ng)
    h_d = _diff_attention(p_d, row(lq1), row(lk1), row(lq2), row(lk2), row(subln_g), lambda_init,
                          batch, seq, tq)

    w_ob = w_out.astype(BF16)
    w_rp = _pad_cols(w_router, LANES)
    b_rp = jnp.concatenate([row(b_router), jnp.full((1, LANES - N_EXPERTS), NEG_BIG, F32)], axis=1)
    x1, x1p2, idx, gates, counts = _outproj_ln_router(
        h_r, h_d, xf, w_ob[:RWKV_WIDTH], w_ob[RWKV_WIDTH:], row(ln1_g), row(ln1_b), w_rp, b_rp, tm_out)
    out = _moe_ffn(x1, x1p2, idx, gates, counts, w_gu, b_gu, w_dn, b_dn, row(ln2_g), row(ln2_b), bm, tn, tt)
    return out.reshape(batch, seq, d)


def kernel(x, w_in, shift_mu, w0, w_up, a0, a_up, g_up, k_k, k_a, r_k, gn_g, gn_b, lq1, lk1, lq2, lk2,
           subln_g, w_out, ln1_g, ln1_b, w_router, b_router, w_gu, b_gu, w_dn, b_dn, ln2_g, ln2_b):
    for l in range(DEPTH):
        lambda_init = 0.8 - 0.6 * math.exp(-0.3 * l)
        x = _layer(x, w_in[l], shift_mu[l], w0[l], w_up[l], a0[l], a_up[l], g_up[l], k_k[l], k_a[l],
                   r_k[l], gn_g[l], gn_b[l], lq1[l], lk1[l], lq2[l], lk2[l], subln_g[l], w_out[l],
                   ln1_g[l], ln1_b[l], w_router[l], b_router[l], w_gu[l], b_gu[l], w_dn[l], b_dn[l],
                   ln2_g[l], ln2_b[l], lambda_init)
    return x
```

```python
import functools
import math

import jax
import jax.numpy as jnp
from jax import lax
from jax.experimental import pallas as pl
from jax.experimental.pallas import tpu as pltpu

F32 = jnp.float32
BF16 = jnp.bfloat16

D_MODEL = 2048
RWKV_HEAD = 64
RWKV_WIDTH = 1024
RWKV_HEADS = 16
DECAY_LORA = 64
AAA_LORA = 64
GATE_LORA = 160
DIFF_HEAD = 64
DIFF_VDIM = 128
DIFF_HEADS = 8
DIFF_WIDTH = 1024
N_EXPERTS = 32
TOP_K = 4
D_FF = 2048
SWIGLU_LIMIT = 7.0
SWIGLU_ALPHA = 1.702
LN_EPS = 1e-5
GN_EPS = RWKV_HEAD * 1e-5
RMS_EPS = 1e-5
NEG_BIG = -1e30
DEPTH = 1
DEEPNORM_ALPHA = (2.0 * DEPTH) ** 0.25

LANES = 128
SUBLANES = 8
ROW_WORDS = D_MODEL // 2
ROW_TILE = ROW_WORDS // LANES
RWKV_GROUP = 256
RWKV_CHUNK = 64
LORA_COLS = 512
VMEM_LIMIT = 56 * 1024 * 1024
HIGH_HALF = -65536
MXU_DEPTH = 256
OUTPROJ_PARTS = 2
MOE_ROW_PIECES = 4
COMBINE_SUBBLOCKS = 8
ATTN_ONES_ROWS = 16
LOG2E = 1.4426950408889634


def _cparams(sem, vmem=None):
    return pltpu.CompilerParams(dimension_semantics=sem, vmem_limit_bytes=vmem)


def _dot(a, b):
    return jnp.dot(a, b, preferred_element_type=F32)


def _dot_nt(a, b):
    return lax.dot_general(a, b, (((1,), (1,)), ((), ())), preferred_element_type=F32)


def _dot_tn(a, b):
    return lax.dot_general(a, b, (((0,), (0,)), ((), ())), preferred_element_type=F32)


def _split3(x):
    h = x.astype(BF16)
    r = x - h.astype(F32)
    m = r.astype(BF16)
    l = (r - m.astype(F32)).astype(BF16)
    return h, m, l


def _dot_split_rhs(x, ones):
    h = x.astype(BF16)
    l = (x - h.astype(F32)).astype(BF16)
    return _dot(h, ones) + _dot(l, ones)


def _dot_exact_lhs(ones, x):
    h, m, l = _split3(x)
    return _dot(ones, h) + _dot(ones, m) + _dot(ones, l)


def _sigmoid(x):
    return 1.0 / (1.0 + jnp.exp(-x))


def _matmul_kernel(x_ref, w_ref, o_ref):
    o_ref[...] = _dot(x_ref[...], w_ref[...]).astype(o_ref.dtype)


def _matmul(x, w, out_dtype, tm, tn):
    m, k = x.shape
    n = w.shape[1]
    return pl.pallas_call(
        _matmul_kernel,
        grid=(n // tn, m // tm),
        in_specs=[pl.BlockSpec((tm, k), lambda j, i: (i, 0)),
                  pl.BlockSpec((k, tn), lambda j, i: (0, j))],
        out_specs=pl.BlockSpec((tm, tn), lambda j, i: (i, j)),
        out_shape=jax.ShapeDtypeStruct((m, n), out_dtype),
        compiler_params=_cparams(("parallel", "parallel"), VMEM_LIMIT),
        name="in_proj",
    )(x, w)


def _rwkv_kernel(r_ref, k_ref, v_ref, l_ref, mur_ref, muk_ref, muv_ref, mul_ref,
                 w0_ref, a0_ref, kk_ref, ka_ref, rk_ref, gng_ref, gnb_ref,
                 wup_ref, aup_ref, gup_ref, o_ref,
                 pr_s, pk_s, pv_s, pl_s, state_s, r_s, w_s, k_s, v_s, a_s, b_s, g_s,
                 y_s, rc_p, lrb_p, lrk_p, tb_p, wc_p, akv_p, be_p, ke_p, dec_p):
    s = pl.program_id(2)
    T = r_ref.shape[0]
    G = RWKV_GROUP
    C = RWKV_CHUNK
    NG = r_ref.shape[1] // G
    groups = [slice(g * G, (g + 1) * G) for g in range(NG)]

    @pl.when(s == 0)
    def _():
        state_s[...] = jnp.zeros_like(state_s)
        pr_s[...] = jnp.zeros_like(pr_s)
        pk_s[...] = jnp.zeros_like(pk_s)
        pv_s[...] = jnp.zeros_like(pv_s)
        pl_s[...] = jnp.zeros_like(pl_s)

    row = lax.broadcasted_iota(jnp.int32, (T, 1), 0)

    def shift(ref, prev_s, mu_ref):
        p = ref[...]
        prev = jnp.where(row == 0, prev_s[...], pltpu.roll(p, 1, 0))
        prev_s[...] = p[T - 1:T, :]
        return p + (prev - p) * mu_ref[...]

    r = shift(r_ref, pr_s, mur_ref)
    k = shift(k_ref, pk_s, muk_ref)
    v = shift(v_ref, pv_s, muv_ref)
    lo = shift(l_ref, pl_s, mul_ref)
    dw = lo[:, 0:LANES]
    da = lo[:, LANES:2 * LANES]
    dg = lo[:, 2 * LANES:4 * LANES]

    wpre = w0_ref[...] + _dot(jnp.tanh(dw).astype(BF16), wup_ref[...])
    wlog = -math.exp(-0.5) * _sigmoid(wpre)
    a_sig = _sigmoid(a0_ref[...] + _dot(da.astype(BF16), aup_ref[...]))
    gate = _dot(_sigmoid(dg).astype(BF16), gup_ref[...])

    gi = lax.broadcasted_iota(jnp.int32, (G, G), 0)
    gj = lax.broadcasted_iota(jnp.int32, (G, G), 1)
    same_head = (gi // RWKV_HEAD) == (gj // RWKV_HEAD)
    head_ones = jnp.where(same_head, 1.0, 0.0).astype(BF16)

    def head_sum(x):
        return jnp.concatenate([_dot_split_rhs(x[:, g], head_ones) for g in groups], axis=1)

    kk = k * kk_ref[...]
    nrm = jnp.sqrt(head_sum(kk * kk))
    kk = kk / jnp.maximum(nrm, 1e-12)
    k2 = k * (1.0 + (a_sig - 1.0) * ka_ref[...])

    r_s[...] = r
    w_s[...] = wlog
    k_s[...] = k2
    v_s[...] = v
    a_s[...] = -kk
    b_s[...] = kk * a_sig
    g_s[...] = gate

    ci = lax.broadcasted_iota(jnp.int32, (C, C), 0)
    cj = lax.broadcasted_iota(jnp.int32, (C, C), 1)
    tri = jnp.where(cj <= ci, 1.0, 0.0).astype(BF16)
    mt = lax.broadcasted_iota(jnp.int32, (C, G), 0)
    mtp = lax.broadcasted_iota(jnp.int32, (C, G), 1) % C
    strict = mtp < mt
    incl = mtp <= mt
    ceye = jnp.where(mtp == mt, 1.0, 0.0)

    def bd(xc):
        return jnp.where(same_head, jnp.concatenate([xc, xc, xc, xc], axis=0), jnp.zeros((), BF16))

    def each(fn, *lists):
        return [fn(*xs) for xs in zip(*lists)]

    def bf(x):
        return x.astype(BF16)

    def prepare(cp, carry):
        streams = [(pl.ds(pl.multiple_of((2 * cp + h) * C, C), C), g) for h in range(2) for g in groups]
        rows = [2 * cp + h for h in range(2) for _ in groups]
        rc = [r_s[sl, g] for sl, g in streams]
        wc = [w_s[sl, g] for sl, g in streams]
        kc = [k_s[sl, g] for sl, g in streams]
        vc = [v_s[sl, g] for sl, g in streams]
        ac = [a_s[sl, g] for sl, g in streams]
        bc = [b_s[sl, g] for sl, g in streams]
        cum = each(lambda w: _dot_exact_lhs(tri, w), wc)
        tot = each(lambda x: x[C - 1:C, :], cum)
        ginv = each(lambda x: jnp.exp(-x), cum)
        gend = each(lambda x, t: jnp.exp(t - x), cum, tot)
        r_c = each(lambda x, g: bf(x * jnp.exp(g)), rc, cum)
        a_c = each(lambda x, g, w: bf(x * jnp.exp(g - w)), ac, cum, wc)
        v_c = each(bf, vc)
        k_bd = each(lambda x, g: bd(bf(x * g)), kc, ginv)
        b_bd = each(lambda x, g: bd(bf(x * g)), bc, ginv)
        ke_c = each(lambda x, g: bf(x * g), kc, gend)
        be_c = each(lambda x, g: bf(x * g), bc, gend)
        v_bd = each(bd, v_c)
        a_bd = each(bd, a_c)

        ar = each(lambda a, r: jnp.concatenate([a, r], axis=0), a_c, r_c)
        arb = each(_dot_nt, ar, b_bd)
        ark = each(_dot_nt, ar, k_bd)
        l_ab = each(lambda x: jnp.where(strict, x[:C], 0.0), arb)
        l_ak = each(lambda x: bf(jnp.where(strict, x[:C], 0.0)), ark)
        l_rb = each(lambda x: bf(jnp.where(incl, x[C:], 0.0)), arb)
        l_rk = each(lambda x: bf(jnp.where(incl, x[C:], 0.0)), ark)

        p = each(bf, l_ab)
        p_bd = each(bd, p)
        tinv = each(lambda x: ceye + x, l_ab)
        for _ in range(int(math.log2(C)) - 1):
            p = each(lambda x, y: bf(_dot(x, y)), p, p_bd)
            p_bd = each(bd, p)
            tinv = each(lambda t, y: t + _dot(bf(t), y), tinv, p_bd)
        tb = each(bf, tinv)
        akv_c = each(lambda a, b: bf(_dot(a, b)), l_ak, v_bd)
        w_c = each(lambda a, b: bf(_dot(a, b)), tb, a_bd)
        for ref, vals in zip((rc_p, lrb_p, lrk_p, tb_p, wc_p, akv_p, be_p, ke_p),
                             (r_c, l_rb, l_rk, tb, w_c, akv_c, be_c, ke_c)):
            for (sl, g), val in zip(streams, vals):
                ref[sl, g] = val
        for (_, g), row, t in zip(streams, rows, tot):
            dec_p[pl.ds(row, 1), g] = jnp.exp(t)
        return carry

    lax.fori_loop(0, T // C // 2, prepare, 0)

    def chunk(c, carry):
        sl = pl.ds(pl.multiple_of(c * C, C), C)
        load = lambda ref: [ref[sl, g] for g in groups]
        r_c, l_rb, l_rk, tb, w_c, be_c, ke_c = (load(ref) for ref in
                                                (rc_p, lrb_p, lrk_p, tb_p, wc_p, be_p, ke_p))
        akv_bd = each(bd, load(akv_p))
        v_c = each(bf, load(v_s))
        v_bd = each(bd, v_c)
        dec = [dec_p[pl.ds(c, 1), g] for g in groups]
        st = [state_s[g] for g in range(NG)]
        stb = each(bf, st)
        u_c = each(lambda w, s0, t, x: bf(_dot_nt(w, s0) + _dot(t, x)), w_c, stb, tb, akv_bd)
        u_bd = each(bd, u_c)
        y = each(lambda r, s0, lb, lk, u, vv: _dot_nt(r, s0) + _dot(jnp.concatenate([lb, lk], axis=1),
                                                                    jnp.concatenate([u, vv], axis=0)),
                 r_c, stb, l_rb, l_rk, u_bd, v_bd)
        new = each(lambda s0, t, u, vv, b, kx: s0 * t + jnp.where(
            same_head, _dot_tn(jnp.concatenate([u, vv], axis=0), jnp.concatenate([b, kx], axis=0)), 0.0),
            st, dec, u_c, v_c, be_c, ke_c)
        for g in range(NG):
            state_s[g] = new[g]
        for g, yy in zip(groups, y):
            y_s[sl, g] = yy
        return carry

    lax.fori_loop(0, T // C, chunk, 0)

    def finish(cp, carry):
        streams = [(pl.ds(pl.multiple_of((2 * cp + h) * C, C), C), g) for h in range(2) for g in groups]
        y = [y_s[sl, g] for sl, g in streams]
        mean = each(lambda x: _dot_split_rhs(x, head_ones) * (1.0 / RWKV_HEAD), y)
        d = each(lambda x, m: x - m, y, mean)
        var = each(lambda x: _dot_split_rhs(x * x, head_ones) * (1.0 / RWKV_HEAD), d)
        bonus = [_dot_split_rhs(r_s[sl, g] * k_s[sl, g] * rk_ref[:, g], head_ones) * v_s[sl, g]
                 for sl, g in streams]
        for (sl, g), dd, vr, bo in zip(streams, d, var, bonus):
            yn = dd * lax.rsqrt(vr + GN_EPS) * gng_ref[:, g] + gnb_ref[:, g]
            o_ref[sl, g] = ((yn + bo) * g_s[sl, g]).astype(o_ref.dtype)
        return carry

    lax.fori_loop(0, T // C // 2, finish, 0)


def _rwkv(p_r, mu, w0, a0, k_k, k_a, r_k, gn_g, gn_b, w_up, a_up, g_up, batch, seq, tseq, ng):
    n = batch * seq
    G = ng * RWKV_GROUP
    nq = RWKV_WIDTH // G
    ns = seq // tseq
    lora_blk = 3 * RWKV_WIDTH // LORA_COLS

    def tok(off):
        return pl.BlockSpec((tseq, G), lambda b, q, s: (b * ns + s, off + q))

    def par(off):
        return pl.BlockSpec((1, G), lambda b, q, s: (0, off + q))

    in_specs = [
        tok(0), tok(nq), tok(2 * nq),
        pl.BlockSpec((tseq, LORA_COLS), lambda b, q, s: (b * ns + s, lora_blk)),
        par(0), par(nq), par(2 * nq),
        pl.BlockSpec((1, LORA_COLS), lambda b, q, s: (0, lora_blk)),
        par(0), par(0), par(0), par(0), par(0), par(0), par(0),
        pl.BlockSpec((LANES, G), lambda b, q, s: (0, q)),
        pl.BlockSpec((LANES, G), lambda b, q, s: (0, q)),
        pl.BlockSpec((2 * LANES, G), lambda b, q, s: (0, q)),
    ]
    scratch = [pltpu.VMEM((1, G), F32), pltpu.VMEM((1, G), F32), pltpu.VMEM((1, G), F32),
               pltpu.VMEM((1, LORA_COLS), F32), pltpu.VMEM((ng, RWKV_GROUP, RWKV_GROUP), F32)]
    scratch += [pltpu.VMEM((tseq, G), F32) for _ in range(8)]
    scratch += [pltpu.VMEM((tseq, G), BF16) for _ in range(8)]
    scratch += [pltpu.VMEM((tseq // RWKV_CHUNK, G), F32)]
    return pl.pallas_call(
        _rwkv_kernel,
        grid=(batch, nq, ns),
        in_specs=in_specs,
        out_specs=pl.BlockSpec((tseq, G), lambda b, q, s: (b * ns + s, q)),
        out_shape=jax.ShapeDtypeStruct((n, RWKV_WIDTH), BF16),
        scratch_shapes=scratch,
        compiler_params=_cparams(("parallel", "parallel", "arbitrary"), VMEM_LIMIT),
        name="rwkv7",
    )(p_r, p_r, p_r, p_r, mu, mu, mu, mu, w0, a0, k_k, k_a, r_k, gn_g, gn_b, w_up, a_up, g_up)


def _attn_kernel(q_ref, k_ref, v_ref, lq1_ref, lk1_ref, lq2_ref, lk2_ref, g_ref, o_ref, vt_s, *, lambda_init):
    i = pl.program_id(2)
    tq = q_ref.shape[0]

    @pl.when(i == 0)
    def _():
        ones = jnp.ones((ATTN_ONES_ROWS, tq), BF16)
        for j in range(vt_s.shape[0]):
            vt = v_ref[j * tq:(j + 1) * tq, :].astype(F32).T.astype(BF16)
            vt_s[j] = jnp.concatenate([vt, ones], axis=0)

    drow = lax.broadcasted_iota(jnp.int32, (DIFF_VDIM, 1), 0)
    qt = (q_ref[...].astype(F32) * (DIFF_HEAD ** -0.5 * LOG2E)).T
    q1 = jnp.where(drow < DIFF_HEAD, qt, 0.0).astype(BF16)
    q2 = jnp.where(drow >= DIFF_HEAD, qt, 0.0).astype(BF16)
    lam = (jnp.exp(jnp.sum(lq1_ref[...] * lk1_ref[...], axis=-1, keepdims=True))
           - jnp.exp(jnp.sum(lq2_ref[...] * lk2_ref[...], axis=-1, keepdims=True)) + lambda_init)

    key = lax.broadcasted_iota(jnp.int32, (tq, tq), 0)
    qry = lax.broadcasted_iota(jnp.int32, (tq, tq), 1)

    def scores(j, diagonal):
        kj = k_ref[j * tq:(j + 1) * tq, :]
        sc = (_dot(kj, q1), _dot(kj, q2))
        if diagonal:
            sc = tuple(jnp.where(key <= qry, s, NEG_BIG) for s in sc)
        return sc

    def absorb(j, sc, stats):
        m, acc = stats[0:2], stats[2:4]
        vtj = vt_s[j]
        m_new = [jnp.maximum(mm, jnp.max(s, axis=0, keepdims=True)) for mm, s in zip(m, sc)]
        alpha = [jnp.exp2(mm - mn) for mm, mn in zip(m, m_new)]
        p = [jnp.exp2(s - mn).astype(BF16) for s, mn in zip(sc, m_new)]
        pv = [_dot(vtj, pp) for pp in p]
        acc = [a * ac + x for a, ac, x in zip(alpha, acc, pv)]
        return tuple(m_new) + tuple(acc)

    neg1 = jnp.full((1, tq), NEG_BIG, F32)
    zacc = jnp.zeros((DIFF_VDIM + ATTN_ONES_ROWS, tq), F32)
    init = (neg1, neg1, zacc, zacc)

    for iv in range(vt_s.shape[0]):
        @pl.when(i == iv)
        def _(iv=iv):
            stats = init
            sc = scores(0, iv == 0)
            for j in range(iv + 1):
                nxt = scores(j + 1, j + 1 == iv) if j < iv else None
                stats = absorb(j, sc, stats)
                sc = nxt
            acc = [a[:DIFF_VDIM] for a in stats[2:4]]
            l = [a[DIFF_VDIM:DIFF_VDIM + 1] for a in stats[2:4]]
            o = acc[0] / l[0] - lam * (acc[1] / l[1])
            o = o * lax.rsqrt(jnp.mean(o * o, axis=0, keepdims=True) + RMS_EPS) * g_ref[...]
            o_ref[...] = (o * (1.0 - lambda_init)).T.astype(o_ref.dtype)


def _diff_attention(p_d, lq1, lk1, lq2, lk2, subln_g, lambda_init, batch, seq, tq):
    n = batch * seq
    nq = seq // tq
    H = DIFF_HEADS
    small = pl.BlockSpec((1, DIFF_HEAD), lambda b, h, i: (0, 0))
    return pl.pallas_call(
        functools.partial(_attn_kernel, lambda_init=lambda_init),
        grid=(batch, H, nq),
        in_specs=[pl.BlockSpec((tq, DIFF_VDIM), lambda b, h, i: (b * nq + i, h)),
                  pl.BlockSpec((seq, DIFF_VDIM), lambda b, h, i: (b, H + h)),
                  pl.BlockSpec((seq, DIFF_VDIM), lambda b, h, i: (b, 2 * H + h)),
                  small, small, small, small,
                  pl.BlockSpec((DIFF_VDIM, 1), lambda b, h, i: (0, 0))],
        out_specs=pl.BlockSpec((tq, DIFF_VDIM), lambda b, h, i: (b * nq + i, h)),
        out_shape=jax.ShapeDtypeStruct((n, DIFF_WIDTH), BF16),
        scratch_shapes=[pltpu.VMEM((nq, DIFF_VDIM + ATTN_ONES_ROWS, tq), BF16)],
        compiler_params=_cparams(("parallel", "parallel", "arbitrary"), VMEM_LIMIT),
        name="diff_attn",
    )(p_d, p_d, p_d, lq1, lk1, lq2, lk2, subln_g.reshape(DIFF_VDIM, 1))


def _layer_norm(y, g, b):
    mu = jnp.mean(y, axis=-1, keepdims=True)
    d = y - mu
    var = jnp.mean(d * d, axis=-1, keepdims=True)
    return d * lax.rsqrt(var + LN_EPS) * g + b


def _store_row_tiles(ref, words, start=0, sublane=0):
    rows = words.shape[0]
    for s in range(words.shape[1] // LANES):
        ref[pl.ds(start * ROW_TILE + sublane + s, rows, stride=ROW_TILE), :] = words[:, s * LANES:(s + 1) * LANES]


def _load_row_tiles(ref, start, rows):
    parts = [ref[pl.ds(start * ROW_TILE + s, rows, stride=ROW_TILE), :] for s in range(ROW_TILE)]
    return jnp.concatenate(parts, axis=1)


def _pack_pair(lo, hi):
    bits = lambda y: lax.bitcast_convert_type(y.astype(BF16).astype(F32), jnp.int32)
    return lax.shift_right_logical(bits(lo), 16) | (bits(hi) & HIGH_HALF)


def _pack_row(y):
    return _pack_pair(y[:, :ROW_WORDS], y[:, ROW_WORDS:])


def _unpack_row(words):
    lo = lax.bitcast_convert_type(lax.shift_left(words, 16), F32)
    hi = lax.bitcast_convert_type(words & HIGH_HALF, F32)
    return lo, hi


def _expert_onehots(idx):
    lane = lax.broadcasted_iota(jnp.int32, idx.shape, 1)
    sels = [lane == idx[:, kk:kk + 1] for kk in range(TOP_K)]
    onehot = jnp.zeros(idx.shape, F32)
    for sel in sels:
        onehot = onehot + jnp.where(sel, 1.0, 0.0)
    return sels, onehot


def _outproj_kernel(hr_ref, hd_ref, x_ref, wt_ref, wb_ref, g_ref, b_ref, wrh_ref, wrl_ref, br_ref,
                    x1_ref, x1p_ref, idx_ref, gate_ref, cnt_ref, carry_s):
    i = pl.program_id(0)
    tm = x_ref.shape[0]
    th = tm // OUTPROJ_PARTS
    parts = [pl.ds(h * th, th) for h in range(OUTPROJ_PARTS)]

    @pl.when(i == 0)
    def _():
        carry_s[...] = jnp.zeros_like(carry_s)

    def each(fn, *lists):
        return [fn(*xs) for xs in zip(*lists)]

    mix = [_dot(hr_ref[p, :], wt_ref[...]) + _dot(hd_ref[p, :], wb_ref[...]) for p in parts]
    x1 = [_layer_norm(DEEPNORM_ALPHA * x_ref[p, :] + m, g_ref[...], b_ref[...]) for p, m in zip(parts, mix)]
    for h, (p, y) in enumerate(zip(parts, x1)):
        x1_ref[p, :] = y
        _store_row_tiles(x1p_ref, _pack_row(y), h * th)

    xh = each(lambda y: y.astype(BF16), x1)
    xl = each(lambda y, hh: (y - hh.astype(F32)).astype(BF16), x1, xh)
    work = each(lambda hh, ll: _dot(hh, wrh_ref[...]) + _dot(ll, wrh_ref[...]) + _dot(hh, wrl_ref[...])
                + br_ref[...], xh, xl)
    lane = lax.broadcasted_iota(jnp.int32, (th, LANES), 1).astype(F32)
    onehot = [jnp.zeros((th, LANES), F32) for _ in parts]
    vals, idxs = [], []
    for _ in range(TOP_K):
        mx = each(lambda w: jnp.max(w, axis=-1, keepdims=True), work)
        idx = each(lambda w, m: jnp.min(jnp.where(w == m, lane, float(LANES)), axis=-1, keepdims=True), work, mx)
        sel = each(lambda ix: lane == ix, idx)
        work = each(lambda s, w: jnp.where(s, -jnp.inf, w), sel, work)
        onehot = each(lambda o, s: o + jnp.where(s, 1.0, 0.0), onehot, sel)
        vals.append(mx)
        idxs.append(idx)
    total = carry_s[0:1, :]
    for h, p in enumerate(parts):
        exps = [jnp.exp(vv[h] - vals[0][h]) for vv in vals]
        den = exps[0] + exps[1] + exps[2] + exps[3]
        idx_out = jnp.zeros((th, LANES), F32)
        gate_out = jnp.zeros((th, LANES), F32)
        for kk in range(TOP_K):
            slot = lane == float(kk)
            idx_out = jnp.where(slot, idxs[kk][h], idx_out)
            gate_out = jnp.where(slot, exps[kk] / den, gate_out)
        idx_ref[p, :] = idx_out.astype(jnp.int32)
        gate_ref[p, :] = gate_out
        total = total + jnp.sum(onehot[h], axis=0, keepdims=True)
    carry_s[...] = jnp.broadcast_to(total, carry_s.shape)
    cnt_ref[...] = jnp.broadcast_to(total, cnt_ref.shape)


def _outproj_ln_router(hr, hd, x, w_top, w_bot, g, b, w_r, b_r, tm):
    n = x.shape[0]
    w_rh = w_r.astype(BF16)
    w_rl = (w_r - w_rh.astype(F32)).astype(BF16)
    const = lambda shape: pl.BlockSpec(shape, lambda i: (0, 0))
    rowb = lambda cols: pl.BlockSpec((tm, cols), lambda i: (i, 0))
    out_shape = (jax.ShapeDtypeStruct((n, D_MODEL), F32),
                 jax.ShapeDtypeStruct((n * ROW_TILE, LANES), jnp.int32),
                 jax.ShapeDtypeStruct((n, LANES), jnp.int32),
                 jax.ShapeDtypeStruct((n, LANES), F32),
                 jax.ShapeDtypeStruct((SUBLANES, LANES), F32))
    return pl.pallas_call(
        _outproj_kernel,
        grid=(n // tm,),
        in_specs=[rowb(RWKV_WIDTH), rowb(DIFF_WIDTH), rowb(D_MODEL),
                  const((RWKV_WIDTH, D_MODEL)), const((DIFF_WIDTH, D_MODEL)),
                  const((1, D_MODEL)), const((1, D_MODEL)),
                  const((D_MODEL, LANES)), const((D_MODEL, LANES)), const((1, LANES))],
        out_specs=(rowb(D_MODEL), pl.BlockSpec((tm * ROW_TILE, LANES), lambda i: (i, 0)),
                   rowb(LANES), rowb(LANES), const((SUBLANES, LANES))),
        out_shape=out_shape,
        scratch_shapes=[pltpu.VMEM((SUBLANES, LANES), F32)],
        compiler_params=_cparams(("arbitrary",), VMEM_LIMIT),
        name="outproj_ln_router",
    )(hr, hd, x, w_top, w_bot, g, b, w_rh, w_rl, b_r)


def _dest_kernel(idx_ref, ps_ref, dest_ref, carry_s):
    i = pl.program_id(0)
    tm = idx_ref.shape[0]

    @pl.when(i == 0)
    def _():
        carry_s[...] = jnp.broadcast_to(ps_ref[...], carry_s.shape)

    sels, onehot = _expert_onehots(idx_ref[...])
    ti = lax.broadcasted_iota(jnp.int32, (tm, tm), 0)
    tj = lax.broadcasted_iota(jnp.int32, (tm, tm), 1)
    before = jnp.where(tj < ti, 1.0, 0.0).astype(BF16)
    cum = _dot(before, onehot.astype(BF16)) + carry_s[0:1, :]
    lane = lax.broadcasted_iota(jnp.int32, (tm, LANES), 1)
    dest = jnp.zeros((tm, LANES), F32)
    for kk in range(TOP_K):
        dk = jnp.sum(jnp.where(sels[kk], cum, 0.0), axis=-1, keepdims=True)
        dest = jnp.where(lane == kk, dk, dest)
    dest_ref[...] = dest.astype(jnp.int32)
    total = carry_s[0:1, :] + jnp.sum(onehot, axis=0, keepdims=True)
    carry_s[...] = jnp.broadcast_to(total, carry_s.shape)


def _row_dest(idx, pstarts, tm):
    n = idx.shape[0]
    return pl.pallas_call(
        _dest_kernel,
        grid=(n // tm,),
        in_specs=[pl.BlockSpec((tm, LANES), lambda i: (i, 0)), pl.BlockSpec((1, LANES), lambda i: (0, 0))],
        out_specs=pl.BlockSpec((tm, LANES), lambda i: (i, 0)),
        out_shape=jax.ShapeDtypeStruct((n, LANES), jnp.int32),
        scratch_shapes=[pltpu.VMEM((SUBLANES, LANES), F32)],
        compiler_params=_cparams(("arbitrary",)),
        name="moe_row_dest",
    )(idx, pstarts)


def _wait_rows(ref, rows, sem):
    view = ref.at[pl.ds(0, rows)]
    pltpu.make_async_copy(view, view, sem).wait()


def _dispatch_kernel(dest_ref, padoff_ref, padlen_ref, x_ref, xs_hbm, zero_s, sem, zsem):
    tt = x_ref.shape[0] // ROW_TILE

    def pad_copies(e, go):
        off = padoff_ref[e]
        ln = padlen_ref[e]
        for bit in range(zero_s.shape[0].bit_length()):
            size = 1 << bit

            @pl.when((ln >> bit) & 1 == 1)
            def _():
                go(pltpu.make_async_copy(zero_s.at[pl.ds(0, size)],
                                         xs_hbm.at[pl.ds(off + (ln & (size - 1)), size)], zsem))

    def tail_copy(piece):
        rows = zero_s.shape[0]
        first = pl.multiple_of(padoff_ref[N_EXPERTS] + piece * rows, rows)
        return pltpu.make_async_copy(zero_s, xs_hbm.at[pl.ds(first, rows)], zsem)

    @pl.when(pl.program_id(0) == 0)
    def _():
        zero_s[...] = jnp.zeros_like(zero_s)
        for go in (lambda cp: cp.start(), lambda cp: cp.wait()):
            def body(e, c, go=go):
                pad_copies(e, go)
                return c
            lax.fori_loop(0, N_EXPERTS, body, 0)

            def tail(piece, c, go=go):
                go(tail_copy(piece))
                return c
            lax.fori_loop(0, padlen_ref[N_EXPERTS], tail, 0)

    def issue(t, c):
        src = x_ref.at[pl.ds(pl.multiple_of(t * ROW_TILE, ROW_TILE), ROW_TILE), :]
        for kk in range(TOP_K):
            pltpu.make_async_copy(src, xs_hbm.at[dest_ref[t * TOP_K + kk]], sem).start(priority=kk % 2)
        return c

    lax.fori_loop(0, tt, issue, 0)
    for _ in range(TOP_K):
        _wait_rows(xs_hbm, tt, sem)


def _dispatch(dest_flat, pad_off, pad_len, x1p2, n_rows, tt, bm):
    n = x1p2.shape[0] // ROW_TILE
    table = pl.BlockSpec((LANES,), lambda i: (0,), memory_space=pltpu.SMEM)
    return pl.pallas_call(
        _dispatch_kernel,
        grid=(n // tt,),
        in_specs=[pl.BlockSpec((tt * TOP_K,), lambda i: (i,), memory_space=pltpu.SMEM), table, table,
                  pl.BlockSpec((tt * ROW_TILE, LANES), lambda i: (i, 0))],
        out_specs=pl.BlockSpec(memory_space=pl.ANY),
        out_shape=jax.ShapeDtypeStruct((n_rows, ROW_TILE, LANES), jnp.int32),
        scratch_shapes=[pltpu.VMEM((bm // 2, ROW_TILE, LANES), jnp.int32),
                        pltpu.SemaphoreType.DMA, pltpu.SemaphoreType.DMA],
        compiler_params=_cparams(("arbitrary",)),
        name="moe_dispatch",
    )(dest_flat, pad_off, pad_len, x1p2)


def _dot_f32_weights(xk, w_ref, cols):
    acc = None
    for kc, x in enumerate(xk):
        part = _dot(x, w_ref[0, kc * MXU_DEPTH:(kc + 1) * MXU_DEPTH, cols].astype(BF16))
        acc = part if acc is None else acc + part
    return acc


def _for_valid_rows(nr, bm, body, o_ref):
    piece = bm // MOE_ROW_PIECES
    out_rows_per_row = o_ref.shape[0] // bm
    for q in range(1, MOE_ROW_PIECES + 1):
        @pl.when(jnp.logical_and(nr > (q - 1) * piece, nr <= q * piece))
        def _(first=bm - q * piece):
            body(first)
            if first > 0:
                o_ref[0:first * out_rows_per_row, :] = jnp.zeros(
                    (first * out_rows_per_row, o_ref.shape[1]), o_ref.dtype)

    @pl.when(nr == 0)
    def _():
        o_ref[...] = jnp.zeros_like(o_ref)


def _gu_kernel(be_ref, rows_ref, xs_ref, wg_ref, wu_ref, bg_ref, bu_ref, o_ref):
    del be_ref

    bm = o_ref.shape[0]

    def body(first):
        lo, hi = _unpack_row(_load_row_tiles(xs_ref, first, bm - first))
        x = jnp.concatenate([lo, hi], axis=1).astype(BF16)
        xk = [x[:, kc * MXU_DEPTH:(kc + 1) * MXU_DEPTH] for kc in range(D_MODEL // MXU_DEPTH)]
        g = _dot_f32_weights(xk, wg_ref, slice(None)) + bg_ref[0]
        u = _dot_f32_weights(xk, wu_ref, slice(None)) + bu_ref[0]
        g = jnp.minimum(g, SWIGLU_LIMIT)
        u = jnp.clip(u, -SWIGLU_LIMIT, SWIGLU_LIMIT)
        o_ref[first:, :] = ((u + 1.0) * (g * _sigmoid(SWIGLU_ALPHA * g))).astype(o_ref.dtype)

    _for_valid_rows(rows_ref[pl.program_id(1)], bm, body, o_ref)


def _moe_gate_up(block_e, blk_rows, xs2, w_gu, b_gu, bm, tn):
    n_rows = xs2.shape[0] // ROW_TILE
    n_blocks = n_rows // bm
    nt = D_FF // tn
    grid_spec = pltpu.PrefetchScalarGridSpec(
        num_scalar_prefetch=2,
        grid=(nt, n_blocks),
        in_specs=[pl.BlockSpec((bm * ROW_TILE, LANES), lambda n, j, be, nv: (j, 0)),
                  pl.BlockSpec((1, D_MODEL, tn), lambda n, j, be, nv: (be[j], 0, n)),
                  pl.BlockSpec((1, D_MODEL, tn), lambda n, j, be, nv: (be[j], 0, nt + n)),
                  pl.BlockSpec((1, 1, tn), lambda n, j, be, nv: (be[j], 0, n)),
                  pl.BlockSpec((1, 1, tn), lambda n, j, be, nv: (be[j], 0, nt + n))],
        out_specs=pl.BlockSpec((bm, tn), lambda n, j, be, nv: (j, n)))
    return pl.pallas_call(
        _gu_kernel,
        grid_spec=grid_spec,
        out_shape=jax.ShapeDtypeStruct((n_rows, D_FF), BF16),
        compiler_params=_cparams(("arbitrary", "arbitrary"), VMEM_LIMIT),
        name="moe_gate_up",
    )(block_e, blk_rows, xs2, w_gu, w_gu, b_gu, b_gu)


def _dn_kernel(be_ref, rows_ref, a_ref, w_ref, b_ref, o_ref):
    del be_ref

    def body(first):
        ak = [a_ref[first:, kc * MXU_DEPTH:(kc + 1) * MXU_DEPTH] for kc in range(D_FF // MXU_DEPTH)]
        hw = ROW_WORDS // 2
        for h in range(2):
            lo = slice(h * hw, (h + 1) * hw)
            hi = slice(ROW_WORDS + h * hw, ROW_WORDS + (h + 1) * hw)
            y_lo = _dot_f32_weights(ak, w_ref, lo) + b_ref[0, :, lo]
            y_hi = _dot_f32_weights(ak, w_ref, hi) + b_ref[0, :, hi]
            _store_row_tiles(o_ref, _pack_pair(y_lo, y_hi), first, h * (hw // LANES))

    _for_valid_rows(rows_ref[pl.program_id(0)], a_ref.shape[0], body, o_ref)


def _moe_down(block_e, blk_rows, act, w_dn, b_dn, bm):
    n_rows = act.shape[0]
    n_blocks = n_rows // bm
    grid_spec = pltpu.PrefetchScalarGridSpec(
        num_scalar_prefetch=2,
        grid=(n_blocks,),
        in_specs=[pl.BlockSpec((bm, D_FF), lambda j, be, nv: (j, 0)),
                  pl.BlockSpec((1, D_FF, D_MODEL), lambda j, be, nv: (be[j], 0, 0)),
                  pl.BlockSpec((1, 1, D_MODEL), lambda j, be, nv: (be[j], 0, 0))],
        out_specs=pl.BlockSpec((bm * ROW_TILE, LANES), lambda j, be, nv: (j, 0)))
    return pl.pallas_call(
        _dn_kernel,
        grid_spec=grid_spec,
        out_shape=jax.ShapeDtypeStruct((n_rows * ROW_TILE, LANES), jnp.int32),
        compiler_params=_cparams(("arbitrary",), VMEM_LIMIT),
        name="moe_down",
    )(block_e, blk_rows, act, w_dn, b_dn)


def _combine_kernel(dest_ref, next_ref, gate_ref, x1_ref, g_ref, b_ref, ys_hbm, o_ref, buf_a, buf_b, sem):
    i = pl.program_id(0)
    tt = x1_ref.shape[0]
    sub = tt // COMBINE_SUBBLOCKS
    bufs = (buf_a, buf_b)

    def start_rows(d_ref, s, t):
        for kk in range(TOP_K):
            row = pl.multiple_of((kk * tt + t) * ROW_TILE, ROW_TILE)
            pltpu.make_async_copy(ys_hbm.at[d_ref[t * TOP_K + kk]],
                                  bufs[s].at[pl.ds(row, ROW_TILE), :], sem.at[s]).start(priority=kk % 2)

    def wait_tile(s):
        for kk in range(TOP_K):
            view = bufs[s].at[pl.ds(kk * tt * ROW_TILE, tt * ROW_TILE), :]
            pltpu.make_async_copy(view, view, sem.at[s]).wait()

    @pl.when(i == 0)
    def _():
        def body(t, c):
            start_rows(dest_ref, 0, t)
            return c
        lax.fori_loop(0, tt, body, 0)

    def step(cur, nxt):
        wait_tile(cur)

        def reduce_rows(sb, c):
            r0 = pl.multiple_of(sb * sub, sub)
            for tl in range(sub):
                start_rows(next_ref, nxt, r0 + tl)
            rows = pl.ds(r0, sub)
            gates = gate_ref[rows, :]
            acc_lo = jnp.zeros((sub, ROW_WORDS), F32)
            acc_hi = jnp.zeros((sub, ROW_WORDS), F32)
            for kk in range(TOP_K):
                lo, hi = _unpack_row(_load_row_tiles(bufs[cur], kk * tt + r0, sub))
                gk = gates[:, kk:kk + 1]
                acc_lo = acc_lo + gk * lo
                acc_hi = acc_hi + gk * hi
            ffn = jnp.concatenate([acc_lo, acc_hi], axis=1)
            o_ref[rows, :] = _layer_norm(DEEPNORM_ALPHA * x1_ref[rows, :] + ffn, g_ref[...], b_ref[...])
            return c

        lax.fori_loop(0, COMBINE_SUBBLOCKS, reduce_rows, 0)

        @pl.when(i + 1 == pl.num_programs(0))
        def _():
            wait_tile(nxt)

    for cur in range(2):
        @pl.when(i % 2 == cur)
        def _(cur=cur):
            step(cur, 1 - cur)


def _combine_ln(dest_flat, gates, x1, g, b, ys3, tt):
    n = x1.shape[0]
    last = n // tt - 1
    const = lambda shape: pl.BlockSpec(shape, lambda i: (0, 0))
    return pl.pallas_call(
        _combine_kernel,
        grid=(n // tt,),
        in_specs=[pl.BlockSpec((tt * TOP_K,), lambda i: (i,), memory_space=pltpu.SMEM),
                  pl.BlockSpec((tt * TOP_K,), lambda i: (jnp.minimum(i + 1, last),), memory_space=pltpu.SMEM),
                  pl.BlockSpec((tt, LANES), lambda i: (i, 0)),
                  pl.BlockSpec((tt, D_MODEL), lambda i: (i, 0)),
                  const((1, D_MODEL)), const((1, D_MODEL)),
                  pl.BlockSpec(memory_space=pl.ANY)],
        out_specs=pl.BlockSpec((tt, D_MODEL), lambda i: (i, 0)),
        out_shape=jax.ShapeDtypeStruct((n, D_MODEL), F32),
        scratch_shapes=[pltpu.VMEM((TOP_K * tt * ROW_TILE, LANES), jnp.int32),
                        pltpu.VMEM((TOP_K * tt * ROW_TILE, LANES), jnp.int32),
                        pltpu.SemaphoreType.DMA((2,))],
        compiler_params=_cparams(("arbitrary",), VMEM_LIMIT),
        name="moe_combine_ln",
    )(dest_flat, dest_flat, gates, x1, g, b, ys3)


def _pad_cols(a, width):
    return jnp.pad(a, ((0, 0), (0, width - a.shape[1])))


def _lora_layout(a):
    dw = a[:, :DECAY_LORA]
    da = a[:, DECAY_LORA:DECAY_LORA + AAA_LORA]
    dg = a[:, DECAY_LORA + AAA_LORA:]
    return jnp.concatenate([_pad_cols(dw, LANES), _pad_cols(da, LANES), _pad_cols(dg, 2 * LANES)], axis=1)


def _moe_ffn(x1, x1p2, idx, gates, counts, w_gu, b_gu, w_dn, b_dn, ln_g, ln_b, bm, tn, tt):
    n = x1.shape[0]
    nk = n * TOP_K
    n_blocks = nk // bm + N_EXPERTS
    n_rows = n_blocks * bm
    cnt = counts[0, :N_EXPERTS].astype(jnp.int32)
    padded = ((cnt + bm - 1) // bm) * bm
    pends = jnp.cumsum(padded)
    lanes = lambda a: jnp.pad(a, (0, LANES - a.shape[0]))
    vstart = pends - cnt
    pad_off = lanes(jnp.concatenate([pends - padded, pends[-1:]])).astype(jnp.int32)
    pad_len = lanes(jnp.concatenate([padded - cnt, (n_rows - pends[-1:]) // (bm // 2)])).astype(jnp.int32)
    block_start = jnp.arange(n_blocks, dtype=jnp.int32)[:, None] * bm
    block_e = jnp.minimum(jnp.sum(pends[None, :] <= block_start, axis=1), N_EXPERTS - 1).astype(jnp.int32)
    owned = jnp.logical_and(block_start >= (pends - padded)[None, :], block_start < pends[None, :])
    blk_rows = jnp.sum(jnp.where(owned, jnp.clip(block_start + bm - vstart[None, :], 0, bm), 0),
                       axis=1).astype(jnp.int32)

    dest_flat = _row_dest(idx, lanes(vstart).astype(F32).reshape(1, LANES), tt)[:, :TOP_K].reshape(nk)
    xs3 = _dispatch(dest_flat, pad_off, pad_len, x1p2, n_rows, tt, bm)
    act = _moe_gate_up(block_e, blk_rows, xs3.reshape(n_rows * ROW_TILE, LANES), w_gu,
                       b_gu.reshape(N_EXPERTS, 1, 2 * D_FF), bm, tn)
    ys2 = _moe_down(block_e, blk_rows, act, w_dn, b_dn.reshape(N_EXPERTS, 1, D_MODEL), bm)
    return _combine_ln(dest_flat, gates, x1, ln_g, ln_b, ys2.reshape(n_rows, ROW_TILE, LANES), tt)


def _layer(x, w_in, shift_mu, w0, w_up, a0, a_up, g_up, k_k, k_a, r_k, gn_g, gn_b,
           lq1, lk1, lq2, lk2, subln_g, w_out, ln1_g, ln1_b,
           w_router, b_router, w_gu, b_gu, w_dn, b_dn, ln2_g, ln2_b, lambda_init,
           tm_in=2048, tseq=512, ng=4, tq=512, tm_out=512, bm=512, tn=1024, tt=256):
    batch, seq, d = x.shape
    n = batch * seq
    rw = 3 * RWKV_WIDTH
    rcols = rw + DECAY_LORA + AAA_LORA + GATE_LORA
    row = lambda a: a.reshape(1, -1)

    xf = x.reshape(n, d)
    xb = xf.astype(BF16)
    w_r = jnp.concatenate([w_in[:, :rw], _lora_layout(w_in[:, rw:rcols])], axis=1).astype(BF16)
    w_d = w_in[:, rcols:].astype(BF16)
    mu = jnp.concatenate([row(shift_mu)[:, :rw], _lora_layout(row(shift_mu)[:, rw:])], axis=1)
    p_r = _matmul(xb, w_r, F32, tm_in, 512)
    p_d = _matmul(xb, w_d, BF16, tm_in, 512)

    pad_rows = lambda a, rows: jnp.pad(a, ((0, rows - a.shape[0]), (0, 0))).astype(BF16)
    h_r = _rwkv(p_r, mu, row(w0), row(a0), row(k_k), row(k_a), row(r_k), row(gn_g), row(gn_b),
                pad_rows(w_up, LANES), pad_rows(a_up, LANES), pad_rows(g_up, 2 * LANES), batch, seq, tseq, ng)
    h_d = _diff_attention(p_d, row(lq1), row(lk1), row(lq2), row(lk2), row(subln_g), lambda_init,
                          batch, seq, tq)

    w_ob = w_out.astype(BF16)
    w_rp = _pad_cols(w_router, LANES)
    b_rp = jnp.concatenate([row(b_router), jnp.full((1, LANES - N_EXPERTS), NEG_BIG, F32)], axis=1)
    x1, x1p2, idx, gates, counts = _outproj_ln_router(
        h_r, h_d, xf, w_ob[:RWKV_WIDTH], w_ob[RWKV_WIDTH:], row(ln1_g), row(ln1_b), w_rp, b_rp, tm_out)
    out = _moe_ffn(x1, x1p2, idx, gates, counts, w_gu, b_gu, w_dn, b_dn, row(ln2_g), row(ln2_b), bm, tn, tt)
    return out.reshape(batch, seq, d)


def kernel(x, w_in, shift_mu, w0, w_up, a0, a_up, g_up, k_k, k_a, r_k, gn_g, gn_b, lq1, lk1, lq2, lk2,
           subln_g, w_out, ln1_g, ln1_b, w_router, b_router, w_gu, b_gu, w_dn, b_dn, ln2_g, ln2_b):
    for l in range(DEPTH):
        lambda_init = 0.8 - 0.6 * math.exp(-0.3 * l)
        x = _layer(x, w_in[l], shift_mu[l], w0[l], w_up[l], a0[l], a_up[l], g_up[l], k_k[l], k_a[l],
                   r_k[l], gn_g[l], gn_b[l], lq1[l], lk1[l], lq2[l], lk2[l], subln_g[l], w_out[l],
                   ln1_g[l], ln1_b[l], w_router[l], b_router[l], w_gu[l], b_gu[l], w_dn[l], b_dn[l],
                   ln2_g[l], ln2_b[l], lambda_init)
    return x
```

```python
import functools
import math

import jax
import jax.numpy as jnp
from jax import lax
from jax.experimental import pallas as pl
from jax.experimental.pallas import tpu as pltpu

F32 = jnp.float32
BF16 = jnp.bfloat16

D_MODEL = 2048
RWKV_HEAD = 64
RWKV_WIDTH = 1024
RWKV_HEADS = 16
DECAY_LORA = 64
AAA_LORA = 64
GATE_LORA = 160
DIFF_HEAD = 64
DIFF_VDIM = 128
DIFF_HEADS = 8
DIFF_WIDTH = 1024
N_EXPERTS = 32
TOP_K = 4
D_FF = 2048
SWIGLU_LIMIT = 7.0
SWIGLU_ALPHA = 1.702
LN_EPS = 1e-5
GN_EPS = RWKV_HEAD * 1e-5
RMS_EPS = 1e-5
NEG_BIG = -1e30
DEPTH = 1
DEEPNORM_ALPHA = (2.0 * DEPTH) ** 0.25

LANES = 128
SUBLANES = 8
ROW_WORDS = D_MODEL // 2
ROW_TILE = ROW_WORDS // LANES
RWKV_GROUP = 256
RWKV_CHUNK = 64
LORA_COLS = 512
VMEM_LIMIT = 56 * 1024 * 1024
HIGH_HALF = -65536
MXU_DEPTH = 256
OUTPROJ_PARTS = 2
MOE_ROW_PIECES = 8
DISPATCH_UNROLL = 2
COMBINE_SUBBLOCKS = 8
ATTN_ONES_ROWS = 16
LOG2E = 1.4426950408889634


def _cparams(sem, vmem=None):
    return pltpu.CompilerParams(dimension_semantics=sem, vmem_limit_bytes=vmem)


def _dot(a, b):
    return jnp.dot(a, b, preferred_element_type=F32)


def _dot_nt(a, b):
    return lax.dot_general(a, b, (((1,), (1,)), ((), ())), preferred_element_type=F32)


def _dot_tn(a, b):
    return lax.dot_general(a, b, (((0,), (0,)), ((), ())), preferred_element_type=F32)


def _split3(x):
    h = x.astype(BF16)
    r = x - h.astype(F32)
    m = r.astype(BF16)
    l = (r - m.astype(F32)).astype(BF16)
    return h, m, l


def _dot_split_rhs(x, ones):
    h = x.astype(BF16)
    l = (x - h.astype(F32)).astype(BF16)
    return _dot(h, ones) + _dot(l, ones)


def _dot_exact_lhs(ones, x):
    h, m, l = _split3(x)
    return _dot(ones, h) + _dot(ones, m) + _dot(ones, l)


def _sigmoid(x):
    return 1.0 / (1.0 + jnp.exp(-x))


def _matmul_kernel(x_ref, w_ref, o_ref):
    o_ref[...] = _dot(x_ref[...], w_ref[...]).astype(o_ref.dtype)


def _matmul(x, w, out_dtype, tm, tn):
    m, k = x.shape
    n = w.shape[1]
    return pl.pallas_call(
        _matmul_kernel,
        grid=(n // tn, m // tm),
        in_specs=[pl.BlockSpec((tm, k), lambda j, i: (i, 0)),
                  pl.BlockSpec((k, tn), lambda j, i: (0, j))],
        out_specs=pl.BlockSpec((tm, tn), lambda j, i: (i, j)),
        out_shape=jax.ShapeDtypeStruct((m, n), out_dtype),
        compiler_params=_cparams(("parallel", "parallel"), VMEM_LIMIT),
        name="in_proj",
    )(x, w)


def _rwkv_kernel(r_ref, k_ref, v_ref, l_ref, mur_ref, muk_ref, muv_ref, mul_ref,
                 w0_ref, a0_ref, kk_ref, ka_ref, rk_ref, gng_ref, gnb_ref,
                 wup_ref, aup_ref, gup_ref, o_ref,
                 pr_s, pk_s, pv_s, pl_s, state_s, r_s, w_s, k_s, v_s, a_s, b_s, g_s,
                 y_s, rc_p, lrb_p, lrk_p, tb_p, wc_p, akv_p, be_p, ke_p, dec_p):
    s = pl.program_id(2)
    T = r_ref.shape[0]
    G = RWKV_GROUP
    C = RWKV_CHUNK
    NG = r_ref.shape[1] // G
    groups = [slice(g * G, (g + 1) * G) for g in range(NG)]

    @pl.when(s == 0)
    def _():
        state_s[...] = jnp.zeros_like(state_s)
        pr_s[...] = jnp.zeros_like(pr_s)
        pk_s[...] = jnp.zeros_like(pk_s)
        pv_s[...] = jnp.zeros_like(pv_s)
        pl_s[...] = jnp.zeros_like(pl_s)

    row = lax.broadcasted_iota(jnp.int32, (T, 1), 0)

    def shift(ref, prev_s, mu_ref):
        p = ref[...]
        prev = jnp.where(row == 0, prev_s[...], pltpu.roll(p, 1, 0))
        prev_s[...] = p[T - 1:T, :]
        return p + (prev - p) * mu_ref[...]

    r = shift(r_ref, pr_s, mur_ref)
    k = shift(k_ref, pk_s, muk_ref)
    v = shift(v_ref, pv_s, muv_ref)
    lo = shift(l_ref, pl_s, mul_ref)
    dw = lo[:, 0:LANES]
    da = lo[:, LANES:2 * LANES]
    dg = lo[:, 2 * LANES:4 * LANES]

    wpre = w0_ref[...] + _dot(jnp.tanh(dw).astype(BF16), wup_ref[...])
    wlog = -math.exp(-0.5) * _sigmoid(wpre)
    a_sig = _sigmoid(a0_ref[...] + _dot(da.astype(BF16), aup_ref[...]))
    gate = _dot(_sigmoid(dg).astype(BF16), gup_ref[...])

    gi = lax.broadcasted_iota(jnp.int32, (G, G), 0)
    gj = lax.broadcasted_iota(jnp.int32, (G, G), 1)
    same_head = (gi // RWKV_HEAD) == (gj // RWKV_HEAD)
    head_ones = jnp.where(same_head, 1.0, 0.0).astype(BF16)

    def head_sum(x):
        return jnp.concatenate([_dot_split_rhs(x[:, g], head_ones) for g in groups], axis=1)

    kk = k * kk_ref[...]
    nrm = jnp.sqrt(head_sum(kk * kk))
    kk = kk / jnp.maximum(nrm, 1e-12)
    k2 = k * (1.0 + (a_sig - 1.0) * ka_ref[...])

    r_s[...] = r
    w_s[...] = wlog
    k_s[...] = k2
    v_s[...] = v
    a_s[...] = -kk
    b_s[...] = kk * a_sig
    g_s[...] = gate

    ci = lax.broadcasted_iota(jnp.int32, (C, C), 0)
    cj = lax.broadcasted_iota(jnp.int32, (C, C), 1)
    tri = jnp.where(cj <= ci, 1.0, 0.0).astype(BF16)
    mt = lax.broadcasted_iota(jnp.int32, (C, G), 0)
    mtp = lax.broadcasted_iota(jnp.int32, (C, G), 1) % C
    strict = mtp < mt
    incl = mtp <= mt
    ceye = jnp.where(mtp == mt, 1.0, 0.0)

    def bd(xc):
        return jnp.where(same_head, jnp.concatenate([xc, xc, xc, xc], axis=0), jnp.zeros((), BF16))

    def each(fn, *lists):
        return [fn(*xs) for xs in zip(*lists)]

    def bf(x):
        return x.astype(BF16)

    def prepare(cp, carry):
        streams = [(pl.ds(pl.multiple_of((2 * cp + h) * C, C), C), g) for h in range(2) for g in groups]
        rows = [2 * cp + h for h in range(2) for _ in groups]
        rc = [r_s[sl, g] for sl, g in streams]
        wc = [w_s[sl, g] for sl, g in streams]
        kc = [k_s[sl, g] for sl, g in streams]
        vc = [v_s[sl, g] for sl, g in streams]
        ac = [a_s[sl, g] for sl, g in streams]
        bc = [b_s[sl, g] for sl, g in streams]
        cum = each(lambda w: _dot_exact_lhs(tri, w), wc)
        tot = each(lambda x: x[C - 1:C, :], cum)
        ginv = each(lambda x: jnp.exp(-x), cum)
        gend = each(lambda x, t: jnp.exp(t - x), cum, tot)
        r_c = each(lambda x, g: bf(x * jnp.exp(g)), rc, cum)
        a_c = each(lambda x, g, w: bf(x * jnp.exp(g - w)), ac, cum, wc)
        v_c = each(bf, vc)
        k_bd = each(lambda x, g: bd(bf(x * g)), kc, ginv)
        b_bd = each(lambda x, g: bd(bf(x * g)), bc, ginv)
        ke_c = each(lambda x, g: bf(x * g), kc, gend)
        be_c = each(lambda x, g: bf(x * g), bc, gend)
        v_bd = each(bd, v_c)
        a_bd = each(bd, a_c)

        ar = each(lambda a, r: jnp.concatenate([a, r], axis=0), a_c, r_c)
        arb = each(_dot_nt, ar, b_bd)
        ark = each(_dot_nt, ar, k_bd)
        l_ab = each(lambda x: jnp.where(strict, x[:C], 0.0), arb)
        l_ak = each(lambda x: bf(jnp.where(strict, x[:C], 0.0)), ark)
        l_rb = each(lambda x: bf(jnp.where(incl, x[C:], 0.0)), arb)
        l_rk = each(lambda x: bf(jnp.where(incl, x[C:], 0.0)), ark)

        p = each(bf, l_ab)
        p_bd = each(bd, p)
        tinv = each(lambda x: ceye + x, l_ab)
        for _ in range(int(math.log2(C)) - 1):
            p = each(lambda x, y: bf(_dot(x, y)), p, p_bd)
            p_bd = each(bd, p)
            tinv = each(lambda t, y: t + _dot(bf(t), y), tinv, p_bd)
        tb = each(bf, tinv)
        akv_c = each(lambda a, b: bf(_dot(a, b)), l_ak, v_bd)
        w_c = each(lambda a, b: bf(_dot(a, b)), tb, a_bd)
        for ref, vals in zip((rc_p, lrb_p, lrk_p, tb_p, wc_p, akv_p, be_p, ke_p),
                             (r_c, l_rb, l_rk, tb, w_c, akv_c, be_c, ke_c)):
            for (sl, g), val in zip(streams, vals):
                ref[sl, g] = val
        for (_, g), row, t in zip(streams, rows, tot):
            dec_p[pl.ds(row, 1), g] = jnp.exp(t)
        return carry

    lax.fori_loop(0, T // C // 2, prepare, 0)

    def chunk(c, carry):
        sl = pl.ds(pl.multiple_of(c * C, C), C)
        load = lambda ref: [ref[sl, g] for g in groups]
        r_c, l_rb, l_rk, tb, w_c, be_c, ke_c = (load(ref) for ref in
                                                (rc_p, lrb_p, lrk_p, tb_p, wc_p, be_p, ke_p))
        akv_bd = each(bd, load(akv_p))
        v_c = each(bf, load(v_s))
        v_bd = each(bd, v_c)
        dec = [dec_p[pl.ds(c, 1), g] for g in groups]
        st = [state_s[g] for g in range(NG)]
        stb = each(bf, st)
        u_c = each(lambda w, s0, t, x: bf(_dot_nt(w, s0) + _dot(t, x)), w_c, stb, tb, akv_bd)
        u_bd = each(bd, u_c)
        y = each(lambda r, s0, lb, lk, u, vv: _dot_nt(r, s0) + _dot(jnp.concatenate([lb, lk], axis=1),
                                                                    jnp.concatenate([u, vv], axis=0)),
                 r_c, stb, l_rb, l_rk, u_bd, v_bd)
        new = each(lambda s0, t, u, vv, b, kx: s0 * t + jnp.where(
            same_head, _dot_tn(jnp.concatenate([u, vv], axis=0), jnp.concatenate([b, kx], axis=0)), 0.0),
            st, dec, u_c, v_c, be_c, ke_c)
        for g in range(NG):
            state_s[g] = new[g]
        for g, yy in zip(groups, y):
            y_s[sl, g] = yy
        return carry

    lax.fori_loop(0, T // C, chunk, 0)

    def finish(cp, carry):
        streams = [(pl.ds(pl.multiple_of((2 * cp + h) * C, C), C), g) for h in range(2) for g in groups]
        y = [y_s[sl, g] for sl, g in streams]
        mean = each(lambda x: _dot_split_rhs(x, head_ones) * (1.0 / RWKV_HEAD), y)
        d = each(lambda x, m: x - m, y, mean)
        var = each(lambda x: _dot_split_rhs(x * x, head_ones) * (1.0 / RWKV_HEAD), d)
        bonus = [_dot_split_rhs(r_s[sl, g] * k_s[sl, g] * rk_ref[:, g], head_ones) * v_s[sl, g]
                 for sl, g in streams]
        for (sl, g), dd, vr, bo in zip(streams, d, var, bonus):
            yn = dd * lax.rsqrt(vr + GN_EPS) * gng_ref[:, g] + gnb_ref[:, g]
            o_ref[sl, g] = ((yn + bo) * g_s[sl, g]).astype(o_ref.dtype)
        return carry

    lax.fori_loop(0, T // C // 2, finish, 0)


def _rwkv(p_r, mu, w0, a0, k_k, k_a, r_k, gn_g, gn_b, w_up, a_up, g_up, batch, seq, tseq, ng):
    n = batch * seq
    G = ng * RWKV_GROUP
    nq = RWKV_WIDTH // G
    ns = seq // tseq
    lora_blk = 3 * RWKV_WIDTH // LORA_COLS

    def tok(off):
        return pl.BlockSpec((tseq, G), lambda b, q, s: (b * ns + s, off + q))

    def par(off):
        return pl.BlockSpec((1, G), lambda b, q, s: (0, off + q))

    in_specs = [
        tok(0), tok(nq), tok(2 * nq),
        pl.BlockSpec((tseq, LORA_COLS), lambda b, q, s: (b * ns + s, lora_blk)),
        par(0), par(nq), par(2 * nq),
        pl.BlockSpec((1, LORA_COLS), lambda b, q, s: (0, lora_blk)),
        par(0), par(0), par(0), par(0), par(0), par(0), par(0),
        pl.BlockSpec((LANES, G), lambda b, q, s: (0, q)),
        pl.BlockSpec((LANES, G), lambda b, q, s: (0, q)),
        pl.BlockSpec((2 * LANES, G), lambda b, q, s: (0, q)),
    ]
    scratch = [pltpu.VMEM((1, G), F32), pltpu.VMEM((1, G), F32), pltpu.VMEM((1, G), F32),
               pltpu.VMEM((1, LORA_COLS), F32), pltpu.VMEM((ng, RWKV_GROUP, RWKV_GROUP), F32)]
    scratch += [pltpu.VMEM((tseq, G), F32) for _ in range(8)]
    scratch += [pltpu.VMEM((tseq, G), BF16) for _ in range(8)]
    scratch += [pltpu.VMEM((tseq // RWKV_CHUNK, G), F32)]
    return pl.pallas_call(
        _rwkv_kernel,
        grid=(batch, nq, ns),
        in_specs=in_specs,
        out_specs=pl.BlockSpec((tseq, G), lambda b, q, s: (b * ns + s, q)),
        out_shape=jax.ShapeDtypeStruct((n, RWKV_WIDTH), BF16),
        scratch_shapes=scratch,
        compiler_params=_cparams(("parallel", "parallel", "arbitrary"), VMEM_LIMIT),
        name="rwkv7",
    )(p_r, p_r, p_r, p_r, mu, mu, mu, mu, w0, a0, k_k, k_a, r_k, gn_g, gn_b, w_up, a_up, g_up)


def _attn_kernel(q_ref, k_ref, v_ref, lq1_ref, lk1_ref, lq2_ref, lk2_ref, g_ref, o_ref, vt_s, *, lambda_init):
    i = pl.program_id(2)
    tq = q_ref.shape[0]

    @pl.when(i == 0)
    def _():
        ones = jnp.ones((ATTN_ONES_ROWS, tq), BF16)
        for j in range(vt_s.shape[0]):
            vt = v_ref[j * tq:(j + 1) * tq, :].astype(F32).T.astype(BF16)
            vt_s[j] = jnp.concatenate([vt, ones], axis=0)

    drow = lax.broadcasted_iota(jnp.int32, (DIFF_VDIM, 1), 0)
    qt = (q_ref[...].astype(F32) * (DIFF_HEAD ** -0.5 * LOG2E)).T
    q1 = jnp.where(drow < DIFF_HEAD, qt, 0.0).astype(BF16)
    q2 = jnp.where(drow >= DIFF_HEAD, qt, 0.0).astype(BF16)
    lam = (jnp.exp(jnp.sum(lq1_ref[...] * lk1_ref[...], axis=-1, keepdims=True))
           - jnp.exp(jnp.sum(lq2_ref[...] * lk2_ref[...], axis=-1, keepdims=True)) + lambda_init)

    key = lax.broadcasted_iota(jnp.int32, (tq, tq), 0)
    qry = lax.broadcasted_iota(jnp.int32, (tq, tq), 1)

    def scores(j, diagonal):
        kj = k_ref[j * tq:(j + 1) * tq, :]
        sc = (_dot(kj, q1), _dot(kj, q2))
        if diagonal:
            sc = tuple(jnp.where(key <= qry, s, NEG_BIG) for s in sc)
        return sc

    def absorb(j, sc, stats):
        m, acc = stats[0:2], stats[2:4]
        vtj = vt_s[j]
        m_new = [jnp.maximum(mm, jnp.max(s, axis=0, keepdims=True)) for mm, s in zip(m, sc)]
        alpha = [jnp.exp2(mm - mn) for mm, mn in zip(m, m_new)]
        p = [jnp.exp2(s - mn).astype(BF16) for s, mn in zip(sc, m_new)]
        pv = [_dot(vtj, pp) for pp in p]
        acc = [a * ac + x for a, ac, x in zip(alpha, acc, pv)]
        return tuple(m_new) + tuple(acc)

    neg1 = jnp.full((1, tq), NEG_BIG, F32)
    zacc = jnp.zeros((DIFF_VDIM + ATTN_ONES_ROWS, tq), F32)
    init = (neg1, neg1, zacc, zacc)

    for iv in range(vt_s.shape[0]):
        @pl.when(i == iv)
        def _(iv=iv):
            stats = init
            sc = scores(0, iv == 0)
            for j in range(iv + 1):
                nxt = scores(j + 1, j + 1 == iv) if j < iv else None
                stats = absorb(j, sc, stats)
                sc = nxt
            acc = [a[:DIFF_VDIM] for a in stats[2:4]]
            l = [a[DIFF_VDIM:DIFF_VDIM + 1] for a in stats[2:4]]
            o = acc[0] / l[0] - lam * (acc[1] / l[1])
            o = o * lax.rsqrt(jnp.mean(o * o, axis=0, keepdims=True) + RMS_EPS) * g_ref[...]
            o_ref[...] = (o * (1.0 - lambda_init)).T.astype(o_ref.dtype)


def _diff_attention(p_d, lq1, lk1, lq2, lk2, subln_g, lambda_init, batch, seq, tq):
    n = batch * seq
    nq = seq // tq
    H = DIFF_HEADS
    small = pl.BlockSpec((1, DIFF_HEAD), lambda b, h, i: (0, 0))
    return pl.pallas_call(
        functools.partial(_attn_kernel, lambda_init=lambda_init),
        grid=(batch, H, nq),
        in_specs=[pl.BlockSpec((tq, DIFF_VDIM), lambda b, h, i: (b * nq + i, h)),
                  pl.BlockSpec((seq, DIFF_VDIM), lambda b, h, i: (b, H + h)),
                  pl.BlockSpec((seq, DIFF_VDIM), lambda b, h, i: (b, 2 * H + h)),
                  small, small, small, small,
                  pl.BlockSpec((DIFF_VDIM, 1), lambda b, h, i: (0, 0))],
        out_specs=pl.BlockSpec((tq, DIFF_VDIM), lambda b, h, i: (b * nq + i, h)),
        out_shape=jax.ShapeDtypeStruct((n, DIFF_WIDTH), BF16),
        scratch_shapes=[pltpu.VMEM((nq, DIFF_VDIM + ATTN_ONES_ROWS, tq), BF16)],
        compiler_params=_cparams(("parallel", "parallel", "arbitrary"), VMEM_LIMIT),
        name="diff_attn",
    )(p_d, p_d, p_d, lq1, lk1, lq2, lk2, subln_g.reshape(DIFF_VDIM, 1))


def _layer_norm(y, g, b):
    mu = jnp.mean(y, axis=-1, keepdims=True)
    d = y - mu
    var = jnp.mean(d * d, axis=-1, keepdims=True)
    return d * lax.rsqrt(var + LN_EPS) * g + b


def _store_row_tiles(ref, words, start=0, sublane=0):
    rows = words.shape[0]
    for s in range(words.shape[1] // LANES):
        ref[pl.ds(start * ROW_TILE + sublane + s, rows, stride=ROW_TILE), :] = words[:, s * LANES:(s + 1) * LANES]


def _load_row_tiles(ref, start, rows):
    parts = [ref[pl.ds(start * ROW_TILE + s, rows, stride=ROW_TILE), :] for s in range(ROW_TILE)]
    return jnp.concatenate(parts, axis=1)


def _pack_pair(lo, hi):
    bits = lambda y: lax.bitcast_convert_type(y.astype(BF16).astype(F32), jnp.int32)
    return lax.shift_right_logical(bits(lo), 16) | (bits(hi) & HIGH_HALF)


def _pack_row(y):
    return _pack_pair(y[:, :ROW_WORDS], y[:, ROW_WORDS:])


def _unpack_row(words):
    lo = lax.bitcast_convert_type(lax.shift_left(words, 16), F32)
    hi = lax.bitcast_convert_type(words & HIGH_HALF, F32)
    return lo, hi


def _expert_onehots(idx):
    lane = lax.broadcasted_iota(jnp.int32, idx.shape, 1)
    sels = [lane == idx[:, kk:kk + 1] for kk in range(TOP_K)]
    onehot = jnp.zeros(idx.shape, F32)
    for sel in sels:
        onehot = onehot + jnp.where(sel, 1.0, 0.0)
    return sels, onehot


def _outproj_kernel(hr_ref, hd_ref, x_ref, wt_ref, wb_ref, g_ref, b_ref, wrh_ref, wrl_ref, br_ref,
                    x1_ref, x1p_ref, idx_ref, gate_ref, cnt_ref, carry_s):
    i = pl.program_id(0)
    tm = x_ref.shape[0]
    th = tm // OUTPROJ_PARTS
    parts = [pl.ds(h * th, th) for h in range(OUTPROJ_PARTS)]

    @pl.when(i == 0)
    def _():
        carry_s[...] = jnp.zeros_like(carry_s)

    def each(fn, *lists):
        return [fn(*xs) for xs in zip(*lists)]

    mix = [_dot(hr_ref[p, :], wt_ref[...]) + _dot(hd_ref[p, :], wb_ref[...]) for p in parts]
    x1 = [_layer_norm(DEEPNORM_ALPHA * x_ref[p, :] + m, g_ref[...], b_ref[...]) for p, m in zip(parts, mix)]
    for h, (p, y) in enumerate(zip(parts, x1)):
        x1_ref[p, :] = y
        _store_row_tiles(x1p_ref, _pack_row(y), h * th)

    xh = each(lambda y: y.astype(BF16), x1)
    xl = each(lambda y, hh: (y - hh.astype(F32)).astype(BF16), x1, xh)
    work = each(lambda hh, ll: _dot(hh, wrh_ref[...]) + _dot(ll, wrh_ref[...]) + _dot(hh, wrl_ref[...])
                + br_ref[...], xh, xl)
    lane = lax.broadcasted_iota(jnp.int32, (th, LANES), 1).astype(F32)
    onehot = [jnp.zeros((th, LANES), F32) for _ in parts]
    vals, idxs = [], []
    for _ in range(TOP_K):
        mx = each(lambda w: jnp.max(w, axis=-1, keepdims=True), work)
        idx = each(lambda w, m: jnp.min(jnp.where(w == m, lane, float(LANES)), axis=-1, keepdims=True), work, mx)
        sel = each(lambda ix: lane == ix, idx)
        work = each(lambda s, w: jnp.where(s, -jnp.inf, w), sel, work)
        onehot = each(lambda o, s: o + jnp.where(s, 1.0, 0.0), onehot, sel)
        vals.append(mx)
        idxs.append(idx)
    total = carry_s[0:1, :]
    for h, p in enumerate(parts):
        exps = [jnp.exp(vv[h] - vals[0][h]) for vv in vals]
        den = exps[0] + exps[1] + exps[2] + exps[3]
        idx_out = jnp.zeros((th, LANES), F32)
        gate_out = jnp.zeros((th, LANES), F32)
        for kk in range(TOP_K):
            slot = lane == float(kk)
            idx_out = jnp.where(slot, idxs[kk][h], idx_out)
            gate_out = jnp.where(slot, exps[kk] / den, gate_out)
        idx_ref[p, :] = idx_out.astype(jnp.int32)
        gate_ref[p, :] = gate_out
        total = total + jnp.sum(onehot[h], axis=0, keepdims=True)
    carry_s[...] = jnp.broadcast_to(total, carry_s.shape)
    cnt_ref[...] = jnp.broadcast_to(total, cnt_ref.shape)


def _outproj_ln_router(hr, hd, x, w_top, w_bot, g, b, w_r, b_r, tm):
    n = x.shape[0]
    w_rh = w_r.astype(BF16)
    w_rl = (w_r - w_rh.astype(F32)).astype(BF16)
    const = lambda shape: pl.BlockSpec(shape, lambda i: (0, 0))
    rowb = lambda cols: pl.BlockSpec((tm, cols), lambda i: (i, 0))
    out_shape = (jax.ShapeDtypeStruct((n, D_MODEL), F32),
                 jax.ShapeDtypeStruct((n * ROW_TILE, LANES), jnp.int32),
                 jax.ShapeDtypeStruct((n, LANES), jnp.int32),
                 jax.ShapeDtypeStruct((n, LANES), F32),
                 jax.ShapeDtypeStruct((SUBLANES, LANES), F32))
    return pl.pallas_call(
        _outproj_kernel,
        grid=(n // tm,),
        in_specs=[rowb(RWKV_WIDTH), rowb(DIFF_WIDTH), rowb(D_MODEL),
                  const((RWKV_WIDTH, D_MODEL)), const((DIFF_WIDTH, D_MODEL)),
                  const((1, D_MODEL)), const((1, D_MODEL)),
                  const((D_MODEL, LANES)), const((D_MODEL, LANES)), const((1, LANES))],
        out_specs=(rowb(D_MODEL), pl.BlockSpec((tm * ROW_TILE, LANES), lambda i: (i, 0)),
                   rowb(LANES), rowb(LANES), const((SUBLANES, LANES))),
        out_shape=out_shape,
        scratch_shapes=[pltpu.VMEM((SUBLANES, LANES), F32)],
        compiler_params=_cparams(("arbitrary",), VMEM_LIMIT),
        name="outproj_ln_router",
    )(hr, hd, x, w_top, w_bot, g, b, w_rh, w_rl, b_r)


def _dest_kernel(idx_ref, ps_ref, dest_ref, carry_s):
    i = pl.program_id(0)
    tm = idx_ref.shape[0]

    @pl.when(i == 0)
    def _():
        carry_s[...] = jnp.broadcast_to(ps_ref[...], carry_s.shape)

    sels, onehot = _expert_onehots(idx_ref[...])
    ti = lax.broadcasted_iota(jnp.int32, (tm, tm), 0)
    tj = lax.broadcasted_iota(jnp.int32, (tm, tm), 1)
    before = jnp.where(tj < ti, 1.0, 0.0).astype(BF16)
    cum = _dot(before, onehot.astype(BF16)) + carry_s[0:1, :]
    lane = lax.broadcasted_iota(jnp.int32, (tm, LANES), 1)
    dest = jnp.zeros((tm, LANES), F32)
    for kk in range(TOP_K):
        dk = jnp.sum(jnp.where(sels[kk], cum, 0.0), axis=-1, keepdims=True)
        dest = jnp.where(lane == kk, dk, dest)
    dest_ref[...] = dest.astype(jnp.int32)
    total = carry_s[0:1, :] + jnp.sum(onehot, axis=0, keepdims=True)
    carry_s[...] = jnp.broadcast_to(total, carry_s.shape)


def _row_dest(idx, pstarts, tm):
    n = idx.shape[0]
    return pl.pallas_call(
        _dest_kernel,
        grid=(n // tm,),
        in_specs=[pl.BlockSpec((tm, LANES), lambda i: (i, 0)), pl.BlockSpec((1, LANES), lambda i: (0, 0))],
        out_specs=pl.BlockSpec((tm, LANES), lambda i: (i, 0)),
        out_shape=jax.ShapeDtypeStruct((n, LANES), jnp.int32),
        scratch_shapes=[pltpu.VMEM((SUBLANES, LANES), F32)],
        compiler_params=_cparams(("arbitrary",)),
        name="moe_row_dest",
    )(idx, pstarts)


def _wait_rows(ref, rows, sem):
    view = ref.at[pl.ds(0, rows)]
    pltpu.make_async_copy(view, view, sem).wait()


def _dispatch_kernel(dest_ref, padoff_ref, padlen_ref, x_ref, xs_hbm, zero_s, sem, zsem):
    tt = x_ref.shape[0] // ROW_TILE

    def pad_copies(e, go):
        off = padoff_ref[e]
        ln = padlen_ref[e]
        for bit in range(zero_s.shape[0].bit_length()):
            size = 1 << bit

            @pl.when((ln >> bit) & 1 == 1)
            def _():
                go(pltpu.make_async_copy(zero_s.at[pl.ds(0, size)],
                                         xs_hbm.at[pl.ds(off + (ln & (size - 1)), size)], zsem))

    def tail_copy(piece):
        rows = zero_s.shape[0]
        first = pl.multiple_of(padoff_ref[N_EXPERTS] + piece * rows, rows)
        return pltpu.make_async_copy(zero_s, xs_hbm.at[pl.ds(first, rows)], zsem)

    @pl.when(pl.program_id(0) == 0)
    def _():
        zero_s[...] = jnp.zeros_like(zero_s)
        for go in (lambda cp: cp.start(), lambda cp: cp.wait()):
            def body(e, c, go=go):
                pad_copies(e, go)
                return c
            lax.fori_loop(0, N_EXPERTS, body, 0)

            def tail(piece, c, go=go):
                go(tail_copy(piece))
                return c
            lax.fori_loop(0, padlen_ref[N_EXPERTS], tail, 0)

    def issue(tp, c):
        for tl in range(DISPATCH_UNROLL):
            t = tp * DISPATCH_UNROLL + tl
            src = x_ref.at[pl.ds(pl.multiple_of(t * ROW_TILE, ROW_TILE), ROW_TILE), :]
            for kk in range(TOP_K):
                pltpu.make_async_copy(src, xs_hbm.at[dest_ref[t * TOP_K + kk]], sem).start(priority=kk % 2)
        return c

    lax.fori_loop(0, tt // DISPATCH_UNROLL, issue, 0)
    for _ in range(TOP_K):
        _wait_rows(xs_hbm, tt, sem)


def _dispatch(dest_flat, pad_off, pad_len, x1p2, n_rows, tt, bm):
    n = x1p2.shape[0] // ROW_TILE
    table = pl.BlockSpec((LANES,), lambda i: (0,), memory_space=pltpu.SMEM)
    return pl.pallas_call(
        _dispatch_kernel,
        grid=(n // tt,),
        in_specs=[pl.BlockSpec((tt * TOP_K,), lambda i: (i,), memory_space=pltpu.SMEM), table, table,
                  pl.BlockSpec((tt * ROW_TILE, LANES), lambda i: (i, 0))],
        out_specs=pl.BlockSpec(memory_space=pl.ANY),
        out_shape=jax.ShapeDtypeStruct((n_rows, ROW_TILE, LANES), jnp.int32),
        scratch_shapes=[pltpu.VMEM((bm // 2, ROW_TILE, LANES), jnp.int32),
                        pltpu.SemaphoreType.DMA, pltpu.SemaphoreType.DMA],
        compiler_params=_cparams(("arbitrary",)),
        name="moe_dispatch",
    )(dest_flat, pad_off, pad_len, x1p2)


def _dot_f32_weights(xk, w_ref, cols):
    acc = None
    for kc, x in enumerate(xk):
        part = _dot(x, w_ref[0, kc * MXU_DEPTH:(kc + 1) * MXU_DEPTH, cols].astype(BF16))
        acc = part if acc is None else acc + part
    return acc


def _for_valid_rows(nr, bm, body, o_ref):
    piece = bm // MOE_ROW_PIECES
    out_rows_per_row = o_ref.shape[0] // bm
    for q in range(1, MOE_ROW_PIECES + 1):
        @pl.when(jnp.logical_and(nr > (q - 1) * piece, nr <= q * piece))
        def _(first=bm - q * piece):
            body(first)
            if first > 0:
                o_ref[0:first * out_rows_per_row, :] = jnp.zeros(
                    (first * out_rows_per_row, o_ref.shape[1]), o_ref.dtype)

    @pl.when(nr == 0)
    def _():
        o_ref[...] = jnp.zeros_like(o_ref)


def _gu_kernel(be_ref, rows_ref, xs_ref, wg_ref, wu_ref, bg_ref, bu_ref, o_ref):
    del be_ref

    bm = o_ref.shape[0]

    def body(first):
        lo, hi = _unpack_row(_load_row_tiles(xs_ref, first, bm - first))
        x = jnp.concatenate([lo, hi], axis=1).astype(BF16)
        xk = [x[:, kc * MXU_DEPTH:(kc + 1) * MXU_DEPTH] for kc in range(D_MODEL // MXU_DEPTH)]
        g = _dot_f32_weights(xk, wg_ref, slice(None)) + bg_ref[0]
        u = _dot_f32_weights(xk, wu_ref, slice(None)) + bu_ref[0]
        g = jnp.minimum(g, SWIGLU_LIMIT)
        u = jnp.clip(u, -SWIGLU_LIMIT, SWIGLU_LIMIT)
        o_ref[first:, :] = ((u + 1.0) * (g * _sigmoid(SWIGLU_ALPHA * g))).astype(o_ref.dtype)

    _for_valid_rows(rows_ref[pl.program_id(1)], bm, body, o_ref)


def _moe_gate_up(block_e, blk_rows, xs2, w_gu, b_gu, bm, tn):
    n_rows = xs2.shape[0] // ROW_TILE
    n_blocks = n_rows // bm
    nt = D_FF // tn
    grid_spec = pltpu.PrefetchScalarGridSpec(
        num_scalar_prefetch=2,
        grid=(nt, n_blocks),
        in_specs=[pl.BlockSpec((bm * ROW_TILE, LANES), lambda n, j, be, nv: (j, 0)),
                  pl.BlockSpec((1, D_MODEL, tn), lambda n, j, be, nv: (be[j], 0, n)),
                  pl.BlockSpec((1, D_MODEL, tn), lambda n, j, be, nv: (be[j], 0, nt + n)),
                  pl.BlockSpec((1, 1, tn), lambda n, j, be, nv: (be[j], 0, n)),
                  pl.BlockSpec((1, 1, tn), lambda n, j, be, nv: (be[j], 0, nt + n))],
        out_specs=pl.BlockSpec((bm, tn), lambda n, j, be, nv: (j, n)))
    return pl.pallas_call(
        _gu_kernel,
        grid_spec=grid_spec,
        out_shape=jax.ShapeDtypeStruct((n_rows, D_FF), BF16),
        compiler_params=_cparams(("arbitrary", "arbitrary"), VMEM_LIMIT),
        name="moe_gate_up",
    )(block_e, blk_rows, xs2, w_gu, w_gu, b_gu, b_gu)


def _dn_kernel(be_ref, rows_ref, a_ref, w_ref, b_ref, o_ref):
    del be_ref

    def body(first):
        ak = [a_ref[first:, kc * MXU_DEPTH:(kc + 1) * MXU_DEPTH] for kc in range(D_FF // MXU_DEPTH)]
        hw = ROW_WORDS // 2
        for h in range(2):
            lo = slice(h * hw, (h + 1) * hw)
            hi = slice(ROW_WORDS + h * hw, ROW_WORDS + (h + 1) * hw)
            y_lo = _dot_f32_weights(ak, w_ref, lo) + b_ref[0, :, lo]
            y_hi = _dot_f32_weights(ak, w_ref, hi) + b_ref[0, :, hi]
            _store_row_tiles(o_ref, _pack_pair(y_lo, y_hi), first, h * (hw // LANES))

    _for_valid_rows(rows_ref[pl.program_id(0)], a_ref.shape[0], body, o_ref)


def _moe_down(block_e, blk_rows, act, w_dn, b_dn, bm):
    n_rows = act.shape[0]
    n_blocks = n_rows // bm
    grid_spec = pltpu.PrefetchScalarGridSpec(
        num_scalar_prefetch=2,
        grid=(n_blocks,),
        in_specs=[pl.BlockSpec((bm, D_FF), lambda j, be, nv: (j, 0)),
                  pl.BlockSpec((1, D_FF, D_MODEL), lambda j, be, nv: (be[j], 0, 0)),
                  pl.BlockSpec((1, 1, D_MODEL), lambda j, be, nv: (be[j], 0, 0))],
        out_specs=pl.BlockSpec((bm * ROW_TILE, LANES), lambda j, be, nv: (j, 0)))
    return pl.pallas_call(
        _dn_kernel,
        grid_spec=grid_spec,
        out_shape=jax.ShapeDtypeStruct((n_rows * ROW_TILE, LANES), jnp.int32),
        compiler_params=_cparams(("arbitrary",), VMEM_LIMIT),
        name="moe_down",
    )(block_e, blk_rows, act, w_dn, b_dn)


def _combine_kernel(dest_ref, next_ref, gate_ref, x1_ref, g_ref, b_ref, ys_hbm, o_ref, buf_a, buf_b, sem):
    i = pl.program_id(0)
    tt = x1_ref.shape[0]
    sub = tt // COMBINE_SUBBLOCKS
    bufs = (buf_a, buf_b)

    def start_rows(d_ref, s, t):
        for kk in range(TOP_K):
            row = pl.multiple_of((kk * tt + t) * ROW_TILE, ROW_TILE)
            pltpu.make_async_copy(ys_hbm.at[d_ref[t * TOP_K + kk]],
                                  bufs[s].at[pl.ds(row, ROW_TILE), :], sem.at[s]).start(priority=kk % 2)

    def wait_tile(s):
        for kk in range(TOP_K):
            view = bufs[s].at[pl.ds(kk * tt * ROW_TILE, tt * ROW_TILE), :]
            pltpu.make_async_copy(view, view, sem.at[s]).wait()

    @pl.when(i == 0)
    def _():
        def body(t, c):
            start_rows(dest_ref, 0, t)
            return c
        lax.fori_loop(0, tt, body, 0)

    def step(cur, nxt):
        wait_tile(cur)

        def reduce_rows(sb, c):
            r0 = pl.multiple_of(sb * sub, sub)
            for tl in range(sub):
                start_rows(next_ref, nxt, r0 + tl)
            rows = pl.ds(r0, sub)
            gates = gate_ref[rows, :]
            acc_lo = jnp.zeros((sub, ROW_WORDS), F32)
            acc_hi = jnp.zeros((sub, ROW_WORDS), F32)
            for kk in range(TOP_K):
                lo, hi = _unpack_row(_load_row_tiles(bufs[cur], kk * tt + r0, sub))
                gk = gates[:, kk:kk + 1]
                acc_lo = acc_lo + gk * lo
                acc_hi = acc_hi + gk * hi
            ffn = jnp.concatenate([acc_lo, acc_hi], axis=1)
            o_ref[rows, :] = _layer_norm(DEEPNORM_ALPHA * x1_ref[rows, :] + ffn, g_ref[...], b_ref[...])
            return c

        lax.fori_loop(0, COMBINE_SUBBLOCKS, reduce_rows, 0)

        @pl.when(i + 1 == pl.num_programs(0))
        def _():
            wait_tile(nxt)

    for cur in range(2):
        @pl.when(i % 2 == cur)
        def _(cur=cur):
            step(cur, 1 - cur)


def _combine_ln(dest_flat, gates, x1, g, b, ys3, tt):
    n = x1.shape[0]
    last = n // tt - 1
    const = lambda shape: pl.BlockSpec(shape, lambda i: (0, 0))
    return pl.pallas_call(
        _combine_kernel,
        grid=(n // tt,),
        in_specs=[pl.BlockSpec((tt * TOP_K,), lambda i: (i,), memory_space=pltpu.SMEM),
                  pl.BlockSpec((tt * TOP_K,), lambda i: (jnp.minimum(i + 1, last),), memory_space=pltpu.SMEM),
                  pl.BlockSpec((tt, LANES), lambda i: (i, 0)),
                  pl.BlockSpec((tt, D_MODEL), lambda i: (i, 0)),
                  const((1, D_MODEL)), const((1, D_MODEL)),
                  pl.BlockSpec(memory_space=pl.ANY)],
        out_specs=pl.BlockSpec((tt, D_MODEL), lambda i: (i, 0)),
        out_shape=jax.ShapeDtypeStruct((n, D_MODEL), F32),
        scratch_shapes=[pltpu.VMEM((TOP_K * tt * ROW_TILE, LANES), jnp.int32),
                        pltpu.VMEM((TOP_K * tt * ROW_TILE, LANES), jnp.int32),
                        pltpu.SemaphoreType.DMA((2,))],
        compiler_params=_cparams(("arbitrary",), VMEM_LIMIT),
        name="moe_combine_ln",
    )(dest_flat, dest_flat, gates, x1, g, b, ys3)


def _pad_cols(a, width):
    return jnp.pad(a, ((0, 0), (0, width - a.shape[1])))


def _lora_layout(a):
    dw = a[:, :DECAY_LORA]
    da = a[:, DECAY_LORA:DECAY_LORA + AAA_LORA]
    dg = a[:, DECAY_LORA + AAA_LORA:]
    return jnp.concatenate([_pad_cols(dw, LANES), _pad_cols(da, LANES), _pad_cols(dg, 2 * LANES)], axis=1)


def _moe_ffn(x1, x1p2, idx, gates, counts, w_gu, b_gu, w_dn, b_dn, ln_g, ln_b, bm, tn, tt):
    n = x1.shape[0]
    nk = n * TOP_K
    n_blocks = nk // bm + N_EXPERTS
    n_rows = n_blocks * bm
    cnt = counts[0, :N_EXPERTS].astype(jnp.int32)
    padded = ((cnt + bm - 1) // bm) * bm
    pends = jnp.cumsum(padded)
    lanes = lambda a: jnp.pad(a, (0, LANES - a.shape[0]))
    vstart = pends - cnt
    pad_off = lanes(jnp.concatenate([pends - padded, pends[-1:]])).astype(jnp.int32)
    pad_len = lanes(jnp.concatenate([padded - cnt, (n_rows - pends[-1:]) // (bm // 2)])).astype(jnp.int32)
    block_start = jnp.arange(n_blocks, dtype=jnp.int32)[:, None] * bm
    block_e = jnp.minimum(jnp.sum(pends[None, :] <= block_start, axis=1), N_EXPERTS - 1).astype(jnp.int32)
    owned = jnp.logical_and(block_start >= (pends - padded)[None, :], block_start < pends[None, :])
    blk_rows = jnp.sum(jnp.where(owned, jnp.clip(block_start + bm - vstart[None, :], 0, bm), 0),
                       axis=1).astype(jnp.int32)

    dest_flat = _row_dest(idx, lanes(vstart).astype(F32).reshape(1, LANES), tt)[:, :TOP_K].reshape(nk)
    xs3 = _dispatch(dest_flat, pad_off, pad_len, x1p2, n_rows, tt, bm)
    act = _moe_gate_up(block_e, blk_rows, xs3.reshape(n_rows * ROW_TILE, LANES), w_gu,
                       b_gu.reshape(N_EXPERTS, 1, 2 * D_FF), bm, tn)
    ys2 = _moe_down(block_e, blk_rows, act, w_dn, b_dn.reshape(N_EXPERTS, 1, D_MODEL), bm)
    return _combine_ln(dest_flat, gates, x1, ln_g, ln_b, ys2.reshape(n_rows, ROW_TILE, LANES), tt)


def _layer(x, w_in, shift_mu, w0, w_up, a0, a_up, g_up, k_k, k_a, r_k, gn_g, gn_b,
           lq1, lk1, lq2, lk2, subln_g, w_out, ln1_g, ln1_b,
           w_router, b_router, w_gu, b_gu, w_dn, b_dn, ln2_g, ln2_b, lambda_init,
           tm_in=2048, tseq=512, ng=4, tq=512, tm_out=512, bm=512, tn=1024, tt=256):
    batch, seq, d = x.shape
    n = batch * seq
    rw = 3 * RWKV_WIDTH
    rcols = rw + DECAY_LORA + AAA_LORA + GATE_LORA
    row = lambda a: a.reshape(1, -1)

    xf = x.reshape(n, d)
    xb = xf.astype(BF16)
    w_r = jnp.concatenate([w_in[:, :rw], _lora_layout(w_in[:, rw:rcols])], axis=1).astype(BF16)
    w_d = w_in[:, rcols:].astype(BF16)
    mu = jnp.concatenate([row(shift_mu)[:, :rw], _lora_layout(row(shift_mu)[:, rw:])], axis=1)
    p_r = _matmul(xb, w_r, F32, tm_in, 512)
    p_d = _matmul(xb, w_d, BF16, tm_in, 512)

    pad_rows = lambda a, rows: jnp.pad(a, ((0, rows - a.shape[0]), (0, 0))).astype(BF16)
    h_r = _rwkv(p_r, mu, row(w0), row(a0), row(k_k), row(k_a), row(r_k), row(gn_g), row(gn_b),
                pad_rows(w_up, LANES), pad_rows(a_up, LANES), pad_rows(g_up, 2 * LANES), batch, seq, tseq, ng)
    h_d = _diff_attention(p_d, row(lq1), row(lk1), row(lq2), row(lk2), row(subln_g), lambda_init,
                          batch, seq, tq)

    w_ob = w_out.astype(BF16)
    w_rp = _pad_cols(w_router, LANES)
    b_rp = jnp.concatenate([row(b_router), jnp.full((1, LANES - N_EXPERTS), NEG_BIG, F32)], axis=1)
    x1, x1p2, idx, gates, counts = _outproj_ln_router(
        h_r, h_d, xf, w_ob[:RWKV_WIDTH], w_ob[RWKV_WIDTH:], row(ln1_g), row(ln1_b), w_rp, b_rp, tm_out)
    out = _moe_ffn(x1, x1p2, idx, gates, counts, w_gu, b_gu, w_dn, b_dn, row(ln2_g), row(ln2_b), bm, tn, tt)
    return out.reshape(batch, seq, d)


def kernel(x, w_in, shift_mu, w0, w_up, a0, a_up, g_up, k_k, k_a, r_k, gn_g, gn_b, lq1, lk1, lq2, lk2,
           subln_g, w_out, ln1_g, ln1_b, w_router, b_router, w_gu, b_gu, w_dn, b_dn, ln2_g, ln2_b):
    for l in range(DEPTH):
        lambda_init = 0.8 - 0.6 * math.exp(-0.3 * l)
        x = _layer(x, w_in[l], shift_mu[l], w0[l], w_up[l], a0[l], a_up[l], g_up[l], k_k[l], k_a[l],
                   r_k[l], gn_g[l], gn_b[l], lq1[l], lk1[l], lq2[l], lk2[l], subln_g[l], w_out[l],
                   ln1_g[l], ln1_b[l], w_router[l], b_router[l], w_gu[l], b_gu[l], w_dn[l], b_dn[l],
                   ln2_g[l], ln2_b[l], lambda_init)
    return x
```

```python
import functools
import math

import jax
import jax.numpy as jnp
from jax import lax
from jax.experimental import pallas as pl
from jax.experimental.pallas import tpu as pltpu

F32 = jnp.float32
BF16 = jnp.bfloat16

D_MODEL = 2048
RWKV_HEAD = 64
RWKV_WIDTH = 1024
RWKV_HEADS = 16
DECAY_LORA = 64
AAA_LORA = 64
GATE_LORA = 160
DIFF_HEAD = 64
DIFF_VDIM = 128
DIFF_HEADS = 8
DIFF_WIDTH = 1024
N_EXPERTS = 32
TOP_K = 4
D_FF = 2048
SWIGLU_LIMIT = 7.0
SWIGLU_ALPHA = 1.702
LN_EPS = 1e-5
GN_EPS = RWKV_HEAD * 1e-5
RMS_EPS = 1e-5
NEG_BIG = -1e30
DEPTH = 1
DEEPNORM_ALPHA = (2.0 * DEPTH) ** 0.25

LANES = 128
SUBLANES = 8
ROW_WORDS = D_MODEL // 2
ROW_TILE = ROW_WORDS // LANES
RWKV_GROUP = 256
RWKV_CHUNK = 64
LORA_COLS = 512
VMEM_LIMIT = 56 * 1024 * 1024
HIGH_HALF = -65536
MXU_DEPTH = 256
OUTPROJ_PARTS = 2
MOE_ROW_PIECES = 8
DISPATCH_UNROLL = 2
COMBINE_SUBBLOCKS = 8
ATTN_ONES_ROWS = 16
LOG2E = 1.4426950408889634


def _cparams(sem, vmem=None):
    return pltpu.CompilerParams(dimension_semantics=sem, vmem_limit_bytes=vmem)


def _dot(a, b):
    return jnp.dot(a, b, preferred_element_type=F32)


def _dot_nt(a, b):
    return lax.dot_general(a, b, (((1,), (1,)), ((), ())), preferred_element_type=F32)


def _dot_tn(a, b):
    return lax.dot_general(a, b, (((0,), (0,)), ((), ())), preferred_element_type=F32)


def _split3(x):
    h = x.astype(BF16)
    r = x - h.astype(F32)
    m = r.astype(BF16)
    l = (r - m.astype(F32)).astype(BF16)
    return h, m, l


def _dot_split_rhs(x, ones):
    h = x.astype(BF16)
    l = (x - h.astype(F32)).astype(BF16)
    return _dot(h, ones) + _dot(l, ones)


def _dot_exact_lhs(ones, x):
    h, m, l = _split3(x)
    return _dot(ones, h) + _dot(ones, m) + _dot(ones, l)


def _sigmoid(x):
    return 1.0 / (1.0 + jnp.exp(-x))


def _matmul_kernel(x_ref, w_ref, o_ref):
    o_ref[...] = _dot(x_ref[...], w_ref[...]).astype(o_ref.dtype)


def _matmul(x, w, out_dtype, tm, tn):
    m, k = x.shape
    n = w.shape[1]
    return pl.pallas_call(
        _matmul_kernel,
        grid=(n // tn, m // tm),
        in_specs=[pl.BlockSpec((tm, k), lambda j, i: (i, 0)),
                  pl.BlockSpec((k, tn), lambda j, i: (0, j))],
        out_specs=pl.BlockSpec((tm, tn), lambda j, i: (i, j)),
        out_shape=jax.ShapeDtypeStruct((m, n), out_dtype),
        compiler_params=_cparams(("parallel", "parallel"), VMEM_LIMIT),
        name="in_proj",
    )(x, w)


def _rwkv_kernel(r_ref, k_ref, v_ref, l_ref, mur_ref, muk_ref, muv_ref, mul_ref,
                 w0_ref, a0_ref, kk_ref, ka_ref, rk_ref, gng_ref, gnb_ref,
                 wup_ref, aup_ref, gup_ref, o_ref,
                 pr_s, pk_s, pv_s, pl_s, state_s, r_s, w_s, k_s, v_s, a_s, b_s, g_s,
                 y_s, rc_p, lrb_p, lrk_p, tb_p, wc_p, akv_p, be_p, ke_p, dec_p):
    s = pl.program_id(2)
    T = r_ref.shape[0]
    G = RWKV_GROUP
    C = RWKV_CHUNK
    NG = r_ref.shape[1] // G
    groups = [slice(g * G, (g + 1) * G) for g in range(NG)]

    @pl.when(s == 0)
    def _():
        state_s[...] = jnp.zeros_like(state_s)
        pr_s[...] = jnp.zeros_like(pr_s)
        pk_s[...] = jnp.zeros_like(pk_s)
        pv_s[...] = jnp.zeros_like(pv_s)
        pl_s[...] = jnp.zeros_like(pl_s)

    row = lax.broadcasted_iota(jnp.int32, (T, 1), 0)

    def shift(ref, prev_s, mu_ref):
        p = ref[...]
        prev = jnp.where(row == 0, prev_s[...], pltpu.roll(p, 1, 0))
        prev_s[...] = p[T - 1:T, :]
        return p + (prev - p) * mu_ref[...]

    r = shift(r_ref, pr_s, mur_ref)
    k = shift(k_ref, pk_s, muk_ref)
    v = shift(v_ref, pv_s, muv_ref)
    lo = shift(l_ref, pl_s, mul_ref)
    dw = lo[:, 0:LANES]
    da = lo[:, LANES:2 * LANES]
    dg = lo[:, 2 * LANES:4 * LANES]

    wpre = w0_ref[...] + _dot(jnp.tanh(dw).astype(BF16), wup_ref[...])
    wlog = -math.exp(-0.5) * _sigmoid(wpre)
    a_sig = _sigmoid(a0_ref[...] + _dot(da.astype(BF16), aup_ref[...]))
    gate = _dot(_sigmoid(dg).astype(BF16), gup_ref[...])

    gi = lax.broadcasted_iota(jnp.int32, (G, G), 0)
    gj = lax.broadcasted_iota(jnp.int32, (G, G), 1)
    same_head = (gi // RWKV_HEAD) == (gj // RWKV_HEAD)
    head_ones = jnp.where(same_head, 1.0, 0.0).astype(BF16)

    def head_sum(x):
        return jnp.concatenate([_dot_split_rhs(x[:, g], head_ones) for g in groups], axis=1)

    kk = k * kk_ref[...]
    nrm = jnp.sqrt(head_sum(kk * kk))
    kk = kk / jnp.maximum(nrm, 1e-12)
    k2 = k * (1.0 + (a_sig - 1.0) * ka_ref[...])

    r_s[...] = r
    w_s[...] = wlog
    k_s[...] = k2
    v_s[...] = v
    a_s[...] = -kk
    b_s[...] = kk * a_sig
    g_s[...] = gate

    ci = lax.broadcasted_iota(jnp.int32, (C, C), 0)
    cj = lax.broadcasted_iota(jnp.int32, (C, C), 1)
    tri = jnp.where(cj <= ci, 1.0, 0.0).astype(BF16)
    mt = lax.broadcasted_iota(jnp.int32, (C, G), 0)
    mtp = lax.broadcasted_iota(jnp.int32, (C, G), 1) % C
    strict = mtp < mt
    incl = mtp <= mt
    ceye = jnp.where(mtp == mt, 1.0, 0.0)

    def bd(xc):
        return jnp.where(same_head, jnp.concatenate([xc, xc, xc, xc], axis=0), jnp.zeros((), BF16))

    def each(fn, *lists):
        return [fn(*xs) for xs in zip(*lists)]

    def bf(x):
        return x.astype(BF16)

    def prepare(cp, carry):
        streams = [(pl.ds(pl.multiple_of((2 * cp + h) * C, C), C), g) for h in range(2) for g in groups]
        rows = [2 * cp + h for h in range(2) for _ in groups]
        rc = [r_s[sl, g] for sl, g in streams]
        wc = [w_s[sl, g] for sl, g in streams]
        kc = [k_s[sl, g] for sl, g in streams]
        vc = [v_s[sl, g] for sl, g in streams]
        ac = [a_s[sl, g] for sl, g in streams]
        bc = [b_s[sl, g] for sl, g in streams]
        cum = each(lambda w: _dot_exact_lhs(tri, w), wc)
        tot = each(lambda x: x[C - 1:C, :], cum)
        ginv = each(lambda x: jnp.exp(-x), cum)
        gend = each(lambda x, t: jnp.exp(t - x), cum, tot)
        r_c = each(lambda x, g: bf(x * jnp.exp(g)), rc, cum)
        a_c = each(lambda x, g, w: bf(x * jnp.exp(g - w)), ac, cum, wc)
        v_c = each(bf, vc)
        k_bd = each(lambda x, g: bd(bf(x * g)), kc, ginv)
        b_bd = each(lambda x, g: bd(bf(x * g)), bc, ginv)
        ke_c = each(lambda x, g: bf(x * g), kc, gend)
        be_c = each(lambda x, g: bf(x * g), bc, gend)
        v_bd = each(bd, v_c)
        a_bd = each(bd, a_c)

        ar = each(lambda a, r: jnp.concatenate([a, r], axis=0), a_c, r_c)
        arb = each(_dot_nt, ar, b_bd)
        ark = each(_dot_nt, ar, k_bd)
        l_ab = each(lambda x: jnp.where(strict, x[:C], 0.0), arb)
        l_ak = each(lambda x: bf(jnp.where(strict, x[:C], 0.0)), ark)
        l_rb = each(lambda x: bf(jnp.where(incl, x[C:], 0.0)), arb)
        l_rk = each(lambda x: bf(jnp.where(incl, x[C:], 0.0)), ark)

        p = each(bf, l_ab)
        p_bd = each(bd, p)
        tinv = each(lambda x: ceye + x, l_ab)
        for _ in range(int(math.log2(C)) - 1):
            p = each(lambda x, y: bf(_dot(x, y)), p, p_bd)
            p_bd = each(bd, p)
            tinv = each(lambda t, y: t + _dot(bf(t), y), tinv, p_bd)
        tb = each(bf, tinv)
        akv_c = each(lambda a, b: bf(_dot(a, b)), l_ak, v_bd)
        w_c = each(lambda a, b: bf(_dot(a, b)), tb, a_bd)
        for ref, vals in zip((rc_p, lrb_p, lrk_p, tb_p, wc_p, akv_p, be_p, ke_p),
                             (r_c, l_rb, l_rk, tb, w_c, akv_c, be_c, ke_c)):
            for (sl, g), val in zip(streams, vals):
                ref[sl, g] = val
        for (_, g), row, t in zip(streams, rows, tot):
            dec_p[pl.ds(row, 1), g] = jnp.exp(t)
        return carry

    lax.fori_loop(0, T // C // 2, prepare, 0)

    def chunk(c, carry):
        sl = pl.ds(pl.multiple_of(c * C, C), C)
        load = lambda ref: [ref[sl, g] for g in groups]
        r_c, l_rb, l_rk, tb, w_c, be_c, ke_c = (load(ref) for ref in
                                                (rc_p, lrb_p, lrk_p, tb_p, wc_p, be_p, ke_p))
        akv_bd = each(bd, load(akv_p))
        v_c = each(bf, load(v_s))
        v_bd = each(bd, v_c)
        dec = [dec_p[pl.ds(c, 1), g] for g in groups]
        st = [state_s[g] for g in range(NG)]
        stb = each(bf, st)
        u_c = each(lambda w, s0, t, x: bf(_dot_nt(w, s0) + _dot(t, x)), w_c, stb, tb, akv_bd)
        u_bd = each(bd, u_c)
        y = each(lambda r, s0, lb, lk, u, vv: _dot_nt(r, s0) + _dot(jnp.concatenate([lb, lk], axis=1),
                                                                    jnp.concatenate([u, vv], axis=0)),
                 r_c, stb, l_rb, l_rk, u_bd, v_bd)
        new = each(lambda s0, t, u, vv, b, kx: s0 * t + jnp.where(
            same_head, _dot_tn(jnp.concatenate([u, vv], axis=0), jnp.concatenate([b, kx], axis=0)), 0.0),
            st, dec, u_c, v_c, be_c, ke_c)
        for g in range(NG):
            state_s[g] = new[g]
        for g, yy in zip(groups, y):
            y_s[sl, g] = yy
        return carry

    lax.fori_loop(0, T // C, chunk, 0)

    def finish(cp, carry):
        streams = [(pl.ds(pl.multiple_of((2 * cp + h) * C, C), C), g) for h in range(2) for g in groups]
        y = [y_s[sl, g] for sl, g in streams]
        mean = each(lambda x: _dot_split_rhs(x, head_ones) * (1.0 / RWKV_HEAD), y)
        d = each(lambda x, m: x - m, y, mean)
        var = each(lambda x: _dot_split_rhs(x * x, head_ones) * (1.0 / RWKV_HEAD), d)
        bonus = [_dot_split_rhs(r_s[sl, g] * k_s[sl, g] * rk_ref[:, g], head_ones) * v_s[sl, g]
                 for sl, g in streams]
        for (sl, g), dd, vr, bo in zip(streams, d, var, bonus):
            yn = dd * lax.rsqrt(vr + GN_EPS) * gng_ref[:, g] + gnb_ref[:, g]
            o_ref[sl, g] = ((yn + bo) * g_s[sl, g]).astype(o_ref.dtype)
        return carry

    lax.fori_loop(0, T // C // 2, finish, 0)


def _rwkv(p_r, mu, w0, a0, k_k, k_a, r_k, gn_g, gn_b, w_up, a_up, g_up, batch, seq, tseq, ng):
    n = batch * seq
    G = ng * RWKV_GROUP
    nq = RWKV_WIDTH // G
    ns = seq // tseq
    lora_blk = 3 * RWKV_WIDTH // LORA_COLS

    def tok(off):
        return pl.BlockSpec((tseq, G), lambda b, q, s: (b * ns + s, off + q))

    def par(off):
        return pl.BlockSpec((1, G), lambda b, q, s: (0, off + q))

    in_specs = [
        tok(0), tok(nq), tok(2 * nq),
        pl.BlockSpec((tseq, LORA_COLS), lambda b, q, s: (b * ns + s, lora_blk)),
        par(0), par(nq), par(2 * nq),
        pl.BlockSpec((1, LORA_COLS), lambda b, q, s: (0, lora_blk)),
        par(0), par(0), par(0), par(0), par(0), par(0), par(0),
        pl.BlockSpec((LANES, G), lambda b, q, s: (0, q)),
        pl.BlockSpec((LANES, G), lambda b, q, s: (0, q)),
        pl.BlockSpec((2 * LANES, G), lambda b, q, s: (0, q)),
    ]
    scratch = [pltpu.VMEM((1, G), F32), pltpu.VMEM((1, G), F32), pltpu.VMEM((1, G), F32),
               pltpu.VMEM((1, LORA_COLS), F32), pltpu.VMEM((ng, RWKV_GROUP, RWKV_GROUP), F32)]
    scratch += [pltpu.VMEM((tseq, G), F32) for _ in range(8)]
    scratch += [pltpu.VMEM((tseq, G), BF16) for _ in range(8)]
    scratch += [pltpu.VMEM((tseq // RWKV_CHUNK, G), F32)]
    return pl.pallas_call(
        _rwkv_kernel,
        grid=(batch, nq, ns),
        in_specs=in_specs,
        out_specs=pl.BlockSpec((tseq, G), lambda b, q, s: (b * ns + s, q)),
        out_shape=jax.ShapeDtypeStruct((n, RWKV_WIDTH), BF16),
        scratch_shapes=scratch,
        compiler_params=_cparams(("parallel", "parallel", "arbitrary"), VMEM_LIMIT),
        name="rwkv7",
    )(p_r, p_r, p_r, p_r, mu, mu, mu, mu, w0, a0, k_k, k_a, r_k, gn_g, gn_b, w_up, a_up, g_up)


def _attn_kernel(q_ref, k_ref, v_ref, lq1_ref, lk1_ref, lq2_ref, lk2_ref, g_ref, o_ref, vt_s, *, lambda_init):
    i = pl.program_id(2)
    tq = q_ref.shape[0]

    @pl.when(i == 0)
    def _():
        ones = jnp.ones((ATTN_ONES_ROWS, tq), BF16)
        for j in range(vt_s.shape[0]):
            vt = v_ref[j * tq:(j + 1) * tq, :].astype(F32).T.astype(BF16)
            vt_s[j] = jnp.concatenate([vt, ones], axis=0)

    drow = lax.broadcasted_iota(jnp.int32, (DIFF_VDIM, 1), 0)
    qt = (q_ref[...].astype(F32) * (DIFF_HEAD ** -0.5 * LOG2E)).T
    q1 = jnp.where(drow < DIFF_HEAD, qt, 0.0).astype(BF16)
    q2 = jnp.where(drow >= DIFF_HEAD, qt, 0.0).astype(BF16)
    lam = (jnp.exp(jnp.sum(lq1_ref[...] * lk1_ref[...], axis=-1, keepdims=True))
           - jnp.exp(jnp.sum(lq2_ref[...] * lk2_ref[...], axis=-1, keepdims=True)) + lambda_init)

    key = lax.broadcasted_iota(jnp.int32, (tq, tq), 0)
    qry = lax.broadcasted_iota(jnp.int32, (tq, tq), 1)

    def scores(j, diagonal):
        kj = k_ref[j * tq:(j + 1) * tq, :]
        sc = (_dot(kj, q1), _dot(kj, q2))
        if diagonal:
            sc = tuple(jnp.where(key <= qry, s, NEG_BIG) for s in sc)
        return sc

    def absorb(j, sc, stats):
        m, acc = stats[0:2], stats[2:4]
        vtj = vt_s[j]
        m_new = [jnp.maximum(mm, jnp.max(s, axis=0, keepdims=True)) for mm, s in zip(m, sc)]
        alpha = [jnp.exp2(mm - mn) for mm, mn in zip(m, m_new)]
        p = [jnp.exp2(s - mn).astype(BF16) for s, mn in zip(sc, m_new)]
        pv = [_dot(vtj, pp) for pp in p]
        acc = [a * ac + x for a, ac, x in zip(alpha, acc, pv)]
        return tuple(m_new) + tuple(acc)

    neg1 = jnp.full((1, tq), NEG_BIG, F32)
    zacc = jnp.zeros((DIFF_VDIM + ATTN_ONES_ROWS, tq), F32)
    init = (neg1, neg1, zacc, zacc)

    for iv in range(vt_s.shape[0]):
        @pl.when(i == iv)
        def _(iv=iv):
            stats = init
            sc = scores(0, iv == 0)
            for j in range(iv + 1):
                nxt = scores(j + 1, j + 1 == iv) if j < iv else None
                stats = absorb(j, sc, stats)
                sc = nxt
            acc = [a[:DIFF_VDIM] for a in stats[2:4]]
            l = [a[DIFF_VDIM:DIFF_VDIM + 1] for a in stats[2:4]]
            o = acc[0] / l[0] - lam * (acc[1] / l[1])
            o = o * lax.rsqrt(jnp.mean(o * o, axis=0, keepdims=True) + RMS_EPS) * g_ref[...]
            o_ref[...] = (o * (1.0 - lambda_init)).T.astype(o_ref.dtype)


def _diff_attention(p_d, lq1, lk1, lq2, lk2, subln_g, lambda_init, batch, seq, tq):
    n = batch * seq
    nq = seq // tq
    H = DIFF_HEADS
    small = pl.BlockSpec((1, DIFF_HEAD), lambda b, h, i: (0, 0))
    return pl.pallas_call(
        functools.partial(_attn_kernel, lambda_init=lambda_init),
        grid=(batch, H, nq),
        in_specs=[pl.BlockSpec((tq, DIFF_VDIM), lambda b, h, i: (b * nq + i, h)),
                  pl.BlockSpec((seq, DIFF_VDIM), lambda b, h, i: (b, H + h)),
                  pl.BlockSpec((seq, DIFF_VDIM), lambda b, h, i: (b, 2 * H + h)),
                  small, small, small, small,
                  pl.BlockSpec((DIFF_VDIM, 1), lambda b, h, i: (0, 0))],
        out_specs=pl.BlockSpec((tq, DIFF_VDIM), lambda b, h, i: (b * nq + i, h)),
        out_shape=jax.ShapeDtypeStruct((n, DIFF_WIDTH), BF16),
        scratch_shapes=[pltpu.VMEM((nq, DIFF_VDIM + ATTN_ONES_ROWS, tq), BF16)],
        compiler_params=_cparams(("parallel", "parallel", "arbitrary"), VMEM_LIMIT),
        name="diff_attn",
    )(p_d, p_d, p_d, lq1, lk1, lq2, lk2, subln_g.reshape(DIFF_VDIM, 1))


def _layer_norm(y, g, b):
    mu = jnp.mean(y, axis=-1, keepdims=True)
    d = y - mu
    var = jnp.mean(d * d, axis=-1, keepdims=True)
    return d * lax.rsqrt(var + LN_EPS) * g + b


def _store_row_tiles(ref, words, start=0, sublane=0):
    rows = words.shape[0]
    for s in range(words.shape[1] // LANES):
        ref[pl.ds(start * ROW_TILE + sublane + s, rows, stride=ROW_TILE), :] = words[:, s * LANES:(s + 1) * LANES]


def _load_row_tiles(ref, start, rows):
    parts = [ref[pl.ds(start * ROW_TILE + s, rows, stride=ROW_TILE), :] for s in range(ROW_TILE)]
    return jnp.concatenate(parts, axis=1)


def _pack_pair(lo, hi):
    bits = lambda y: lax.bitcast_convert_type(y.astype(BF16).astype(F32), jnp.int32)
    return lax.shift_right_logical(bits(lo), 16) | (bits(hi) & HIGH_HALF)


def _pack_row(y):
    return _pack_pair(y[:, :ROW_WORDS], y[:, ROW_WORDS:])


def _unpack_row(words):
    lo = lax.bitcast_convert_type(lax.shift_left(words, 16), F32)
    hi = lax.bitcast_convert_type(words & HIGH_HALF, F32)
    return lo, hi


def _expert_onehots(idx):
    lane = lax.broadcasted_iota(jnp.int32, idx.shape, 1)
    sels = [lane == idx[:, kk:kk + 1] for kk in range(TOP_K)]
    onehot = jnp.zeros(idx.shape, F32)
    for sel in sels:
        onehot = onehot + jnp.where(sel, 1.0, 0.0)
    return sels, onehot


def _outproj_kernel(hr_ref, hd_ref, x_ref, wt_ref, wb_ref, g_ref, b_ref, wrh_ref, wrl_ref, br_ref,
                    x1_ref, x1p_ref, idx_ref, gate_ref, cnt_ref, carry_s):
    i = pl.program_id(0)
    tm = x_ref.shape[0]
    th = tm // OUTPROJ_PARTS
    parts = [pl.ds(h * th, th) for h in range(OUTPROJ_PARTS)]

    @pl.when(i == 0)
    def _():
        carry_s[...] = jnp.zeros_like(carry_s)

    def each(fn, *lists):
        return [fn(*xs) for xs in zip(*lists)]

    mix = [_dot(hr_ref[p, :], wt_ref[...]) + _dot(hd_ref[p, :], wb_ref[...]) for p in parts]
    x1 = [_layer_norm(DEEPNORM_ALPHA * x_ref[p, :] + m, g_ref[...], b_ref[...]) for p, m in zip(parts, mix)]
    for h, (p, y) in enumerate(zip(parts, x1)):
        x1_ref[p, :] = y
        _store_row_tiles(x1p_ref, _pack_row(y), h * th)

    xh = each(lambda y: y.astype(BF16), x1)
    xl = each(lambda y, hh: (y - hh.astype(F32)).astype(BF16), x1, xh)
    work = each(lambda hh, ll: _dot(hh, wrh_ref[...]) + _dot(ll, wrh_ref[...]) + _dot(hh, wrl_ref[...])
                + br_ref[...], xh, xl)
    lane = lax.broadcasted_iota(jnp.int32, (th, LANES), 1).astype(F32)
    onehot = [jnp.zeros((th, LANES), F32) for _ in parts]
    vals, idxs = [], []
    for _ in range(TOP_K):
        mx = each(lambda w: jnp.max(w, axis=-1, keepdims=True), work)
        idx = each(lambda w, m: jnp.min(jnp.where(w == m, lane, float(LANES)), axis=-1, keepdims=True), work, mx)
        sel = each(lambda ix: lane == ix, idx)
        work = each(lambda s, w: jnp.where(s, -jnp.inf, w), sel, work)
        onehot = each(lambda o, s: o + jnp.where(s, 1.0, 0.0), onehot, sel)
        vals.append(mx)
        idxs.append(idx)
    total = carry_s[0:1, :]
    for h, p in enumerate(parts):
        exps = [jnp.exp(vv[h] - vals[0][h]) for vv in vals]
        den = exps[0] + exps[1] + exps[2] + exps[3]
        idx_out = jnp.zeros((th, LANES), F32)
        gate_out = jnp.zeros((th, LANES), F32)
        for kk in range(TOP_K):
            slot = lane == float(kk)
            idx_out = jnp.where(slot, idxs[kk][h], idx_out)
            gate_out = jnp.where(slot, exps[kk] / den, gate_out)
        idx_ref[p, :] = idx_out.astype(jnp.int32)
        gate_ref[p, :] = gate_out
        total = total + jnp.sum(onehot[h], axis=0, keepdims=True)
    carry_s[...] = jnp.broadcast_to(total, carry_s.shape)
    cnt_ref[...] = jnp.broadcast_to(total, cnt_ref.shape)


def _outproj_ln_router(hr, hd, x, w_top, w_bot, g, b, w_r, b_r, tm):
    n = x.shape[0]
    w_rh = w_r.astype(BF16)
    w_rl = (w_r - w_rh.astype(F32)).astype(BF16)
    const = lambda shape: pl.BlockSpec(shape, lambda i: (0, 0))
    rowb = lambda cols: pl.BlockSpec((tm, cols), lambda i: (i, 0))
    out_shape = (jax.ShapeDtypeStruct((n, D_MODEL), F32),
                 jax.ShapeDtypeStruct((n * ROW_TILE, LANES), jnp.int32),
                 jax.ShapeDtypeStruct((n, LANES), jnp.int32),
                 jax.ShapeDtypeStruct((n, LANES), F32),
                 jax.ShapeDtypeStruct((SUBLANES, LANES), F32))
    return pl.pallas_call(
        _outproj_kernel,
        grid=(n // tm,),
        in_specs=[rowb(RWKV_WIDTH), rowb(DIFF_WIDTH), rowb(D_MODEL),
                  const((RWKV_WIDTH, D_MODEL)), const((DIFF_WIDTH, D_MODEL)),
                  const((1, D_MODEL)), const((1, D_MODEL)),
                  const((D_MODEL, LANES)), const((D_MODEL, LANES)), const((1, LANES))],
        out_specs=(rowb(D_MODEL), pl.BlockSpec((tm * ROW_TILE, LANES), lambda i: (i, 0)),
                   rowb(LANES), rowb(LANES), const((SUBLANES, LANES))),
        out_shape=out_shape,
        scratch_shapes=[pltpu.VMEM((SUBLANES, LANES), F32)],
        compiler_params=_cparams(("arbitrary",), VMEM_LIMIT),
        name="outproj_ln_router",
    )(hr, hd, x, w_top, w_bot, g, b, w_rh, w_rl, b_r)


def _dest_kernel(idx_ref, ps_ref, dest_ref, carry_s):
    i = pl.program_id(0)
    tm = idx_ref.shape[0]

    @pl.when(i == 0)
    def _():
        carry_s[...] = jnp.broadcast_to(ps_ref[...], carry_s.shape)

    sels, onehot = _expert_onehots(idx_ref[...])
    ti = lax.broadcasted_iota(jnp.int32, (tm, tm), 0)
    tj = lax.broadcasted_iota(jnp.int32, (tm, tm), 1)
    before = jnp.where(tj < ti, 1.0, 0.0).astype(BF16)
    cum = _dot(before, onehot.astype(BF16)) + carry_s[0:1, :]
    lane = lax.broadcasted_iota(jnp.int32, (tm, LANES), 1)
    dest = jnp.zeros((tm, LANES), F32)
    for kk in range(TOP_K):
        dk = jnp.sum(jnp.where(sels[kk], cum, 0.0), axis=-1, keepdims=True)
        dest = jnp.where(lane == kk, dk, dest)
    dest_ref[...] = dest.astype(jnp.int32)
    total = carry_s[0:1, :] + jnp.sum(onehot, axis=0, keepdims=True)
    carry_s[...] = jnp.broadcast_to(total, carry_s.shape)


def _row_dest(idx, pstarts, tm):
    n = idx.shape[0]
    return pl.pallas_call(
        _dest_kernel,
        grid=(n // tm,),
        in_specs=[pl.BlockSpec((tm, LANES), lambda i: (i, 0)), pl.BlockSpec((1, LANES), lambda i: (0, 0))],
        out_specs=pl.BlockSpec((tm, LANES), lambda i: (i, 0)),
        out_shape=jax.ShapeDtypeStruct((n, LANES), jnp.int32),
        scratch_shapes=[pltpu.VMEM((SUBLANES, LANES), F32)],
        compiler_params=_cparams(("arbitrary",)),
        name="moe_row_dest",
    )(idx, pstarts)


def _wait_rows(ref, rows, sem):
    view = ref.at[pl.ds(0, rows)]
    pltpu.make_async_copy(view, view, sem).wait()


def _dispatch_kernel(dest_ref, padoff_ref, padlen_ref, x_ref, xs_hbm, zero_s, sem, zsem):
    tt = x_ref.shape[0] // ROW_TILE

    def pad_copies(e, go):
        off = padoff_ref[e]
        ln = padlen_ref[e]
        for bit in range(zero_s.shape[0].bit_length()):
            size = 1 << bit

            @pl.when((ln >> bit) & 1 == 1)
            def _():
                go(pltpu.make_async_copy(zero_s.at[pl.ds(0, size)],
                                         xs_hbm.at[pl.ds(off + (ln & (size - 1)), size)], zsem))

    def tail_copy(piece):
        rows = zero_s.shape[0]
        first = pl.multiple_of(padoff_ref[N_EXPERTS] + piece * rows, rows)
        return pltpu.make_async_copy(zero_s, xs_hbm.at[pl.ds(first, rows)], zsem)

    @pl.when(pl.program_id(0) == 0)
    def _():
        zero_s[...] = jnp.zeros_like(zero_s)
        for go in (lambda cp: cp.start(), lambda cp: cp.wait()):
            def body(e, c, go=go):
                pad_copies(e, go)
                return c
            lax.fori_loop(0, N_EXPERTS, body, 0)

            def tail(piece, c, go=go):
                go(tail_copy(piece))
                return c
            lax.fori_loop(0, padlen_ref[N_EXPERTS], tail, 0)

    def issue(tp, c):
        for tl in range(DISPATCH_UNROLL):
            t = tp * DISPATCH_UNROLL + tl
            src = x_ref.at[pl.ds(pl.multiple_of(t * ROW_TILE, ROW_TILE), ROW_TILE), :]
            for kk in range(TOP_K):
                pltpu.make_async_copy(src, xs_hbm.at[dest_ref[t * TOP_K + kk]], sem).start(priority=kk % 2)
        return c

    lax.fori_loop(0, tt // DISPATCH_UNROLL, issue, 0)
    for _ in range(TOP_K):
        _wait_rows(xs_hbm, tt, sem)


def _dispatch(dest_flat, pad_off, pad_len, x1p2, n_rows, tt, bm):
    n = x1p2.shape[0] // ROW_TILE
    table = pl.BlockSpec((LANES,), lambda i: (0,), memory_space=pltpu.SMEM)
    return pl.pallas_call(
        _dispatch_kernel,
        grid=(n // tt,),
        in_specs=[pl.BlockSpec((tt * TOP_K,), lambda i: (i,), memory_space=pltpu.SMEM), table, table,
                  pl.BlockSpec((tt * ROW_TILE, LANES), lambda i: (i, 0))],
        out_specs=pl.BlockSpec(memory_space=pl.ANY),
        out_shape=jax.ShapeDtypeStruct((n_rows, ROW_TILE, LANES), jnp.int32),
        scratch_shapes=[pltpu.VMEM((bm // 2, ROW_TILE, LANES), jnp.int32),
                        pltpu.SemaphoreType.DMA, pltpu.SemaphoreType.DMA],
        compiler_params=_cparams(("arbitrary",)),
        name="moe_dispatch",
    )(dest_flat, pad_off, pad_len, x1p2)


def _dot_f32_weights(xk, w_ref, cols):
    acc = None
    for kc, x in enumerate(xk):
        part = _dot(x, w_ref[0, kc * MXU_DEPTH:(kc + 1) * MXU_DEPTH, cols].astype(BF16))
        acc = part if acc is None else acc + part
    return acc


def _for_valid_rows(nr, bm, body, o_ref):
    piece = bm // MOE_ROW_PIECES
    out_rows_per_row = o_ref.shape[0] // bm
    for q in range(1, MOE_ROW_PIECES + 1):
        @pl.when(jnp.logical_and(nr > (q - 1) * piece, nr <= q * piece))
        def _(first=bm - q * piece):
            body(first)
            if first > 0:
                o_ref[0:first * out_rows_per_row, :] = jnp.zeros(
                    (first * out_rows_per_row, o_ref.shape[1]), o_ref.dtype)

    @pl.when(nr == 0)
    def _():
        o_ref[...] = jnp.zeros_like(o_ref)


def _gu_kernel(be_ref, rows_ref, xs_ref, wg_ref, wu_ref, bg_ref, bu_ref, o_ref):
    del be_ref

    bm = o_ref.shape[0]

    def body(first):
        lo, hi = _unpack_row(_load_row_tiles(xs_ref, first, bm - first))
        x = jnp.concatenate([lo, hi], axis=1).astype(BF16)
        xk = [x[:, kc * MXU_DEPTH:(kc + 1) * MXU_DEPTH] for kc in range(D_MODEL // MXU_DEPTH)]
        g = _dot_f32_weights(xk, wg_ref, slice(None)) + bg_ref[0]
        u = _dot_f32_weights(xk, wu_ref, slice(None)) + bu_ref[0]
        g = jnp.minimum(g, SWIGLU_LIMIT)
        u = jnp.clip(u, -SWIGLU_LIMIT, SWIGLU_LIMIT)
        o_ref[first:, :] = ((u + 1.0) * (g * _sigmoid(SWIGLU_ALPHA * g))).astype(o_ref.dtype)

    _for_valid_rows(rows_ref[pl.program_id(1)], bm, body, o_ref)


def _moe_gate_up(block_e, blk_rows, xs2, w_gu, b_gu, bm, tn):
    n_rows = xs2.shape[0] // ROW_TILE
    n_blocks = n_rows // bm
    nt = D_FF // tn
    grid_spec = pltpu.PrefetchScalarGridSpec(
        num_scalar_prefetch=2,
        grid=(nt, n_blocks),
        in_specs=[pl.BlockSpec((bm * ROW_TILE, LANES), lambda n, j, be, nv: (j, 0)),
                  pl.BlockSpec((1, D_MODEL, tn), lambda n, j, be, nv: (be[j], 0, n)),
                  pl.BlockSpec((1, D_MODEL, tn), lambda n, j, be, nv: (be[j], 0, nt + n)),
                  pl.BlockSpec((1, 1, tn), lambda n, j, be, nv: (be[j], 0, n)),
                  pl.BlockSpec((1, 1, tn), lambda n, j, be, nv: (be[j], 0, nt + n))],
        out_specs=pl.BlockSpec((bm, tn), lambda n, j, be, nv: (j, n)))
    return pl.pallas_call(
        _gu_kernel,
        grid_spec=grid_spec,
        out_shape=jax.ShapeDtypeStruct((n_rows, D_FF), BF16),
        compiler_params=_cparams(("arbitrary", "arbitrary"), VMEM_LIMIT),
        name="moe_gate_up",
    )(block_e, blk_rows, xs2, w_gu, w_gu, b_gu, b_gu)


def _dn_kernel(be_ref, rows_ref, a_ref, w_ref, b_ref, o_ref):
    del be_ref

    def body(first):
        ak = [a_ref[first:, kc * MXU_DEPTH:(kc + 1) * MXU_DEPTH] for kc in range(D_FF // MXU_DEPTH)]
        hw = ROW_WORDS // 2
        for h in range(2):
            lo = slice(h * hw, (h + 1) * hw)
            hi = slice(ROW_WORDS + h * hw, ROW_WORDS + (h + 1) * hw)
            y_lo = _dot_f32_weights(ak, w_ref, lo) + b_ref[0, :, lo]
            y_hi = _dot_f32_weights(ak, w_ref, hi) + b_ref[0, :, hi]
            _store_row_tiles(o_ref, _pack_pair(y_lo, y_hi), first, h * (hw // LANES))

    _for_valid_rows(rows_ref[pl.program_id(0)], a_ref.shape[0], body, o_ref)


def _moe_down(block_e, blk_rows, act, w_dn, b_dn, bm):
    n_rows = act.shape[0]
    n_blocks = n_rows // bm
    grid_spec = pltpu.PrefetchScalarGridSpec(
        num_scalar_prefetch=2,
        grid=(n_blocks,),
        in_specs=[pl.BlockSpec((bm, D_FF), lambda j, be, nv: (j, 0)),
                  pl.BlockSpec((1, D_FF, D_MODEL), lambda j, be, nv: (be[j], 0, 0)),
                  pl.BlockSpec((1, 1, D_MODEL), lambda j, be, nv: (be[j], 0, 0))],
        out_specs=pl.BlockSpec((bm * ROW_TILE, LANES), lambda j, be, nv: (j, 0)))
    return pl.pallas_call(
        _dn_kernel,
        grid_spec=grid_spec,
        out_shape=jax.ShapeDtypeStruct((n_rows * ROW_TILE, LANES), jnp.int32),
        compiler_params=_cparams(("arbitrary",), VMEM_LIMIT),
        name="moe_down",
    )(block_e, blk_rows, act, w_dn, b_dn)


def _combine_kernel(dest_ref, next_ref, gate_ref, x1_ref, g_ref, b_ref, ys_hbm, o_ref, buf_a, buf_b, sem):
    i = pl.program_id(0)
    tt = x1_ref.shape[0]
    sub = tt // COMBINE_SUBBLOCKS
    bufs = (buf_a, buf_b)

    def start_rows(d_ref, s, t):
        for kk in range(TOP_K):
            row = pl.multiple_of((kk * tt + t) * ROW_TILE, ROW_TILE)
            pltpu.make_async_copy(ys_hbm.at[d_ref[t * TOP_K + kk]],
                                  bufs[s].at[pl.ds(row, ROW_TILE), :], sem.at[s]).start(priority=kk % 2)

    def wait_tile(s):
        for kk in range(TOP_K):
            view = bufs[s].at[pl.ds(kk * tt * ROW_TILE, tt * ROW_TILE), :]
            pltpu.make_async_copy(view, view, sem.at[s]).wait()

    @pl.when(i == 0)
    def _():
        def body(t, c):
            start_rows(dest_ref, 0, t)
            return c
        lax.fori_loop(0, tt, body, 0)

    def step(cur, nxt):
        wait_tile(cur)

        def reduce_rows(sb, c):
            r0 = pl.multiple_of(sb * sub, sub)
            for tl in range(sub):
                start_rows(next_ref, nxt, r0 + tl)
            rows = pl.ds(r0, sub)
            gates = gate_ref[rows, :]
            acc_lo = jnp.zeros((sub, ROW_WORDS), F32)
            acc_hi = jnp.zeros((sub, ROW_WORDS), F32)
            for kk in range(TOP_K):
                lo, hi = _unpack_row(_load_row_tiles(bufs[cur], kk * tt + r0, sub))
                gk = gates[:, kk:kk + 1]
                acc_lo = acc_lo + gk * lo
                acc_hi = acc_hi + gk * hi
            ffn = jnp.concatenate([acc_lo, acc_hi], axis=1)
            o_ref[rows, :] = _layer_norm(DEEPNORM_ALPHA * x1_ref[rows, :] + ffn, g_ref[...], b_ref[...])
            return c

        lax.fori_loop(0, COMBINE_SUBBLOCKS, reduce_rows, 0)

        @pl.when(i + 1 == pl.num_programs(0))
        def _():
            wait_tile(nxt)

    for cur in range(2):
        @pl.when(i % 2 == cur)
        def _(cur=cur):
            step(cur, 1 - cur)


def _combine_ln(dest_flat, gates, x1, g, b, ys3, tt):
    n = x1.shape[0]
    last = n // tt - 1
    const = lambda shape: pl.BlockSpec(shape, lambda i: (0, 0))
    return pl.pallas_call(
        _combine_kernel,
        grid=(n // tt,),
        in_specs=[pl.BlockSpec((tt * TOP_K,), lambda i: (i,), memory_space=pltpu.SMEM),
                  pl.BlockSpec((tt * TOP_K,), lambda i: (jnp.minimum(i + 1, last),), memory_space=pltpu.SMEM),
                  pl.BlockSpec((tt, LANES), lambda i: (i, 0)),
                  pl.BlockSpec((tt, D_MODEL), lambda i: (i, 0)),
                  const((1, D_MODEL)), const((1, D_MODEL)),
                  pl.BlockSpec(memory_space=pl.ANY)],
        out_specs=pl.BlockSpec((tt, D_MODEL), lambda i: (i, 0)),
        out_shape=jax.ShapeDtypeStruct((n, D_MODEL), F32),
        scratch_shapes=[pltpu.VMEM((TOP_K * tt * ROW_TILE, LANES), jnp.int32),
                        pltpu.VMEM((TOP_K * tt * ROW_TILE, LANES), jnp.int32),
                        pltpu.SemaphoreType.DMA((2,))],
        compiler_params=_cparams(("arbitrary",), VMEM_LIMIT),
        name="moe_combine_ln",
    )(dest_flat, dest_flat, gates, x1, g, b, ys3)


def _pad_cols(a, width):
    return jnp.pad(a, ((0, 0), (0, width - a.shape[1])))


def _lora_layout(a):
    dw = a[:, :DECAY_LORA]
    da = a[:, DECAY_LORA:DECAY_LORA + AAA_LORA]
    dg = a[:, DECAY_LORA + AAA_LORA:]
    return jnp.concatenate([_pad_cols(dw, LANES), _pad_cols(da, LANES), _pad_cols(dg, 2 * LANES)], axis=1)


def _moe_ffn(x1, x1p2, idx, gates, counts, w_gu, b_gu, w_dn, b_dn, ln_g, ln_b, bm, tn, tt):
    n = x1.shape[0]
    nk = n * TOP_K
    n_blocks = nk // bm + N_EXPERTS
    n_rows = n_blocks * bm
    cnt = counts[0, :N_EXPERTS].astype(jnp.int32)
    padded = ((cnt + bm - 1) // bm) * bm
    pends = jnp.cumsum(padded)
    lanes = lambda a: jnp.pad(a, (0, LANES - a.shape[0]))
    vstart = pends - cnt
    pad_off = lanes(jnp.concatenate([pends - padded, pends[-1:]])).astype(jnp.int32)
    pad_len = lanes(jnp.concatenate([padded - cnt, (n_rows - pends[-1:]) // (bm // 2)])).astype(jnp.int32)
    block_start = jnp.arange(n_blocks, dtype=jnp.int32)[:, None] * bm
    block_e = jnp.minimum(jnp.sum(pends[None, :] <= block_start, axis=1), N_EXPERTS - 1).astype(jnp.int32)
    owned = jnp.logical_and(block_start >= (pends - padded)[None, :], block_start < pends[None, :])
    blk_rows = jnp.sum(jnp.where(owned, jnp.clip(block_start + bm - vstart[None, :], 0, bm), 0),
                       axis=1).astype(jnp.int32)

    dest_flat = _row_dest(idx, lanes(vstart).astype(F32).reshape(1, LANES), tt)[:, :TOP_K].reshape(nk)
    xs3 = _dispatch(dest_flat, pad_off, pad_len, x1p2, n_rows, tt, bm)
    act = _moe_gate_up(block_e, blk_rows, xs3.reshape(n_rows * ROW_TILE, LANES), w_gu,
                       b_gu.reshape(N_EXPERTS, 1, 2 * D_FF), bm, tn)
    ys2 = _moe_down(block_e, blk_rows, act, w_dn, b_dn.reshape(N_EXPERTS, 1, D_MODEL), bm)
    return _combine_ln(dest_flat, gates, x1, ln_g, ln_b, ys2.reshape(n_rows, ROW_TILE, LANES), tt)


def _layer(x, w_in, shift_mu, w0, w_up, a0, a_up, g_up, k_k, k_a, r_k, gn_g, gn_b,
           lq1, lk1, lq2, lk2, subln_g, w_out, ln1_g, ln1_b,
           w_router, b_router, w_gu, b_gu, w_dn, b_dn, ln2_g, ln2_b, lambda_init,
           tm_in=2048, tseq=512, ng=4, tq=512, tm_out=512, bm=512, tn=1024, tt=512):
    batch, seq, d = x.shape
    n = batch * seq
    rw = 3 * RWKV_WIDTH
    rcols = rw + DECAY_LORA + AAA_LORA + GATE_LORA
    row = lambda a: a.reshape(1, -1)

    xf = x.reshape(n, d)
    xb = xf.astype(BF16)
    w_r = jnp.concatenate([w_in[:, :rw], _lora_layout(w_in[:, rw:rcols])], axis=1).astype(BF16)
    w_d = w_in[:, rcols:].astype(BF16)
    mu = jnp.concatenate([row(shift_mu)[:, :rw], _lora_layout(row(shift_mu)[:, rw:])], axis=1)
    p_r = _matmul(xb, w_r, F32, tm_in, 512)
    p_d = _matmul(xb, w_d, BF16, tm_in, 512)

    pad_rows = lambda a, rows: jnp.pad(a, ((0, rows - a.shape[0]), (0, 0))).astype(BF16)
    h_r = _rwkv(p_r, mu, row(w0), row(a0), row(k_k), row(k_a), row(r_k), row(gn_g), row(gn_b),
                pad_rows(w_up, LANES), pad_rows(a_up, LANES), pad_rows(g_up, 2 * LANES), batch, seq, tseq, ng)
    h_d = _diff_attention(p_d, row(lq1), row(lk1), row(lq2), row(lk2), row(subln_g), lambda_init,
                          batch, seq, tq)

    w_ob = w_out.astype(BF16)
    w_rp = _pad_cols(w_router, LANES)
    b_rp = jnp.concatenate([row(b_router), jnp.full((1, LANES - N_EXPERTS), NEG_BIG, F32)], axis=1)
    x1, x1p2, idx, gates, counts = _outproj_ln_router(
        h_r, h_d, xf, w_ob[:RWKV_WIDTH], w_ob[RWKV_WIDTH:], row(ln1_g), row(ln1_b), w_rp, b_rp, tm_out)
    out = _moe_ffn(x1, x1p2, idx, gates, counts, w_gu, b_gu, w_dn, b_dn, row(ln2_g), row(ln2_b), bm, tn, tt)
    return out.reshape(batch, seq, d)


def kernel(x, w_in, shift_mu, w0, w_up, a0, a_up, g_up, k_k, k_a, r_k, gn_g, gn_b, lq1, lk1, lq2, lk2,
           subln_g, w_out, ln1_g, ln1_b, w_router, b_router, w_gu, b_gu, w_dn, b_dn, ln2_g, ln2_b):
    for l in range(DEPTH):
        lambda_init = 0.8 - 0.6 * math.exp(-0.3 * l)
        x = _layer(x, w_in[l], shift_mu[l], w0[l], w_up[l], a0[l], a_up[l], g_up[l], k_k[l], k_a[l],
                   r_k[l], gn_g[l], gn_b[l], lq1[l], lk1[l], lq2[l], lk2[l], subln_g[l], w_out[l],
                   ln1_g[l], ln1_b[l], w_router[l], b_router[l], w_gu[l], b_gu[l], w_dn[l], b_dn[l],
                   ln2_g[l], ln2_b[l], lambda_init)
    return x
```
